```python
import jax
import jax.numpy as jnp
from jax import lax
import numpy as np

D_MODEL = 1024
BATCH = 4
SEQ = 4096
DEPTH = 4
DEC_BATCH = 128
DEC_SEQ = 1
PAST_LEN = 8192
PAGE_SIZE = 128

N_MIXERS = 3
N_MLA_LAYERS = (DEPTH + 2) // 3
N_MLSTM_LAYERS = (DEPTH + 1) // 3
N_NSA_LAYERS = DEPTH // 3

MLA_HEADS = 16
MLA_NOPE_DIM = 64
MLA_ROPE_DIM = 32
MLA_V_DIM = 64
MLA_Q_LORA = 384
MLA_KV_LORA = 256
MLA_A_COLS = MLA_Q_LORA + MLA_KV_LORA + MLA_ROPE_DIM
MLA_SCALE = (MLA_NOPE_DIM + MLA_ROPE_DIM) ** -0.5
ROPE_THETA = 10000.0

MLSTM_HEADS = 4
MLSTM_DQK = 128
MLSTM_DV = 256
MLSTM_CHUNK = 64
MLSTM_IN_COLS = 2 * MLSTM_HEADS * MLSTM_DQK + 2 * MLSTM_HEADS * MLSTM_DV + 2 * MLSTM_HEADS
FORGET_BIAS = 3.0

NSA_HEADS = 16
NSA_KV_HEADS = 4
NSA_HEAD_DIM = 64
CMP_BLOCK = 64
SEL_BLOCK = 64
TOP_K_BLOCKS = 16
WINDOW = 512
NSA_Q_COLS = NSA_HEADS * NSA_HEAD_DIM
NSA_KV_COLS = NSA_KV_HEADS * 2 * NSA_HEAD_DIM
NSA_IN_COLS = NSA_Q_COLS + 3 * NSA_KV_COLS + 3 * NSA_HEADS
NSA_SCALE = NSA_HEAD_DIM ** -0.5

D_FF = 4 * D_MODEL
Q_BLOCK = 128
RMS_EPS = 1e-6
NEG = -1e30
FORCE = 1e4

kernel_name = 'hybrid_mla_mlstm_nsa_step'

F32 = jnp.float32


def _rmsnorm(x, g):
    xf = x.astype(F32)
    y = xf * lax.rsqrt(jnp.mean(xf * xf, axis=-1, keepdims=True) + RMS_EPS)
    return (y * g.astype(F32)).astype(x.dtype)


def _mlp(h, w1, w2):
    a = jnp.maximum(h @ w1, 0)
    return ((a * a) @ w2).astype(h.dtype)


def _rope(x, pos):
    half = x.shape[-1] // 2
    freqs = ROPE_THETA ** (-jnp.arange(half, dtype=F32) / half)
    ang = pos.astype(F32)[:, None] * freqs[None, :]
    cos = jnp.cos(ang)[None, :, None, :]
    sin = jnp.sin(ang)[None, :, None, :]
    xf = x.astype(F32)
    x1, x2 = xf[..., :half], xf[..., half:]
    return jnp.concatenate([x1 * cos - x2 * sin, x1 * sin + x2 * cos], axis=-1).astype(x.dtype)


def _alibi_slopes(n):
    return 2.0 ** (-8.0 * jnp.arange(1, n + 1, dtype=F32) / n)


def _to_blocks(a, axis):
    n = a.shape[axis] // Q_BLOCK
    a = a.reshape(a.shape[:axis] + (n, Q_BLOCK) + a.shape[axis + 1:])
    return jnp.moveaxis(a, axis, 0)


def _from_blocks(a, axis):
    a = jnp.moveaxis(a, 0, axis)
    return a.reshape(a.shape[:axis] + (-1,) + a.shape[axis + 2:])


def _attend(q, ks, vs, kposs, q_pos, scale, slopes=None, window=None):
    B, Tq, H, Dk = q.shape
    kvh = ks[0].shape[2]
    G = H // kvh
    qg = q.reshape(B, Tq, kvh, G, Dk)
    scores = []
    for k, kp in zip(ks, kposs):
        s = jnp.einsum('bqgnd,bkgd->bgnqk', qg, k).astype(F32) * scale
        dist = q_pos[:, None] - kp[None, :]
        mask = (dist >= 0) & (kp[None, :] >= 0)
        if window is not None:
            mask = mask & (dist < window)
        if slopes is not None:
            s = s - slopes.reshape(kvh, G)[None, :, :, None, None] * dist.astype(F32)
        scores.append(jnp.where(mask, s, NEG))
    p = jax.nn.softmax(jnp.concatenate(scores, axis=-1), axis=-1)
    out, off = 0.0, 0
    for v in vs:
        n = v.shape[1]
        out = out + jnp.einsum('bgnqk,bkgd->bqgnd', p[..., off:off + n].astype(v.dtype), v)
        off += n
    return out.reshape(B, Tq, H, -1)


def _mla_project(h, pos, w_a, g_q, g_kv, w_uq):
    B, T, _ = h.shape
    a = h @ w_a
    cq = _rmsnorm(a[..., :MLA_Q_LORA], g_q)
    ckv = _rmsnorm(a[..., MLA_Q_LORA:MLA_Q_LORA + MLA_KV_LORA], g_kv)
    kpe = _rope(a[..., MLA_Q_LORA + MLA_KV_LORA:][:, :, None, :], pos)
    q = (cq @ w_uq).reshape(B, T, MLA_HEADS, MLA_NOPE_DIM + MLA_ROPE_DIM)
    q_nope = q[..., :MLA_NOPE_DIM]
    q_pe = _rope(q[..., MLA_NOPE_DIM:], pos)
    return q_nope, q_pe, ckv, kpe


def _mla_prompt(h, w_a, g_q, g_kv, w_uq, w_uk, w_uv, w_o):
    B, S, _ = h.shape
    pos = jnp.arange(S, dtype=jnp.int32)
    q_nope, q_pe, ckv, kpe = _mla_project(h, pos, w_a, g_q, g_kv, w_uq)
    k_nope = jnp.einsum('btc,chn->bthn', ckv, w_uk)
    v = jnp.einsum('btc,chv->bthv', ckv, w_uv)
    k = jnp.concatenate([k_nope, jnp.broadcast_to(kpe, (B, S, MLA_HEADS, MLA_ROPE_DIM))], axis=-1)
    q = jnp.concatenate([q_nope, q_pe], axis=-1)

    def blk(args):
        qb, pb = args
        return _attend(qb, [k], [v], [pos], pb, MLA_SCALE)

    o = _from_blocks(lax.map(blk, (_to_blocks(q, 1), _to_blocks(pos, 0))), 1)
    y = (o.reshape(B, S, -1) @ w_o).astype(h.dtype)
    new_rows = jnp.concatenate([ckv, kpe[:, :, 0]], axis=-1)
    return y, new_rows


def _mla_sample(h, cache, j, page_table, w_a, g_q, g_kv, w_uq, w_uk, w_uv, w_o):
    B, T, _ = h.shape
    past_len = page_table.shape[1] * cache.shape[2]
    pos = past_len + jnp.arange(T, dtype=jnp.int32)
    q_nope, q_pe, ckv, kpe = _mla_project(h, pos, w_a, g_q, g_kv, w_uq)
    new_rows = jnp.concatenate([ckv, kpe[:, :, 0]], axis=-1)
    past = cache[j, page_table].reshape(B, past_len, 1, -1)
    new = new_rows[:, :, None, :]
    q_lat = jnp.einsum('bthn,chn->bthc', q_nope, w_uk)
    q = jnp.concatenate([q_lat, q_pe], axis=-1)
    o_lat = _attend(q, [past, new], [past[..., :MLA_KV_LORA], new[..., :MLA_KV_LORA]],
                    [jnp.arange(past_len, dtype=jnp.int32), pos], pos, MLA_SCALE)
    o = jnp.einsum('bthc,chv->bthv', o_lat, w_uv)
    y = (o.reshape(B, T, -1) @ w_o).astype(h.dtype)
    return y, new_rows


def _mlstm_project(h, w_in, b_gate):
    B, T, _ = h.shape
    qk = MLSTM_HEADS * MLSTM_DQK
    hv = MLSTM_HEADS * MLSTM_DV
    z = h @ w_in
    q = z[..., :qk].reshape(B, T, MLSTM_HEADS, MLSTM_DQK)
    k = z[..., qk:2 * qk].reshape(B, T, MLSTM_HEADS, MLSTM_DQK) * (MLSTM_DQK ** -0.5)
    v = z[..., 2 * qk:2 * qk + hv].reshape(B, T, MLSTM_HEADS, MLSTM_DV)
    o = jax.nn.sigmoid(z[..., 2 * qk + hv:2 * qk + 2 * hv].astype(F32)).reshape(B, T, MLSTM_HEADS, MLSTM_DV)
    gt = z[..., 2 * qk + 2 * hv:].astype(F32) + b_gate.astype(F32)
    ig = gt[..., :MLSTM_HEADS]
    lf = jax.nn.log_sigmoid(gt[..., MLSTM_HEADS:])
    return q, k, v, o, ig, lf


def _mlstm_chunkwise(q, k, v, ig, lf, c0, n0, m0):
    B, T, H, _ = q.shape
    L = MLSTM_CHUNK if T % MLSTM_CHUNK == 0 else T
    nc = T // L

    def split(a):
        a = a.astype(F32).reshape((B, nc, L) + a.shape[2:])
        return jnp.moveaxis(a, 1, 0)

    causal = jnp.tril(jnp.ones((L, L), dtype=bool))

    def step(carry, xs):
        c, n, m = carry
        qc, kc, vc, ic, fc = xs
        b = jnp.cumsum(fc, axis=1).transpose(0, 2, 1)
        it = ic.transpose(0, 2, 1)
        d = jnp.where(causal, b[:, :, :, None] - b[:, :, None, :] + it[:, :, None, :], NEG)
        inter = b + m[:, :, None]
        mt = jnp.maximum(inter, d.max(axis=-1))
        w = jnp.exp(d - mt[..., None])
        g = jnp.exp(inter - mt)
        a = w * jnp.einsum('bthd,bshd->bhts', qc, kc)
        num = jnp.einsum('bhts,bshe->bthe', a, vc) + jnp.einsum('bht,bhed,bthd->bthe', g, c, qc)
        den = a.sum(axis=-1) + g * jnp.einsum('bhd,bthd->bht', n, qc)
        hc = num / jnp.maximum(jnp.abs(den), jnp.exp(-mt)).transpose(0, 2, 1)[..., None]
        m_new = mt[:, :, -1]
        wl = jnp.exp(b[:, :, -1:] - b + it - m_new[:, :, None])
        gl = jnp.exp(b[:, :, -1] + m - m_new)
        c_new = gl[:, :, None, None] * c + jnp.einsum('bhs,bshe,bshd->bhed', wl, vc, kc)
        n_new = gl[:, :, None] * n + jnp.einsum('bhs,bshd->bhd', wl, kc)
        return (c_new, n_new, m_new), hc

    carry0 = (c0.astype(F32), n0.astype(F32), m0.astype(F32))
    (c, n, m), hs = lax.scan(step, carry0, (split(q), split(k), split(v), split(ig), split(lf)))
    hs = jnp.moveaxis(hs, 0, 1).reshape(B, T, H, -1)
    return hs, c, n, m


def _mlstm_mix(h, c0, n0, m0, w_in, b_gate, w_out):
    B, T, _ = h.shape
    q, k, v, o, ig, lf = _mlstm_project(h, w_in, b_gate)
    hs, c, n, m = _mlstm_chunkwise(q, k, v, ig, lf, c0, n0, m0)
    y = ((hs * o).reshape(B, T, -1) @ w_out).astype(h.dtype)
    return y, c, n, m


def _nsa_project(h, w_in):
    B, T, _ = h.shape
    z = h @ w_in
    q = z[..., :NSA_Q_COLS].reshape(B, T, NSA_HEADS, NSA_HEAD_DIM)
    kv = z[..., NSA_Q_COLS:NSA_Q_COLS + 3 * NSA_KV_COLS].reshape(B, T, 3, NSA_KV_HEADS, 2, NSA_HEAD_DIM)
    g = jax.nn.sigmoid(z[..., NSA_Q_COLS + 3 * NSA_KV_COLS:].astype(F32)).reshape(B, T, NSA_HEADS, 3)
    return q, kv[:, :, 0], kv[:, :, 1], kv[:, :, 2], g


def _summarize(kv, w_cmp):
    B, T = kv.shape[:2]
    nb = T // CMP_BLOCK
    blk = kv[:, :nb * CMP_BLOCK].reshape(B, nb, CMP_BLOCK, NSA_KV_HEADS, 2, NSA_HEAD_DIM)
    return jnp.einsum('bnlgcd,clde->bngce', blk, w_cmp)


def _nsa_compressed(q, summ, q_pos, slopes):
    B, T, H, Dh = q.shape
    nbc = summ.shape[1]
    G = H // NSA_KV_HEADS
    qg = q.reshape(B, T, NSA_KV_HEADS, G, Dh)
    s = jnp.einsum('btgnd,bjgd->bgntj', qg, summ[..., 0, :]).astype(F32) * NSA_SCALE
    end = (jnp.arange(nbc, dtype=jnp.int32) + 1) * CMP_BLOCK - 1
    dist = q_pos[:, None] - end[None, :]
    mask = dist >= 0
    s = jnp.where(mask, s - slopes.reshape(NSA_KV_HEADS, G)[None, :, :, None, None] * dist.astype(F32), NEG)
    p = jnp.where(mask, jax.nn.softmax(s, axis=-1), 0.0)
    o = jnp.einsum('bgntj,bjgd->btgnd', p.astype(q.dtype), summ[..., 1, :]).reshape(B, T, H, Dh)
    return o, p.sum(axis=2)


def _nsa_select(imp, q_pos, nb):
    nbc = imp.shape[-1]
    score = jnp.pad(imp, ((0, 0), (0, 0), (0, 0), (0, nb - nbc)))
    j = jnp.arange(nb, dtype=jnp.int32)[None, :]
    cur = (q_pos // SEL_BLOCK)[:, None]
    forced = (j == 0) | (j == cur) | (j == cur - 1)
    score = jnp.where(forced, FORCE, jnp.where(j <= cur, score, -FORCE))
    return lax.top_k(score, min(TOP_K_BLOCKS, nb))[1]


def _sel_attn(q, q_pos, kg, vg, idx, slopes):
    B, Tq, H, Dh = q.shape
    G = H // NSA_KV_HEADS
    qg = q.reshape(B, Tq, NSA_KV_HEADS, G, Dh)
    s = jnp.einsum('btgnd,bgtkld->bgntkl', qg, kg).astype(F32) * NSA_SCALE
    rows = idx[..., None] * SEL_BLOCK + jnp.arange(SEL_BLOCK, dtype=jnp.int32)
    dist = (q_pos[None, None, :, None, None] - rows)[:, :, None]
    sl = slopes.reshape(NSA_KV_HEADS, G)[None, :, :, None, None, None]
    s = jnp.where(dist >= 0, s - sl * dist.astype(F32), NEG)
    shp = s.shape
    p = jax.nn.softmax(s.reshape(shp[:4] + (-1,)), axis=-1).reshape(shp)
    o = jnp.einsum('bgntkl,bgtkld->btgnd', p.astype(vg.dtype), vg)
    return o.reshape(B, Tq, H, Dh)


def _nsa_merge(g, o_cmp, o_sel, o_win, w_out, dtype):
    o = g[..., 0:1] * o_cmp.astype(F32) + g[..., 1:2] * o_sel.astype(F32) + g[..., 2:3] * o_win.astype(F32)
    B, T = o.shape[:2]
    return (o.reshape(B, T, -1) @ w_out).astype(dtype)


def _nsa_prompt(h, w_in, w_cmp, w_out):
    B, S, _ = h.shape
    pos = jnp.arange(S, dtype=jnp.int32)
    slopes = _alibi_slopes(NSA_HEADS)
    q, kv_c, kv_s, kv_w, g = _nsa_project(h, w_in)
    summ = _summarize(kv_c, w_cmp)
    o_cmp, imp = _nsa_compressed(q, summ, pos, slopes)
    nb = S // SEL_BLOCK
    idx = _nsa_select(imp, pos, nb)
    sel = kv_s.reshape(B, nb, SEL_BLOCK, NSA_KV_HEADS, 2, NSA_HEAD_DIM).transpose(0, 3, 1, 2, 4, 5)
    win = jnp.pad(kv_w, ((0, 0), (WINDOW, 0), (0, 0), (0, 0), (0, 0)))
    bi = jnp.arange(B)[:, None, None, None]
    gi = jnp.arange(NSA_KV_HEADS)[None, :, None, None]
    band = WINDOW + Q_BLOCK

    def blk(args):
        i, qb, pb, ib = args
        kvg = sel[bi, gi, ib]
        o_sel = _sel_attn(qb, pb, kvg[..., 0, :], kvg[..., 1, :], ib, slopes)
        wb = lax.dynamic_slice_in_dim(win, i * Q_BLOCK, band, axis=1)
        kpos = i * Q_BLOCK - WINDOW + jnp.arange(band, dtype=jnp.int32)
        o_win = _attend(qb, [wb[..., 0, :]], [wb[..., 1, :]], [kpos], pb, NSA_SCALE, slopes, WINDOW)
        return o_sel, o_win

    nq = S // Q_BLOCK
    o_sel, o_win = lax.map(blk, (jnp.arange(nq, dtype=jnp.int32), _to_blocks(q, 1),
                                 _to_blocks(pos, 0), _to_blocks(idx, 2)))
    y = _nsa_merge(g, o_cmp, _from_blocks(o_sel, 1), _from_blocks(o_win, 1), w_out, h.dtype)
    return y, kv_c, kv_s, kv_w[:, -min(WINDOW, S):]


def _nsa_sample(h, cache_cmp, cache_sel, win_state, j, page_table, w_in, w_cmp, w_out):
    B, T, _ = h.shape
    n_pages = page_table.shape[1]
    page = cache_cmp.shape[2]
    past_len = n_pages * page
    total = past_len + T
    pos = past_len + jnp.arange(T, dtype=jnp.int32)
    slopes = _alibi_slopes(NSA_HEADS)
    q, kv_c, kv_s, kv_w, g = _nsa_project(h, w_in)
    past_c = cache_cmp[j, page_table].reshape(B, past_len, NSA_KV_HEADS, 2, NSA_HEAD_DIM)
    summ = jnp.concatenate([_summarize(past_c, w_cmp), _summarize(kv_c, w_cmp)], axis=1)
    o_cmp, imp = _nsa_compressed(q, summ, pos, slopes)
    nb = -(-total // SEL_BLOCK)
    idx = _nsa_select(imp, pos, nb)
    rows = idx[..., None] * SEL_BLOCK + jnp.arange(SEL_BLOCK, dtype=jnp.int32)
    bi = jnp.arange(B)[:, None, None, None, None]
    gi = jnp.arange(NSA_KV_HEADS)[None, :, None, None, None]
    phys = page_table[bi, jnp.minimum(rows // page, n_pages - 1)]
    from_past = cache_sel[j, phys, rows % page, gi]
    from_new = kv_s[bi, jnp.clip(rows - past_len, 0, T - 1), gi]
    kvg = jnp.where((rows < past_len)[..., None, None], from_past, from_new)
    o_sel = _sel_attn(q, pos, kvg[..., 0, :], kvg[..., 1, :], idx, slopes)
    wbuf = win_state.shape[1]
    o_win = _attend(q, [win_state[..., 0, :], kv_w[..., 0, :]], [win_state[..., 1, :], kv_w[..., 1, :]],
                    [past_len - wbuf + jnp.arange(wbuf, dtype=jnp.int32), pos], pos, NSA_SCALE, slopes, WINDOW)
    y = _nsa_merge(g, o_cmp, o_sel, o_win, w_out, h.dtype)
    new_win = jnp.concatenate([win_state, kv_w], axis=1)[:, -min(WINDOW, wbuf + T):]
    return y, kv_c, kv_s, new_win


def setup_inputs(seed: int = 0) -> dict:
    key = jax.random.key(seed)
    keys = jax.random.split(key, 40)
    counter = [0]

    def nxt():
        k = keys[counter[0]]
        counter[0] += 1
        return k

    def nrm(shape, scale=1.0):
        return jax.random.normal(nxt(), shape, F32) * scale

    n_pages = PAST_LEN // PAGE_SIZE
    n_pool = (DEC_BATCH * n_pages * 5) // 4
    win_buf = min(WINDOW, PAST_LEN)
    hv = MLSTM_HEADS * MLSTM_DV
    gate_bias_center = jnp.concatenate([jnp.zeros((MLSTM_HEADS,), F32),
                                        jnp.full((MLSTM_HEADS,), FORGET_BIAS, F32)])
    x_prompt = nrm((BATCH, SEQ, D_MODEL))
    x_sample = nrm((DEC_BATCH, DEC_SEQ, D_MODEL))
    cache_mla_kv = nrm((N_MLA_LAYERS, n_pool, PAGE_SIZE, MLA_KV_LORA + MLA_ROPE_DIM))
    state_mlstm_c = nrm((N_MLSTM_LAYERS, DEC_BATCH, MLSTM_HEADS, MLSTM_DV, MLSTM_DQK), 0.1)
    state_mlstm_n = nrm((N_MLSTM_LAYERS, DEC_BATCH, MLSTM_HEADS, MLSTM_DQK), 0.1)
    state_mlstm_m = nrm((N_MLSTM_LAYERS, DEC_BATCH, MLSTM_HEADS))
    cache_nsa_cmp = nrm((N_NSA_LAYERS, n_pool, PAGE_SIZE, NSA_KV_HEADS, 2, NSA_HEAD_DIM))
    cache_nsa_sel = nrm((N_NSA_LAYERS, n_pool, PAGE_SIZE, NSA_KV_HEADS, 2, NSA_HEAD_DIM))
    state_nsa_win = nrm((N_NSA_LAYERS, DEC_BATCH, win_buf, NSA_KV_HEADS, 2, NSA_HEAD_DIM))
    page_table = jax.random.permutation(nxt(), n_pool)[:DEC_BATCH * n_pages].reshape(
        DEC_BATCH, n_pages).astype(jnp.int32)
    return {
        'x_prompt': x_prompt,
        'x_sample': x_sample,
        'cache_mla_kv': cache_mla_kv,
        'state_mlstm_c': state_mlstm_c,
        'state_mlstm_n': state_mlstm_n,
        'state_mlstm_m': state_mlstm_m,
        'cache_nsa_cmp': cache_nsa_cmp,
        'cache_nsa_sel': cache_nsa_sel,
        'state_nsa_win': state_nsa_win,
        'page_table': page_table,
        'norm_g': 1.0 + nrm((DEPTH, 2, D_MODEL), 0.01),
        'final_norm_g': 1.0 + nrm((D_MODEL,), 0.01),
        'mla_w_a': nrm((N_MLA_LAYERS, D_MODEL, MLA_A_COLS), D_MODEL ** -0.5),
        'mla_g_q': 1.0 + nrm((N_MLA_LAYERS, MLA_Q_LORA), 0.01),
        'mla_g_kv': 1.0 + nrm((N_MLA_LAYERS, MLA_KV_LORA), 0.01),
        'mla_w_uq': nrm((N_MLA_LAYERS, MLA_Q_LORA, MLA_HEADS * (MLA_NOPE_DIM + MLA_ROPE_DIM)), MLA_Q_LORA ** -0.5),
        'mla_w_uk': nrm((N_MLA_LAYERS, MLA_KV_LORA, MLA_HEADS, MLA_NOPE_DIM), MLA_KV_LORA ** -0.5),
        'mla_w_uv': nrm((N_MLA_LAYERS, MLA_KV_LORA, MLA_HEADS, MLA_V_DIM), MLA_KV_LORA ** -0.5),
        'mla_w_o': nrm((N_MLA_LAYERS, MLA_HEADS * MLA_V_DIM, D_MODEL), (MLA_HEADS * MLA_V_DIM) ** -0.5),
        'mlstm_w_in': nrm((N_MLSTM_LAYERS, D_MODEL, MLSTM_IN_COLS), D_MODEL ** -0.5),
        'mlstm_b_gate': gate_bias_center + nrm((N_MLSTM_LAYERS, 2 * MLSTM_HEADS), 0.1),
        'mlstm_w_out': nrm((N_MLSTM_LAYERS, hv, D_MODEL), hv ** -0.5),
        'nsa_w_in': nrm((N_NSA_LAYERS, D_MODEL, NSA_IN_COLS), D_MODEL ** -0.5),
        'nsa_w_cmp': nrm((N_NSA_LAYERS, 2, CMP_BLOCK, NSA_HEAD_DIM, NSA_HEAD_DIM), (CMP_BLOCK * NSA_HEAD_DIM) ** -0.5),
        'nsa_w_out': nrm((N_NSA_LAYERS, NSA_Q_COLS, D_MODEL), NSA_Q_COLS ** -0.5),
        'mlp_w1': nrm((DEPTH, D_MODEL, D_FF), D_MODEL ** -0.5),
        'mlp_w2': nrm((DEPTH, D_FF, D_MODEL), D_FF ** -0.5),
    }


def reference(x_prompt, x_sample, cache_mla_kv, state_mlstm_c, state_mlstm_n, state_mlstm_m,
              cache_nsa_cmp, cache_nsa_sel, state_nsa_win, page_table,
              norm_g, final_norm_g, mla_w_a, mla_g_q, mla_g_kv, mla_w_uq, mla_w_uk, mla_w_uv, mla_w_o,
              mlstm_w_in, mlstm_b_gate, mlstm_w_out, nsa_w_in, nsa_w_cmp, nsa_w_out, mlp_w1, mlp_w2):
    xp, xs = x_prompt, x_sample
    B = xp.shape[0]
    mla_p, mla_s = [], []
    mc_p, mn_p, mm_p, mc_s, mn_s, mm_s = [], [], [], [], [], []
    cmp_p, cmp_s, sel_p, sel_s, win_p, win_s = [], [], [], [], [], []
    for i in range(DEPTH):
        j = i // N_MIXERS
        hp = _rmsnorm(xp, norm_g[i, 0])
        hs = _rmsnorm(xs, norm_g[i, 0])
        if i % N_MIXERS == 0:
            w = (mla_w_a[j], mla_g_q[j], mla_g_kv[j], mla_w_uq[j], mla_w_uk[j], mla_w_uv[j], mla_w_o[j])
            yp, rp = _mla_prompt(hp, *w)
            ys, rs = _mla_sample(hs, cache_mla_kv, j, page_table, *w)
            mla_p.append(rp)
            mla_s.append(rs)
        elif i % N_MIXERS == 1:
            c0 = jnp.zeros((B, MLSTM_HEADS, MLSTM_DV, MLSTM_DQK), F32)
            n0 = jnp.zeros((B, MLSTM_HEADS, MLSTM_DQK), F32)
            m0 = jnp.zeros((B, MLSTM_HEADS), F32)
            yp, cp, nst_p, mp = _mlstm_mix(hp, c0, n0, m0, mlstm_w_in[j], mlstm_b_gate[j], mlstm_w_out[j])
            ys, cs, nst_s, ms = _mlstm_mix(hs, state_mlstm_c[j], state_mlstm_n[j], state_mlstm_m[j],
                                           mlstm_w_in[j], mlstm_b_gate[j], mlstm_w_out[j])
            mc_p.append(cp)
            mn_p.append(nst_p)
            mm_p.append(mp)
            mc_s.append(cs)
            mn_s.append(nst_s)
            mm_s.append(ms)
        else:
            yp, kcp, ksp, kwp = _nsa_prompt(hp, nsa_w_in[j], nsa_w_cmp[j], nsa_w_out[j])
            ys, kcs, kss, kws = _nsa_sample(hs, cache_nsa_cmp, cache_nsa_sel, state_nsa_win[j], j, page_table,
                                            nsa_w_in[j], nsa_w_cmp[j], nsa_w_out[j])
            cmp_p.append(kcp)
            cmp_s.append(kcs)
            sel_p.append(ksp)
            sel_s.append(kss)
            win_p.append(kwp)
            win_s.append(kws)
        xp = xp + yp
        xs = xs + ys
        xp = xp + _mlp(_rmsnorm(xp, norm_g[i, 1]), mlp_w1[i], mlp_w2[i])
        xs = xs + _mlp(_rmsnorm(xs, norm_g[i, 1]), mlp_w1[i], mlp_w2[i])
    y_prompt = _rmsnorm(xp, final_norm_g)
    y_sample = _rmsnorm(xs, final_norm_g)
    return (y_prompt, y_sample,
            jnp.stack(mla_p), jnp.stack(mla_s),
            jnp.stack(mc_p), jnp.stack(mn_p), jnp.stack(mm_p),
            jnp.stack(mc_s), jnp.stack(mn_s), jnp.stack(mm_s),
            jnp.stack(cmp_p), jnp.stack(cmp_s),
            jnp.stack(sel_p), jnp.stack(sel_s),
            jnp.stack(win_p), jnp.stack(win_s))
```

```python
import functools

import jax
import jax.numpy as jnp
from jax import lax
from jax.experimental import pallas as pl
from jax.experimental.pallas import tpu as pltpu

F32 = jnp.float32
BF16 = jnp.bfloat16

D_MODEL = 1024
DEPTH = 4
N_MIXERS = 3
PAGE_SIZE = 128

MLA_HEADS = 16
MLA_NOPE_DIM = 64
MLA_ROPE_DIM = 32
MLA_V_DIM = 64
MLA_Q_LORA = 384
MLA_KV_LORA = 256
MLA_SCALE = (MLA_NOPE_DIM + MLA_ROPE_DIM) ** -0.5
ROPE_THETA = 10000.0

MLSTM_HEADS = 4
MLSTM_DQK = 128
MLSTM_DV = 256
MLSTM_CHUNK = 64

NSA_HEADS = 16
NSA_KV_HEADS = 4
NSA_HEAD_DIM = 64
CMP_BLOCK = 64
SEL_BLOCK = 64
TOP_K_BLOCKS = 16
WINDOW = 512
NSA_Q_COLS = NSA_HEADS * NSA_HEAD_DIM
NSA_KV_COLS = NSA_KV_HEADS * 2 * NSA_HEAD_DIM
NSA_SCALE = NSA_HEAD_DIM ** -0.5

D_FF = 4 * D_MODEL
Q_BLOCK = 128
RMS_EPS = 1e-6
NEG = -1e30
FORCE = 1e4

VMEM_LIMIT_BYTES = 56 * 1024 * 1024
FF_CHUNK = 512


def _row_tile(m):
    for t in (512, 256, 128):
        if m % t == 0:
            return t
    return m


def _rms_rows(x, g):
    return x * lax.rsqrt(jnp.mean(x * x, axis=-1, keepdims=True) + RMS_EPS) * g


def _norm_proj_body(x_ref, g_ref, w_ref, o_ref):
    h = _rms_rows(x_ref[...], g_ref[...]).astype(BF16)
    o_ref[...] = jnp.dot(h, w_ref[...], preferred_element_type=F32)


def _norm_proj(x, g, w):
    m, d = x.shape
    n = w.shape[1]
    tm = _row_tile(m)
    return pl.pallas_call(
        _norm_proj_body,
        grid=(m // tm,),
        in_specs=[pl.BlockSpec((tm, d), lambda i: (i, 0)),
                  pl.BlockSpec((1, d), lambda i: (0, 0)),
                  pl.BlockSpec((d, n), lambda i: (0, 0))],
        out_specs=pl.BlockSpec((tm, n), lambda i: (i, 0)),
        out_shape=jax.ShapeDtypeStruct((m, n), F32),
        compiler_params=pltpu.CompilerParams(dimension_semantics=("arbitrary",),
                                             vmem_limit_bytes=VMEM_LIMIT_BYTES),
        name="norm_proj",
    )(x, g.reshape(1, d), w.astype(BF16))


def _proj_res_body(a_ref, w_ref, r_ref, o_ref):
    o_ref[...] = r_ref[...] + jnp.dot(a_ref[...].astype(BF16), w_ref[...], preferred_element_type=F32)


def _proj_res(a, w, res):
    m, k = a.shape
    d = w.shape[1]
    tm = _row_tile(m)
    return pl.pallas_call(
        _proj_res_body,
        grid=(m // tm,),
        in_specs=[pl.BlockSpec((tm, k), lambda i: (i, 0)),
                  pl.BlockSpec((k, d), lambda i: (0, 0)),
                  pl.BlockSpec((tm, d), lambda i: (i, 0))],
        out_specs=pl.BlockSpec((tm, d), lambda i: (i, 0)),
        out_shape=jax.ShapeDtypeStruct((m, d), F32),
        compiler_params=pltpu.CompilerParams(dimension_semantics=("arbitrary",),
                                             vmem_limit_bytes=VMEM_LIMIT_BYTES),
        name="proj_res",
    )(a, w.astype(BF16), res)


def _mlp_body(x_ref, g_ref, w1_ref, w2_ref, o_ref):
    x = x_ref[...]
    h = _rms_rows(x, g_ref[...]).astype(BF16)
    acc = x
    for c in range(D_FF // FF_CHUNK):
        a = jnp.dot(h, w1_ref[:, c * FF_CHUNK:(c + 1) * FF_CHUNK], preferred_element_type=F32)
        a = jnp.maximum(a, 0.0)
        acc = acc + jnp.dot((a * a).astype(BF16), w2_ref[c * FF_CHUNK:(c + 1) * FF_CHUNK, :],
                            preferred_element_type=F32)
    o_ref[...] = acc


def _mlp_res(x, g, w1, w2):
    m, d = x.shape
    tm = _row_tile(m)
    return pl.pallas_call(
        _mlp_body,
        grid=(m // tm,),
        in_specs=[pl.BlockSpec((tm, d), lambda i: (i, 0)),
                  pl.BlockSpec((1, d), lambda i: (0, 0)),
                  pl.BlockSpec((d, D_FF), lambda i: (0, 0)),
                  pl.BlockSpec((D_FF, d), lambda i: (0, 0))],
        out_specs=pl.BlockSpec((tm, d), lambda i: (i, 0)),
        out_shape=jax.ShapeDtypeStruct((m, d), F32),
        compiler_params=pltpu.CompilerParams(dimension_semantics=("arbitrary",),
                                             vmem_limit_bytes=VMEM_LIMIT_BYTES),
        name="mlp_res",
    )(x, g.reshape(1, d), w1.astype(BF16), w2.astype(BF16))


def _final_norm_body(x_ref, g_ref, o_ref):
    o_ref[...] = _rms_rows(x_ref[...], g_ref[...])


def _final_norm(x, g):
    m, d = x.shape
    tm = _row_tile(m)
    return pl.pallas_call(
        _final_norm_body,
        grid=(m // tm,),
        in_specs=[pl.BlockSpec((tm, d), lambda i: (i, 0)), pl.BlockSpec((1, d), lambda i: (0, 0))],
        out_specs=pl.BlockSpec((tm, d), lambda i: (i, 0)),
        out_shape=jax.ShapeDtypeStruct((m, d), F32),
        compiler_params=pltpu.CompilerParams(dimension_semantics=("arbitrary",)),
        name="final_norm",
    )(x, g.reshape(1, d))


def _rmsnorm(x, g):
    return x * lax.rsqrt(jnp.mean(x * x, axis=-1, keepdims=True) + RMS_EPS) * g


def _rope(x, pos):
    half = x.shape[-1] // 2
    freqs = ROPE_THETA ** (-jnp.arange(half, dtype=F32) / half)
    ang = pos.astype(F32)[:, None] * freqs[None, :]
    cos = jnp.cos(ang)[None, :, None, :]
    sin = jnp.sin(ang)[None, :, None, :]
    x1, x2 = x[..., :half], x[..., half:]
    return jnp.concatenate([x1 * cos - x2 * sin, x1 * sin + x2 * cos], axis=-1)


def _alibi_slopes(n):
    return 2.0 ** (-8.0 * jnp.arange(1, n + 1, dtype=F32) / n)


def _to_blocks(a, axis):
    n = a.shape[axis] // Q_BLOCK
    a = a.reshape(a.shape[:axis] + (n, Q_BLOCK) + a.shape[axis + 1:])
    return jnp.moveaxis(a, axis, 0)


def _from_blocks(a, axis):
    a = jnp.moveaxis(a, 0, axis)
    return a.reshape(a.shape[:axis] + (-1,) + a.shape[axis + 2:])


def _attend(q, ks, vs, kposs, q_pos, scale, slopes=None, window=None):
    B, Tq, H, Dk = q.shape
    kvh = ks[0].shape[2]
    G = H // kvh
    qg = q.reshape(B, Tq, kvh, G, Dk)
    scores = []
    for k, kp in zip(ks, kposs):
        s = jnp.einsum('bqgnd,bkgd->bgnqk', qg, k).astype(F32) * scale
        dist = q_pos[:, None] - kp[None, :]
        mask = (dist >= 0) & (kp[None, :] >= 0)
        if window is not None:
            mask = mask & (dist < window)
        if slopes is not None:
            s = s - slopes.reshape(kvh, G)[None, :, :, None, None] * dist.astype(F32)
        scores.append(jnp.where(mask, s, NEG))
    p = jax.nn.softmax(jnp.concatenate(scores, axis=-1), axis=-1)
    out, off = 0.0, 0
    for v in vs:
        n = v.shape[1]
        out = out + jnp.einsum('bgnqk,bkgd->bqgnd', p[..., off:off + n], v)
        off += n
    return out.reshape(B, Tq, H, -1)


def _mla_project(a, pos, g_q, g_kv, w_uq):
    B, T, _ = a.shape
    cq = _rmsnorm(a[..., :MLA_Q_LORA], g_q)
    ckv = _rmsnorm(a[..., MLA_Q_LORA:MLA_Q_LORA + MLA_KV_LORA], g_kv)
    kpe = _rope(a[..., MLA_Q_LORA + MLA_KV_LORA:][:, :, None, :], pos)
    q = (cq @ w_uq).reshape(B, T, MLA_HEADS, MLA_NOPE_DIM + MLA_ROPE_DIM)
    q_nope = q[..., :MLA_NOPE_DIM]
    q_pe = _rope(q[..., MLA_NOPE_DIM:], pos)
    return q_nope, q_pe, ckv, kpe


def _mla_prompt(a, g_q, g_kv, w_uq, w_uk, w_uv):
    B, S, _ = a.shape
    pos = jnp.arange(S, dtype=jnp.int32)
    q_nope, q_pe, ckv, kpe = _mla_project(a, pos, g_q, g_kv, w_uq)
    k_nope = jnp.einsum('btc,chn->bthn', ckv, w_uk)
    v = jnp.einsum('btc,chv->bthv', ckv, w_uv)
    k = jnp.concatenate([k_nope, jnp.broadcast_to(kpe, (B, S, MLA_HEADS, MLA_ROPE_DIM))], axis=-1)
    q = jnp.concatenate([q_nope, q_pe], axis=-1)

    def blk(args):
        qb, pb = args
        return _attend(qb, [k], [v], [pos], pb, MLA_SCALE)

    o = _from_blocks(lax.map(blk, (_to_blocks(q, 1), _to_blocks(pos, 0))), 1)
    new_rows = jnp.concatenate([ckv, kpe[:, :, 0]], axis=-1)
    return o.reshape(B, S, -1), new_rows


def _mla_sample(a, cache, j, page_table, g_q, g_kv, w_uq, w_uk, w_uv):
    B, T, _ = a.shape
    past_len = page_table.shape[1] * cache.shape[2]
    pos = past_len + jnp.arange(T, dtype=jnp.int32)
    q_nope, q_pe, ckv, kpe = _mla_project(a, pos, g_q, g_kv, w_uq)
    new_rows = jnp.concatenate([ckv, kpe[:, :, 0]], axis=-1)
    past = cache[j, page_table].reshape(B, past_len, 1, -1)
    new = new_rows[:, :, None, :]
    q_lat = jnp.einsum('bthn,chn->bthc', q_nope, w_uk)
    q = jnp.concatenate([q_lat, q_pe], axis=-1)
    o_lat = _attend(q, [past, new], [past[..., :MLA_KV_LORA], new[..., :MLA_KV_LORA]],
                    [jnp.arange(past_len, dtype=jnp.int32), pos], pos, MLA_SCALE)
    o = jnp.einsum('bthc,chv->bthv', o_lat, w_uv)
    return o.reshape(B, T, -1), new_rows


def _mlstm_split(z, b_gate):
    B, T, _ = z.shape
    qk = MLSTM_HEADS * MLSTM_DQK
    hv = MLSTM_HEADS * MLSTM_DV
    q = z[..., :qk].reshape(B, T, MLSTM_HEADS, MLSTM_DQK)
    k = z[..., qk:2 * qk].reshape(B, T, MLSTM_HEADS, MLSTM_DQK) * (MLSTM_DQK ** -0.5)
    v = z[..., 2 * qk:2 * qk + hv].reshape(B, T, MLSTM_HEADS, MLSTM_DV)
    o = jax.nn.sigmoid(z[..., 2 * qk + hv:2 * qk + 2 * hv]).reshape(B, T, MLSTM_HEADS, MLSTM_DV)
    gt = z[..., 2 * qk + 2 * hv:] + b_gate
    ig = gt[..., :MLSTM_HEADS]
    lf = jax.nn.log_sigmoid(gt[..., MLSTM_HEADS:])
    return q, k, v, o, ig, lf


def _mlstm_chunkwise(q, k, v, ig, lf, c0, n0, m0):
    B, T, H, _ = q.shape
    L = MLSTM_CHUNK if T % MLSTM_CHUNK == 0 else T
    nc = T // L

    def split(a):
        a = a.reshape((B, nc, L) + a.shape[2:])
        return jnp.moveaxis(a, 1, 0)

    causal = jnp.tril(jnp.ones((L, L), dtype=bool))

    def step(carry, xs):
        c, n, m = carry
        qc, kc, vc, ic, fc = xs
        b = jnp.cumsum(fc, axis=1).transpose(0, 2, 1)
        it = ic.transpose(0, 2, 1)
        d = jnp.where(causal, b[:, :, :, None] - b[:, :, None, :] + it[:, :, None, :], NEG)
        inter = b + m[:, :, None]
        mt = jnp.maximum(inter, d.max(axis=-1))
        w = jnp.exp(d - mt[..., None])
        g = jnp.exp(inter - mt)
        a = w * jnp.einsum('bthd,bshd->bhts', qc, kc)
        num = jnp.einsum('bhts,bshe->bthe', a, vc) + jnp.einsum('bht,bhed,bthd->bthe', g, c, qc)
        den = a.sum(axis=-1) + g * jnp.einsum('bhd,bthd->bht', n, qc)
        hc = num / jnp.maximum(jnp.abs(den), jnp.exp(-mt)).transpose(0, 2, 1)[..., None]
        m_new = mt[:, :, -1]
        wl = jnp.exp(b[:, :, -1:] - b + it - m_new[:, :, None])
        gl = jnp.exp(b[:, :, -1] + m - m_new)
        c_new = gl[:, :, None, None] * c + jnp.einsum('bhs,bshe,bshd->bhed', wl, vc, kc)
        n_new = gl[:, :, None] * n + jnp.einsum('bhs,bshd->bhd', wl, kc)
        return (c_new, n_new, m_new), hc

    (c, n, m), hs = lax.scan(step, (c0, n0, m0), (split(q), split(k), split(v), split(ig), split(lf)))
    hs = jnp.moveaxis(hs, 0, 1).reshape(B, T, H, -1)
    return hs, c, n, m


def _mlstm_mix(z, c0, n0, m0, b_gate):
    B, T, _ = z.shape
    q, k, v, o, ig, lf = _mlstm_split(z, b_gate)
    hs, c, n, m = _mlstm_chunkwise(q, k, v, ig, lf, c0, n0, m0)
    return (hs * o).reshape(B, T, -1), c, n, m


def _nsa_split(z):
    B, T, _ = z.shape
    q = z[..., :NSA_Q_COLS].reshape(B, T, NSA_HEADS, NSA_HEAD_DIM)
    kv = z[..., NSA_Q_COLS:NSA_Q_COLS + 3 * NSA_KV_COLS].reshape(B, T, 3, NSA_KV_HEADS, 2, NSA_HEAD_DIM)
    g = jax.nn.sigmoid(z[..., NSA_Q_COLS + 3 * NSA_KV_COLS:]).reshape(B, T, NSA_HEADS, 3)
    return q, kv[:, :, 0], kv[:, :, 1], kv[:, :, 2], g


def _summarize(kv, w_cmp):
    B, T = kv.shape[:2]
    nb = T // CMP_BLOCK
    blk = kv[:, :nb * CMP_BLOCK].reshape(B, nb, CMP_BLOCK, NSA_KV_HEADS, 2, NSA_HEAD_DIM)
    return jnp.einsum('bnlgcd,clde->bngce', blk, w_cmp)


def _nsa_compressed(q, summ, q_pos, slopes):
    B, T, H, Dh = q.shape
    nbc = summ.shape[1]
    G = H // NSA_KV_HEADS
    qg = q.reshape(B, T, NSA_KV_HEADS, G, Dh)
    s = jnp.einsum('btgnd,bjgd->bgntj', qg, summ[..., 0, :]) * NSA_SCALE
    end = (jnp.arange(nbc, dtype=jnp.int32) + 1) * CMP_BLOCK - 1
    dist = q_pos[:, None] - end[None, :]
    mask = dist >= 0
    s = jnp.where(mask, s - slopes.reshape(NSA_KV_HEADS, G)[None, :, :, None, None] * dist.astype(F32), NEG)
    p = jnp.where(mask, jax.nn.softmax(s, axis=-1), 0.0)
    o = jnp.einsum('bgntj,bjgd->btgnd', p, summ[..., 1, :]).reshape(B, T, H, Dh)
    return o, p.sum(axis=2)


def _nsa_select(imp, q_pos, nb):
    nbc = imp.shape[-1]
    score = jnp.pad(imp, ((0, 0), (0, 0), (0, 0), (0, nb - nbc)))
    j = jnp.arange(nb, dtype=jnp.int32)[None, :]
    cur = (q_pos // SEL_BLOCK)[:, None]
    forced = (j == 0) | (j == cur) | (j == cur - 1)
    score = jnp.where(forced, FORCE, jnp.where(j <= cur, score, -FORCE))
    return lax.top_k(score, min(TOP_K_BLOCKS, nb))[1]


def _sel_attn(q, q_pos, kg, vg, idx, slopes):
    B, Tq, H, Dh = q.shape
    G = H // NSA_KV_HEADS
    qg = q.reshape(B, Tq, NSA_KV_HEADS, G, Dh)
    s = jnp.einsum('btgnd,bgtkld->bgntkl', qg, kg) * NSA_SCALE
    rows = idx[..., None] * SEL_BLOCK + jnp.arange(SEL_BLOCK, dtype=jnp.int32)
    dist = (q_pos[None, None, :, None, None] - rows)[:, :, None]
    sl = slopes.reshape(NSA_KV_HEADS, G)[None, :, :, None, None, None]
    s = jnp.where(dist >= 0, s - sl * dist.astype(F32), NEG)
    shp = s.shape
    p = jax.nn.softmax(s.reshape(shp[:4] + (-1,)), axis=-1).reshape(shp)
    o = jnp.einsum('bgntkl,bgtkld->btgnd', p, vg)
    return o.reshape(B, Tq, H, Dh)


def _nsa_merge(g, o_cmp, o_sel, o_win):
    o = g[..., 0:1] * o_cmp + g[..., 1:2] * o_sel + g[..., 2:3] * o_win
    B, T = o.shape[:2]
    return o.reshape(B, T, -1)


def _nsa_prompt(z, w_cmp):
    B, S, _ = z.shape
    pos = jnp.arange(S, dtype=jnp.int32)
    slopes = _alibi_slopes(NSA_HEADS)
    q, kv_c, kv_s, kv_w, g = _nsa_split(z)
    summ = _summarize(kv_c, w_cmp)
    o_cmp, imp = _nsa_compressed(q, summ, pos, slopes)
    nb = S // SEL_BLOCK
    idx = _nsa_select(imp, pos, nb)
    sel = kv_s.reshape(B, nb, SEL_BLOCK, NSA_KV_HEADS, 2, NSA_HEAD_DIM).transpose(0, 3, 1, 2, 4, 5)
    win = jnp.pad(kv_w, ((0, 0), (WINDOW, 0), (0, 0), (0, 0), (0, 0)))
    bi = jnp.arange(B)[:, None, None, None]
    gi = jnp.arange(NSA_KV_HEADS)[None, :, None, None]
    band = WINDOW + Q_BLOCK

    def blk(args):
        i, qb, pb, ib = args
        kvg = sel[bi, gi, ib]
        o_sel = _sel_attn(qb, pb, kvg[..., 0, :], kvg[..., 1, :], ib, slopes)
        wb = lax.dynamic_slice_in_dim(win, i * Q_BLOCK, band, axis=1)
        kpos = i * Q_BLOCK - WINDOW + jnp.arange(band, dtype=jnp.int32)
        o_win = _attend(qb, [wb[..., 0, :]], [wb[..., 1, :]], [kpos], pb, NSA_SCALE, slopes, WINDOW)
        return o_sel, o_win

    nq = S // Q_BLOCK
    o_sel, o_win = lax.map(blk, (jnp.arange(nq, dtype=jnp.int32), _to_blocks(q, 1),
                                 _to_blocks(pos, 0), _to_blocks(idx, 2)))
    y = _nsa_merge(g, o_cmp, _from_blocks(o_sel, 1), _from_blocks(o_win, 1))
    return y, kv_c, kv_s, kv_w[:, -min(WINDOW, S):]


def _nsa_sample(z, cache_cmp, cache_sel, win_state, j, page_table, w_cmp):
    B, T, _ = z.shape
    n_pages = page_table.shape[1]
    page = cache_cmp.shape[2]
    past_len = n_pages * page
    total = past_len + T
    pos = past_len + jnp.arange(T, dtype=jnp.int32)
    slopes = _alibi_slopes(NSA_HEADS)
    q, kv_c, kv_s, kv_w, g = _nsa_split(z)
    past_c = cache_cmp[j, page_table].reshape(B, past_len, NSA_KV_HEADS, 2, NSA_HEAD_DIM)
    summ = jnp.concatenate([_summarize(past_c, w_cmp), _summarize(kv_c, w_cmp)], axis=1)
    o_cmp, imp = _nsa_compressed(q, summ, pos, slopes)
    nb = -(-total // SEL_BLOCK)
    idx = _nsa_select(imp, pos, nb)
    rows = idx[..., None] * SEL_BLOCK + jnp.arange(SEL_BLOCK, dtype=jnp.int32)
    bi = jnp.arange(B)[:, None, None, None, None]
    gi = jnp.arange(NSA_KV_HEADS)[None, :, None, None, None]
    phys = page_table[bi, jnp.minimum(rows // page, n_pages - 1)]
    from_past = cache_sel[j, phys, rows % page, gi]
    from_new = kv_s[bi, jnp.clip(rows - past_len, 0, T - 1), gi]
    kvg = jnp.where((rows < past_len)[..., None, None], from_past, from_new)
    o_sel = _sel_attn(q, pos, kvg[..., 0, :], kvg[..., 1, :], idx, slopes)
    wbuf = win_state.shape[1]
    o_win = _attend(q, [win_state[..., 0, :], kv_w[..., 0, :]], [win_state[..., 1, :], kv_w[..., 1, :]],
                    [past_len - wbuf + jnp.arange(wbuf, dtype=jnp.int32), pos], pos, NSA_SCALE, slopes, WINDOW)
    y = _nsa_merge(g, o_cmp, o_sel, o_win)
    new_win = jnp.concatenate([win_state, kv_w], axis=1)[:, -min(WINDOW, wbuf + T):]
    return y, kv_c, kv_s, new_win


def kernel(x_prompt, x_sample, cache_mla_kv, state_mlstm_c, state_mlstm_n, state_mlstm_m, cache_nsa_cmp,
           cache_nsa_sel, state_nsa_win, page_table, norm_g, final_norm_g, mla_w_a, mla_g_q, mla_g_kv,
           mla_w_uq, mla_w_uk, mla_w_uv, mla_w_o, mlstm_w_in, mlstm_b_gate, mlstm_w_out, nsa_w_in,
           nsa_w_cmp, nsa_w_out, mlp_w1, mlp_w2):
    B, S, D = x_prompt.shape
    Bs, Ts, _ = x_sample.shape
    xp = x_prompt.reshape(B * S, D)
    xs = x_sample.reshape(Bs * Ts, D)
    mla_p, mla_s = [], []
    mc_p, mn_p, mm_p, mc_s, mn_s, mm_s = [], [], [], [], [], []
    cmp_p, cmp_s, sel_p, sel_s, win_p, win_s = [], [], [], [], [], []
    for i in range(DEPTH):
        j = i // N_MIXERS
        g0 = norm_g[i, 0]
        if i % N_MIXERS == 0:
            ap = _norm_proj(xp, g0, mla_w_a[j]).reshape(B, S, -1)
            as_ = _norm_proj(xs, g0, mla_w_a[j]).reshape(Bs, Ts, -1)
            w = (mla_g_q[j], mla_g_kv[j], mla_w_uq[j], mla_w_uk[j], mla_w_uv[j])
            op, rp = _mla_prompt(ap, *w)
            os_, rs = _mla_sample(as_, cache_mla_kv, j, page_table, *w)
            mla_p.append(rp)
            mla_s.append(rs)
            w_out = mla_w_o[j]
        elif i % N_MIXERS == 1:
            zp = _norm_proj(xp, g0, mlstm_w_in[j]).reshape(B, S, -1)
            zs = _norm_proj(xs, g0, mlstm_w_in[j]).reshape(Bs, Ts, -1)
            c0 = jnp.zeros((B, MLSTM_HEADS, MLSTM_DV, MLSTM_DQK), F32)
            n0 = jnp.zeros((B, MLSTM_HEADS, MLSTM_DQK), F32)
            m0 = jnp.zeros((B, MLSTM_HEADS), F32)
            op, cp, nst_p, mp = _mlstm_mix(zp, c0, n0, m0, mlstm_b_gate[j])
            os_, cs, nst_s, ms = _mlstm_mix(zs, state_mlstm_c[j], state_mlstm_n[j], state_mlstm_m[j],
                                            mlstm_b_gate[j])
            mc_p.append(cp)
            mn_p.append(nst_p)
            mm_p.append(mp)
            mc_s.append(cs)
            mn_s.append(nst_s)
            mm_s.append(ms)
            w_out = mlstm_w_out[j]
        else:
            zp = _norm_proj(xp, g0, nsa_w_in[j]).reshape(B, S, -1)
            zs = _norm_proj(xs, g0, nsa_w_in[j]).reshape(Bs, Ts, -1)
            op, kcp, ksp, kwp = _nsa_prompt(zp, nsa_w_cmp[j])
            os_, kcs, kss, kws = _nsa_sample(zs, cache_nsa_cmp, cache_nsa_sel, state_nsa_win[j], j,
                                             page_table, nsa_w_cmp[j])
            cmp_p.append(kcp)
            cmp_s.append(kcs)
            sel_p.append(ksp)
            sel_s.append(kss)
            win_p.append(kwp)
            win_s.append(kws)
            w_out = nsa_w_out[j]
        xp = _proj_res(op.reshape(B * S, -1), w_out, xp)
        xs = _proj_res(os_.reshape(Bs * Ts, -1), w_out, xs)
        xp = _mlp_res(xp, norm_g[i, 1], mlp_w1[i], mlp_w2[i])
        xs = _mlp_res(xs, norm_g[i, 1], mlp_w1[i], mlp_w2[i])
    y_prompt = _final_norm(xp, final_norm_g).reshape(B, S, D)
    y_sample = _final_norm(xs, final_norm_g).reshape(Bs, Ts, D)
    return (y_prompt, y_sample,
            jnp.stack(mla_p), jnp.stack(mla_s),
            jnp.stack(mc_p), jnp.stack(mn_p), jnp.stack(mm_p),
            jnp.stack(mc_s), jnp.stack(mn_s), jnp.stack(mm_s),
            jnp.stack(cmp_p), jnp.stack(cmp_s),
            jnp.stack(sel_p), jnp.stack(sel_s),
            jnp.stack(win_p), jnp.stack(win_s))
```

```python
import functools

import jax
import jax.numpy as jnp
from jax import lax
from jax.experimental import pallas as pl
from jax.experimental.pallas import tpu as pltpu

F32 = jnp.float32
BF16 = jnp.bfloat16

D_MODEL = 1024
DEPTH = 4
N_MIXERS = 3
PAGE_SIZE = 128

MLA_HEADS = 16
MLA_NOPE_DIM = 64
MLA_ROPE_DIM = 32
MLA_V_DIM = 64
MLA_Q_LORA = 384
MLA_KV_LORA = 256
MLA_SCALE = (MLA_NOPE_DIM + MLA_ROPE_DIM) ** -0.5
ROPE_THETA = 10000.0

MLSTM_HEADS = 4
MLSTM_DQK = 128
MLSTM_DV = 256
MLSTM_CHUNK = 64

NSA_HEADS = 16
NSA_KV_HEADS = 4
NSA_HEAD_DIM = 64
CMP_BLOCK = 64
SEL_BLOCK = 64
TOP_K_BLOCKS = 16
WINDOW = 512
NSA_Q_COLS = NSA_HEADS * NSA_HEAD_DIM
NSA_KV_COLS = NSA_KV_HEADS * 2 * NSA_HEAD_DIM
NSA_SCALE = NSA_HEAD_DIM ** -0.5

D_FF = 4 * D_MODEL
Q_BLOCK = 128
RMS_EPS = 1e-6
NEG = -1e30
FORCE = 1e4

VMEM_LIMIT_BYTES = 56 * 1024 * 1024
FF_CHUNK = 512


def _row_tile(m):
    for t in (512, 256, 128):
        if m % t == 0:
            return t
    return m


def _rms_rows(x, g):
    return x * lax.rsqrt(jnp.mean(x * x, axis=-1, keepdims=True) + RMS_EPS) * g


def _norm_proj_body(x_ref, g_ref, w_ref, o_ref, *h_ref):
    h = _rms_rows(x_ref[...], g_ref[...])
    o_ref[...] = jnp.dot(h.astype(BF16), w_ref[...], preferred_element_type=F32)
    if h_ref:
        h_ref[0][...] = h


def _norm_proj(x, g, w, with_normed=False):
    m, d = x.shape
    n = w.shape[1]
    tm = _row_tile(m)
    out_specs = [pl.BlockSpec((tm, n), lambda i: (i, 0))]
    out_shape = [jax.ShapeDtypeStruct((m, n), F32)]
    if with_normed:
        out_specs.append(pl.BlockSpec((tm, d), lambda i: (i, 0)))
        out_shape.append(jax.ShapeDtypeStruct((m, d), F32))
    out = pl.pallas_call(
        _norm_proj_body,
        grid=(m // tm,),
        in_specs=[pl.BlockSpec((tm, d), lambda i: (i, 0)),
                  pl.BlockSpec((1, d), lambda i: (0, 0)),
                  pl.BlockSpec((d, n), lambda i: (0, 0))],
        out_specs=out_specs,
        out_shape=out_shape,
        compiler_params=pltpu.CompilerParams(dimension_semantics=("arbitrary",),
                                             vmem_limit_bytes=VMEM_LIMIT_BYTES),
        name="norm_proj",
    )(x, g.reshape(1, d), w.astype(BF16))
    return out if with_normed else out[0]


def _proj_res_body(a_ref, w_ref, r_ref, o_ref):
    o_ref[...] = r_ref[...] + jnp.dot(a_ref[...].astype(BF16), w_ref[...], preferred_element_type=F32)


def _proj_res(a, w, res):
    m, k = a.shape
    d = w.shape[1]
    tm = _row_tile(m)
    return pl.pallas_call(
        _proj_res_body,
        grid=(m // tm,),
        in_specs=[pl.BlockSpec((tm, k), lambda i: (i, 0)),
                  pl.BlockSpec((k, d), lambda i: (0, 0)),
                  pl.BlockSpec((tm, d), lambda i: (i, 0))],
        out_specs=pl.BlockSpec((tm, d), lambda i: (i, 0)),
        out_shape=jax.ShapeDtypeStruct((m, d), F32),
        compiler_params=pltpu.CompilerParams(dimension_semantics=("arbitrary",),
                                             vmem_limit_bytes=VMEM_LIMIT_BYTES),
        name="proj_res",
    )(a, w.astype(BF16), res)


def _mlp_body(x_ref, g_ref, w1_ref, w2_ref, o_ref):
    x = x_ref[...]
    h = _rms_rows(x, g_ref[...]).astype(BF16)
    acc = x
    for c in range(D_FF // FF_CHUNK):
        a = jnp.dot(h, w1_ref[:, c * FF_CHUNK:(c + 1) * FF_CHUNK], preferred_element_type=F32)
        a = jnp.maximum(a, 0.0)
        acc = acc + jnp.dot((a * a).astype(BF16), w2_ref[c * FF_CHUNK:(c + 1) * FF_CHUNK, :],
                            preferred_element_type=F32)
    o_ref[...] = acc


def _mlp_res(x, g, w1, w2):
    m, d = x.shape
    tm = _row_tile(m)
    return pl.pallas_call(
        _mlp_body,
        grid=(m // tm,),
        in_specs=[pl.BlockSpec((tm, d), lambda i: (i, 0)),
                  pl.BlockSpec((1, d), lambda i: (0, 0)),
                  pl.BlockSpec((d, D_FF), lambda i: (0, 0)),
                  pl.BlockSpec((D_FF, d), lambda i: (0, 0))],
        out_specs=pl.BlockSpec((tm, d), lambda i: (i, 0)),
        out_shape=jax.ShapeDtypeStruct((m, d), F32),
        compiler_params=pltpu.CompilerParams(dimension_semantics=("arbitrary",),
                                             vmem_limit_bytes=VMEM_LIMIT_BYTES),
        name="mlp_res",
    )(x, g.reshape(1, d), w1.astype(BF16), w2.astype(BF16))


def _final_norm_body(x_ref, g_ref, o_ref):
    o_ref[...] = _rms_rows(x_ref[...], g_ref[...])


def _final_norm(x, g):
    m, d = x.shape
    tm = _row_tile(m)
    return pl.pallas_call(
        _final_norm_body,
        grid=(m // tm,),
        in_specs=[pl.BlockSpec((tm, d), lambda i: (i, 0)), pl.BlockSpec((1, d), lambda i: (0, 0))],
        out_specs=pl.BlockSpec((tm, d), lambda i: (i, 0)),
        out_shape=jax.ShapeDtypeStruct((m, d), F32),
        compiler_params=pltpu.CompilerParams(dimension_semantics=("arbitrary",)),
        name="final_norm",
    )(x, g.reshape(1, d))


def _dot_nt(a, b):
    return lax.dot_general(a, b, (((1,), (1,)), ((), ())), preferred_element_type=F32)


SUMM_L_PER_STEP = 8
SUMM_COLS = NSA_KV_COLS
SUMM_HALF = SUMM_COLS // 2


def _summ_weights(w_cmp):
    wk = jnp.stack([w_cmp[0], w_cmp[1], w_cmp[0], w_cmp[1]], axis=0)
    bd = jnp.einsum('kj,klde->lkdje', jnp.eye(4, dtype=F32), wk)
    return bd.reshape(CMP_BLOCK, SUMM_HALF, SUMM_HALF).astype(BF16)


def _summarize_body(x_ref, w_ref, o_ref):
    @pl.when(pl.program_id(1) == 0)
    def _():
        o_ref[...] = jnp.zeros_like(o_ref)

    lo = o_ref[:, :SUMM_HALF]
    hi = o_ref[:, SUMM_HALF:]
    for li in range(SUMM_L_PER_STEP):
        x = x_ref[:, li * SUMM_COLS:(li + 1) * SUMM_COLS].astype(BF16)
        w = w_ref[li]
        lo = lo + jnp.dot(x[:, :SUMM_HALF], w, preferred_element_type=F32)
        hi = hi + jnp.dot(x[:, SUMM_HALF:], w, preferred_element_type=F32)
    o_ref[:, :SUMM_HALF] = lo
    o_ref[:, SUMM_HALF:] = hi


def _summarize_blocks(x2d, w_bd):
    nb = x2d.shape[0]
    p = 512 if nb % 512 == 0 else nb
    step_cols = SUMM_L_PER_STEP * SUMM_COLS
    return pl.pallas_call(
        _summarize_body,
        grid=(nb // p, CMP_BLOCK // SUMM_L_PER_STEP),
        in_specs=[pl.BlockSpec((p, step_cols), lambda i, l: (i, l)),
                  pl.BlockSpec((SUMM_L_PER_STEP, SUMM_HALF, SUMM_HALF), lambda i, l: (l, 0, 0))],
        out_specs=pl.BlockSpec((p, SUMM_COLS), lambda i, l: (i, 0)),
        out_shape=jax.ShapeDtypeStruct((nb, SUMM_COLS), F32),
        compiler_params=pltpu.CompilerParams(dimension_semantics=("arbitrary", "arbitrary"),
                                             vmem_limit_bytes=VMEM_LIMIT_BYTES),
        name="nsa_summarize",
    )(x2d, w_bd)


NSA_TQ = 128
NSA_TK = 512
NSA_GROUP = NSA_HEADS // NSA_KV_HEADS
NSA_BAND = WINDOW + NSA_TQ


def _softmax_rows(s, valid):
    m = jnp.max(s, axis=1, keepdims=True)
    e = jnp.where(valid, jnp.exp(s - m), 0.0)
    l = jnp.sum(e, axis=1, keepdims=True)
    return e / jnp.where(l > 0.0, l, 1.0)


def _nsa_prompt_body(q_ref, ksum_ref, vsum_ref, ks_ref, vs_ref, kw_ref, vw_ref, gate_ref, o_ref):
    g = pl.program_id(1)
    q0 = pl.program_id(2) * NSA_TQ
    rows = NSA_GROUP * NSA_TQ
    nblk = ksum_ref.shape[2]
    q = q_ref[0].reshape(rows, NSA_HEAD_DIM) * jnp.asarray(NSA_SCALE, BF16)

    row = lax.broadcasted_iota(jnp.int32, (rows, 1), 0)
    qpos = q0 + (row & (NSA_TQ - 1))
    head = g * NSA_GROUP + (row >> 7)
    slope = jnp.exp((head + 1).astype(F32) * (-8.0 / NSA_HEADS * 0.6931471805599453))

    s = _dot_nt(q, ksum_ref[0, 0])
    blk_end = (lax.broadcasted_iota(jnp.int32, (1, nblk), 1) + 1) * CMP_BLOCK - 1
    dist = qpos - blk_end
    vis = dist >= 0
    p = _softmax_rows(jnp.where(vis, s - slope * dist.astype(F32), NEG), vis)
    o_cmp = jnp.dot(p.astype(BF16), vsum_ref[0, 0], preferred_element_type=F32)
    imp = p[0:NSA_TQ]
    for h in range(1, NSA_GROUP):
        imp = imp + p[h * NSA_TQ:(h + 1) * NSA_TQ]

    tok = q0 + lax.broadcasted_iota(jnp.int32, (NSA_TQ, 1), 0)
    cur = tok >> 6
    jj = lax.broadcasted_iota(jnp.int32, (NSA_TQ, nblk), 1)
    forced = jnp.where(jj == 0, 1, jnp.where(jj == cur, 1, jnp.where(jj == cur - 1, 1, 0)))
    score = jnp.where(forced > 0, FORCE, jnp.where(jj <= cur, imp, -FORCE))
    rank = jnp.zeros((NSA_TQ, nblk), jnp.int32)
    for i in range(nblk):
        ci = score[:, i:i + 1]
        rank = rank + jnp.where(ci > score, 1, jnp.where(ci == score, jnp.where(jj > i, 1, 0), 0))
    sel = jnp.where(rank < min(TOP_K_BLOCKS, nblk), 1.0, 0.0).astype(BF16)
    sel4 = jnp.concatenate([sel] * NSA_GROUP, axis=0)

    def sel_step(kt, carry):
        m, l, acc = carry
        k0 = pl.multiple_of(kt * NSA_TK, NSA_TK)
        k = ks_ref[0, 0, pl.ds(k0, NSA_TK), :]
        v = vs_ref[0, 0, pl.ds(k0, NSA_TK), :]
        kcol = k0 + lax.broadcasted_iota(jnp.int32, (nblk, NSA_TK), 1)
        expand = jnp.where((kcol >> 6) == lax.broadcasted_iota(jnp.int32, (nblk, NSA_TK), 0), 1.0, 0.0)
        picked = jnp.dot(sel4, expand.astype(BF16), preferred_element_type=F32)
        dist = qpos - (k0 + lax.broadcasted_iota(jnp.int32, (1, NSA_TK), 1))
        valid = jnp.where(dist >= 0, picked, 0.0) > 0.5
        s = jnp.where(valid, _dot_nt(q, k) - slope * dist.astype(F32), NEG)
        m_new = jnp.maximum(m, jnp.max(s, axis=1, keepdims=True))
        alpha = jnp.exp(m - m_new)
        e = jnp.where(valid, jnp.exp(s - m_new), 0.0)
        l = alpha * l + jnp.sum(e, axis=1, keepdims=True)
        acc = alpha * acc + jnp.dot(e.astype(BF16), v, preferred_element_type=F32)
        return m_new, l, acc

    n_kt = (q0 + NSA_TQ + NSA_TK - 1) // NSA_TK
    init = (jnp.full((rows, 1), NEG, F32), jnp.zeros((rows, 1), F32), jnp.zeros((rows, NSA_HEAD_DIM), F32))
    _, l, acc = lax.fori_loop(0, n_kt, sel_step, init)
    o_sel = acc / jnp.where(l > 0.0, l, 1.0)

    w0 = pl.multiple_of(jnp.maximum(q0 - WINDOW, 0), NSA_TQ)
    kw = kw_ref[0, 0, pl.ds(w0, NSA_BAND), :]
    vw = vw_ref[0, 0, pl.ds(w0, NSA_BAND), :]
    dist = qpos - (w0 + lax.broadcasted_iota(jnp.int32, (1, NSA_BAND), 1))
    valid = jnp.where(dist >= 0, jnp.where(dist < WINDOW, 1, 0), 0) > 0
    p = _softmax_rows(jnp.where(valid, _dot_nt(q, kw) - slope * dist.astype(F32), NEG), valid)
    o_win = jnp.dot(p.astype(BF16), vw, preferred_element_type=F32)

    gate = jax.nn.sigmoid(gate_ref[0, 0])
    outs = []
    for h in range(NSA_GROUP):
        r = slice(h * NSA_TQ, (h + 1) * NSA_TQ)
        outs.append(gate[:, 3 * h:3 * h + 1] * o_cmp[r] + gate[:, 3 * h + 1:3 * h + 2] * o_sel[r]
                    + gate[:, 3 * h + 2:3 * h + 3] * o_win[r])
    o_ref[0] = jnp.concatenate(outs, axis=-1)


def _nsa_prompt_attend(q, summ, kv_s, kv_w, gate):
    b, h, s, dh = q.shape
    g = NSA_KV_HEADS
    nblk = summ.shape[3]
    assert s % NSA_TK == 0 and s >= NSA_BAND
    seq_spec = pl.BlockSpec((1, 1, s, dh), lambda bi, gi, qi: (bi, gi, 0, 0))
    sum_spec = pl.BlockSpec((1, 1, nblk, dh), lambda bi, gi, qi: (bi, gi, 0, 0))
    return pl.pallas_call(
        _nsa_prompt_body,
        grid=(b, g, s // NSA_TQ),
        in_specs=[pl.BlockSpec((1, NSA_GROUP, NSA_TQ, dh), lambda bi, gi, qi: (bi, gi, qi, 0)),
                  sum_spec, sum_spec, seq_spec, seq_spec, seq_spec, seq_spec,
                  pl.BlockSpec((1, 1, NSA_TQ, 3 * NSA_GROUP), lambda bi, gi, qi: (bi, gi, qi, 0))],
        out_specs=pl.BlockSpec((1, NSA_TQ, NSA_GROUP * dh), lambda bi, gi, qi: (bi, qi, gi)),
        out_shape=jax.ShapeDtypeStruct((b, s, h * dh), F32),
        compiler_params=pltpu.CompilerParams(dimension_semantics=("arbitrary",) * 3,
                                             vmem_limit_bytes=VMEM_LIMIT_BYTES),
        name="nsa_prompt_attend",
    )(q, summ[0], summ[1], kv_s[0], kv_s[1], kv_w[0], kv_w[1], gate)


def _heads_matmul_body(x_ref, w_ref, o_ref):
    o_ref[0] = jnp.dot(x_ref[0].astype(BF16), w_ref[0], preferred_element_type=F32)


def _heads_matmul(x, w):
    h, m, k = x.shape
    n = w.shape[2]
    return pl.pallas_call(
        _heads_matmul_body,
        grid=(h,),
        in_specs=[pl.BlockSpec((1, m, k), lambda i: (i, 0, 0)), pl.BlockSpec((1, k, n), lambda i: (i, 0, 0))],
        out_specs=pl.BlockSpec((1, m, n), lambda i: (i, 0, 0)),
        out_shape=jax.ShapeDtypeStruct((h, m, n), F32),
        compiler_params=pltpu.CompilerParams(dimension_semantics=("arbitrary",)),
        name="heads_matmul",
    )(x, w.astype(BF16))


MLA_TILE = 512
MLA_QK_PAD = 128
MLA_HEAD_PAIR = 2


def _mla_flash_body(q_ref, k_ref, v_ref, o_ref):
    qi = pl.program_id(2)
    t = MLA_TILE
    causal = lax.broadcasted_iota(jnp.int32, (t, t), 0) >= lax.broadcasted_iota(jnp.int32, (t, t), 1)
    outs = []
    for hh in range(MLA_HEAD_PAIR):
        q = q_ref[0, hh]

        def step(kt, carry, diagonal, hh=hh, q=q):
            m, l, acc = carry
            k0 = pl.multiple_of(kt * t, t)
            s = _dot_nt(q, k_ref[0, hh, pl.ds(k0, t), :])
            if diagonal:
                s = jnp.where(causal, s, NEG)
            m_new = jnp.maximum(m, jnp.max(s, axis=1, keepdims=True))
            alpha = jnp.exp(m - m_new)
            e = jnp.exp(s - m_new)
            l = alpha * l + jnp.sum(e, axis=1, keepdims=True)
            acc = alpha * acc + jnp.dot(e.astype(BF16), v_ref[0, hh, pl.ds(k0, t), :],
                                        preferred_element_type=F32)
            return m_new, l, acc

        init = (jnp.full((t, 1), NEG, F32), jnp.zeros((t, 1), F32), jnp.zeros((t, MLA_V_DIM), F32))
        carry = lax.fori_loop(0, qi, functools.partial(step, diagonal=False), init)
        _, l, acc = step(qi, carry, True)
        outs.append(acc / l)
    o_ref[0] = jnp.concatenate(outs, axis=-1)


def _mla_flash(q, k, v):
    b, h, s, _ = q.shape
    t = MLA_TILE
    assert s % t == 0
    kv_map = lambda bi, hi, qi: (bi, hi, 0, 0)
    return pl.pallas_call(
        _mla_flash_body,
        grid=(b, h // MLA_HEAD_PAIR, s // t),
        in_specs=[pl.BlockSpec((1, MLA_HEAD_PAIR, t, MLA_QK_PAD), lambda bi, hi, qi: (bi, hi, qi, 0)),
                  pl.BlockSpec((1, MLA_HEAD_PAIR, s, MLA_QK_PAD), kv_map),
                  pl.BlockSpec((1, MLA_HEAD_PAIR, s, MLA_V_DIM), kv_map)],
        out_specs=pl.BlockSpec((1, t, MLA_HEAD_PAIR * MLA_V_DIM), lambda bi, hi, qi: (bi, qi, hi)),
        out_shape=jax.ShapeDtypeStruct((b, s, h * MLA_V_DIM), F32),
        compiler_params=pltpu.CompilerParams(dimension_semantics=("arbitrary",) * 3,
                                             vmem_limit_bytes=VMEM_LIMIT_BYTES),
        name="mla_flash",
    )(q, k, v)


MLA_PAGES_PER_STEP = 16
MLA_ROW = MLA_KV_LORA + MLA_ROPE_DIM


def _mla_decode_body(pt_ref, q_ref, new_ref, *refs):
    del pt_ref
    npg = MLA_PAGES_PER_STEP
    pages, o_ref = refs[:npg], refs[npg]
    m_sc, l_sc, acc_sc = refs[npg + 1:]
    step = pl.program_id(1)

    @pl.when(step == 0)
    def _():
        m_sc[...] = jnp.full_like(m_sc, NEG)
        l_sc[...] = jnp.zeros_like(l_sc)
        acc_sc[...] = jnp.zeros_like(acc_sc)

    qf = q_ref[0]
    q = qf.astype(BF16)
    rows = [pages[p][0, 0].astype(BF16) for p in range(npg)]
    s = jnp.concatenate([_dot_nt(q, r) for r in rows], axis=1)
    m = m_sc[...]
    m_new = jnp.maximum(m, jnp.max(s, axis=1, keepdims=True))
    alpha = jnp.exp(m - m_new)
    e = jnp.exp(s - m_new)
    l_new = alpha * l_sc[...] + jnp.sum(e, axis=1, keepdims=True)
    eb = e.astype(BF16)
    pv = jnp.zeros(acc_sc.shape, F32)
    for p in range(npg):
        pv = pv + jnp.dot(eb[:, p * PAGE_SIZE:(p + 1) * PAGE_SIZE], rows[p][:, :MLA_KV_LORA],
                          preferred_element_type=F32)
    acc_new = alpha * acc_sc[...] + pv
    m_sc[...] = m_new
    l_sc[...] = l_new
    acc_sc[...] = acc_new

    @pl.when(step == pl.num_programs(1) - 1)
    def _():
        new = new_ref[0]
        s_new = jnp.sum(qf * new, axis=1, keepdims=True)
        m_fin = jnp.maximum(m_new, s_new)
        a = jnp.exp(m_new - m_fin)
        e_new = jnp.exp(s_new - m_fin)
        o_ref[0] = (a * acc_new + e_new * new[:, :MLA_KV_LORA]) / (a * l_new + e_new)


def _mla_decode(q, new_rows, cache, layer, page_table):
    b, h, w = q.shape
    n_pages = page_table.shape[1]
    npg = MLA_PAGES_PER_STEP
    assert n_pages % npg == 0 and cache.shape[2] == PAGE_SIZE

    def page_spec(p):
        return pl.BlockSpec((1, 1, PAGE_SIZE, w), lambda bi, si, pt: (layer, pt[bi, si * npg + p], 0, 0))

    grid_spec = pltpu.PrefetchScalarGridSpec(
        num_scalar_prefetch=1,
        grid=(b, n_pages // npg),
        in_specs=[pl.BlockSpec((1, h, w), lambda bi, si, pt: (bi, 0, 0)),
                  pl.BlockSpec((1, 1, w), lambda bi, si, pt: (bi, 0, 0))] + [page_spec(p) for p in range(npg)],
        out_specs=pl.BlockSpec((1, h, MLA_KV_LORA), lambda bi, si, pt: (bi, 0, 0)),
        scratch_shapes=[pltpu.VMEM((h, 1), F32), pltpu.VMEM((h, 1), F32), pltpu.VMEM((h, MLA_KV_LORA), F32)],
    )
    return pl.pallas_call(
        _mla_decode_body,
        grid_spec=grid_spec,
        out_shape=jax.ShapeDtypeStruct((b, h, MLA_KV_LORA), F32),
        compiler_params=pltpu.CompilerParams(dimension_semantics=("arbitrary", "arbitrary"),
                                             vmem_limit_bytes=VMEM_LIMIT_BYTES),
        name="mla_decode",
    )(page_table, q, new_rows, *([cache] * npg))


NSA_KV_PAIR = 2 * NSA_HEAD_DIM


def _alibi_col(first_head, n):
    head = first_head + lax.broadcasted_iota(jnp.int32, (n, 1), 0)
    return jnp.exp((head + 1).astype(F32) * (-8.0 / NSA_HEADS * 0.6931471805599453))


def _nsa_cmp_sample_body(q_ref, k_ref, v_ref, o_ref, idx_ref):
    nblk = k_ref.shape[2]
    pos = nblk * CMP_BLOCK
    blk_end = (lax.broadcasted_iota(jnp.int32, (1, nblk), 1) + 1) * CMP_BLOCK - 1
    dist = (pos - blk_end).astype(F32)
    imps = []
    for g in range(NSA_KV_HEADS):
        q = q_ref[0, g].astype(BF16)
        s = _dot_nt(q, k_ref[0, g]) - _alibi_col(g * NSA_GROUP, NSA_GROUP) * dist
        p = _softmax_rows(s, jnp.full(s.shape, True))
        o_ref[0, g] = jnp.dot(p.astype(BF16), v_ref[0, g], preferred_element_type=F32)
        imps.append(jnp.sum(p, axis=0, keepdims=True))
    imp = jnp.concatenate(imps, axis=0)
    jj = lax.broadcasted_iota(jnp.int32, (NSA_KV_HEADS, nblk), 1)
    score = jnp.where(jj == 0, FORCE, jnp.where(jj == nblk - 1, FORCE, imp))
    rank = jnp.zeros((NSA_KV_HEADS, nblk), jnp.int32)
    for i in range(nblk):
        ci = score[:, i:i + 1]
        rank = rank + jnp.where(ci > score, 1, jnp.where(ci == score, jnp.where(jj > i, 1, 0), 0))
    n_pick = min(TOP_K_BLOCKS, nblk + 1) - 1
    cols = [jnp.sum(jnp.where(rank == r, jj, 0), axis=1, keepdims=True) for r in range(n_pick)]
    cols.append(jnp.full((NSA_KV_HEADS, 1), nblk, jnp.int32))
    idx_ref[0] = jnp.concatenate(cols, axis=1)


def _nsa_cmp_sample(q, summ):
    b = q.shape[0]
    nblk = summ.shape[3]
    n_sel = min(TOP_K_BLOCKS, nblk + 1)
    sum_spec = pl.BlockSpec((1, NSA_KV_HEADS, nblk, NSA_HEAD_DIM), lambda i: (i, 0, 0, 0))
    return pl.pallas_call(
        _nsa_cmp_sample_body,
        grid=(b,),
        in_specs=[pl.BlockSpec((1, NSA_KV_HEADS, NSA_GROUP, NSA_HEAD_DIM), lambda i: (i, 0, 0, 0)),
                  sum_spec, sum_spec],
        out_specs=[pl.BlockSpec((1, NSA_KV_HEADS, NSA_GROUP, NSA_HEAD_DIM), lambda i: (i, 0, 0, 0)),
                   pl.BlockSpec((1, NSA_KV_HEADS, n_sel), lambda i: (i, 0, 0))],
        out_shape=[jax.ShapeDtypeStruct((b, NSA_KV_HEADS, NSA_GROUP, NSA_HEAD_DIM), F32),
                   jax.ShapeDtypeStruct((b, NSA_KV_HEADS, n_sel), jnp.int32)],
        compiler_params=pltpu.CompilerParams(dimension_semantics=("arbitrary",)),
        name="nsa_cmp_sample",
    )(q, summ[0], summ[1])


def _nsa_sel_sample_body(n_past, idx_ref, pt_ref, q_ref, new_ref, *refs):
    del pt_ref
    n_sel = len(refs) - 1
    blocks, o_ref = refs[:n_sel], refs[n_sel]
    bi, g = pl.program_id(0), pl.program_id(1)
    pos = n_past * SEL_BLOCK
    qf = q_ref[0, 0]
    q = qf.astype(BF16)
    slope = _alibi_col(g * NSA_GROUP, NSA_GROUP)
    lane = lax.broadcasted_iota(jnp.int32, (1, SEL_BLOCK), 1)
    rows, scores = [], []
    for k in range(n_sel):
        j = idx_ref[bi, g * n_sel + k]
        r = blocks[k][0].astype(BF16)
        dist = (pos - (j * SEL_BLOCK + lane)).astype(F32)
        s = jnp.where(j < n_past, _dot_nt(q, r) - slope * dist, NEG)
        rows.append(r)
        scores.append(s)
    new = new_ref[0, 0]
    s_new = jnp.sum(qf * new, axis=1, keepdims=True)
    m = s_new
    for s in scores:
        m = jnp.maximum(m, jnp.max(s, axis=1, keepdims=True))
    e_new = jnp.exp(s_new - m)
    l = e_new
    acc = e_new * new
    for s, r in zip(scores, rows):
        e = jnp.exp(s - m)
        l = l + jnp.sum(e, axis=1, keepdims=True)
        acc = acc + jnp.dot(e.astype(BF16), r, preferred_element_type=F32)
    o_ref[0, 0] = acc / l


def _nsa_sel_sample(q_pad, new_kv, cache_sel, layer, page_table, idx):
    b, g, n_sel = idx.shape
    n_pages = page_table.shape[1]
    per_page = PAGE_SIZE // SEL_BLOCK
    n_past = n_pages * per_page
    pool_blocks = cache_sel.shape[1] * per_page
    cache = cache_sel.reshape(cache_sel.shape[0], pool_blocks, SEL_BLOCK, g * NSA_KV_PAIR)

    def block_spec(k):
        def index(bi, gi, idx_ref, pt_ref):
            j = jnp.minimum(idx_ref[bi, gi * n_sel + k], n_past - 1)
            return (layer, pt_ref[bi, j // per_page] * per_page + j % per_page, 0, gi)
        return pl.BlockSpec((None, 1, SEL_BLOCK, NSA_KV_PAIR), index)

    grp_spec = lambda rows: pl.BlockSpec((1, 1, rows, NSA_KV_PAIR), lambda bi, gi, i_, p_: (bi, gi, 0, 0))
    grid_spec = pltpu.PrefetchScalarGridSpec(
        num_scalar_prefetch=2,
        grid=(b, g),
        in_specs=[grp_spec(NSA_GROUP), grp_spec(1)] + [block_spec(k) for k in range(n_sel)],
        out_specs=grp_spec(NSA_GROUP),
    )
    return pl.pallas_call(
        functools.partial(_nsa_sel_sample_body, n_past),
        grid_spec=grid_spec,
        out_shape=jax.ShapeDtypeStruct((b, g, NSA_GROUP, NSA_KV_PAIR), F32),
        compiler_params=pltpu.CompilerParams(dimension_semantics=("arbitrary", "arbitrary")),
        name="nsa_sel_sample",
    )(idx.reshape(b, g * n_sel), page_table, q_pad, new_kv, *([cache] * n_sel))


def _nsa_win_sample_body(q_ref, new_ref, win_ref, o_ref, nwin_ref):
    wbuf = win_ref.shape[1]
    win = win_ref[0]
    new = new_ref[0]
    dist = wbuf - lax.broadcasted_iota(jnp.int32, (1, wbuf), 1)
    valid = dist < WINDOW
    for g in range(NSA_KV_HEADS):
        lanes = slice(g * NSA_KV_PAIR, (g + 1) * NSA_KV_PAIR)
        qf = q_ref[0, g]
        r = win[:, lanes].astype(BF16)
        slope = _alibi_col(g * NSA_GROUP, NSA_GROUP)
        s = jnp.where(valid, _dot_nt(qf.astype(BF16), r) - slope * dist.astype(F32), NEG)
        s_new = jnp.sum(qf * new[:, lanes], axis=1, keepdims=True)
        m = jnp.maximum(s_new, jnp.max(s, axis=1, keepdims=True))
        e = jnp.exp(s - m)
        e_new = jnp.exp(s_new - m)
        acc = e_new * new[:, lanes] + jnp.dot(e.astype(BF16), r, preferred_element_type=F32)
        o_ref[0, g] = acc / (e_new + jnp.sum(e, axis=1, keepdims=True))
    is_last = lax.broadcasted_iota(jnp.int32, (wbuf, 1), 0) == wbuf - 1
    nwin_ref[0] = jnp.where(is_last, new, pltpu.roll(win, wbuf - 1, 0))


def _nsa_win_sample(q_pad, new_row, win_state):
    b, wbuf, cols = win_state.shape
    assert wbuf == WINDOW
    q_spec = pl.BlockSpec((1, NSA_KV_HEADS, NSA_GROUP, NSA_KV_PAIR), lambda i: (i, 0, 0, 0))
    win_spec = pl.BlockSpec((1, wbuf, cols), lambda i: (i, 0, 0))
    return pl.pallas_call(
        _nsa_win_sample_body,
        grid=(b,),
        in_specs=[q_spec, pl.BlockSpec((1, 1, cols), lambda i: (i, 0, 0)), win_spec],
        out_specs=[q_spec, win_spec],
        out_shape=[jax.ShapeDtypeStruct((b, NSA_KV_HEADS, NSA_GROUP, NSA_KV_PAIR), F32),
                   jax.ShapeDtypeStruct((b, wbuf, cols), F32)],
        compiler_params=pltpu.CompilerParams(dimension_semantics=("arbitrary",)),
        name="nsa_win_sample",
    )(q_pad, new_row, win_state)


MLSTM_KERNEL_CHUNK = 256
MLSTM_QK_COLS = MLSTM_HEADS * MLSTM_DQK
MLSTM_V_COLS = MLSTM_HEADS * MLSTM_DV


def _log_sigmoid(x):
    return jnp.minimum(x, 0.0) - jnp.log(1.0 + jnp.exp(-jnp.abs(x)))


def _mlstm_chunk_body(q_ref, k_ref, v_ref, og_ref, vt_ref, ig_ref, fg_ref,
                      h_ref, c_ref, n_ref, m_ref, c_sc, n_sc, m_sc):
    chunk = pl.program_id(2)

    @pl.when(chunk == 0)
    def _():
        c_sc[...] = jnp.zeros_like(c_sc)
        n_sc[...] = jnp.zeros_like(n_sc)
        m_sc[...] = jnp.zeros_like(m_sc)

    L = q_ref.shape[1]
    q = q_ref[0]
    qb = q.astype(BF16)
    kb = (k_ref[0] * (MLSTM_DQK ** -0.5)).astype(BF16)
    i_row = ig_ref[0, 0, 0]
    f_row = _log_sigmoid(fg_ref[0, 0, 0])
    tt = lax.broadcasted_iota(jnp.int32, (L, L), 0)
    ss = lax.broadcasted_iota(jnp.int32, (L, L), 1)
    tri = ss <= tt
    b_col = jnp.sum(jnp.where(tri, f_row, 0.0), axis=1, keepdims=True)
    b_row = jnp.sum(jnp.where(tt == ss, b_col, 0.0), axis=0, keepdims=True)
    m = m_sc[...]
    c = c_sc[...]
    n = n_sc[...]
    d = jnp.where(tri, b_col - b_row + i_row, NEG)
    inter = b_col + m
    mt = jnp.maximum(inter, jnp.max(d, axis=1, keepdims=True))
    w = jnp.exp(d - mt)
    gq = jnp.exp(inter - mt)
    a = w * _dot_nt(qb, kb)
    num = (jnp.dot(a.astype(BF16), v_ref[0].astype(BF16), preferred_element_type=F32)
           + gq * _dot_nt(qb, c.astype(BF16)))
    den = jnp.sum(a, axis=1, keepdims=True) + gq * jnp.sum(q * n, axis=1, keepdims=True)
    hc = num / jnp.maximum(jnp.abs(den), jnp.exp(-mt))
    h_ref[0] = hc * jax.nn.sigmoid(og_ref[0])

    b_last = b_col[L - 1:L, :]
    m_new = mt[L - 1:L, :]
    wl = jnp.exp(b_last - b_row + i_row - m_new)
    gl = jnp.exp(b_last + m - m_new)
    c_new = gl * c + jnp.dot((vt_ref[0, 0] * wl).astype(BF16), kb, preferred_element_type=F32)
    wl8 = jnp.broadcast_to(wl, (8, L)).astype(BF16)
    n_new = gl * n + jnp.dot(wl8, kb, preferred_element_type=F32)[0:1]
    c_sc[...] = c_new
    n_sc[...] = n_new
    m_sc[...] = m_new

    @pl.when(chunk == pl.num_programs(2) - 1)
    def _():
        c_ref[0, 0] = c_new
        n_ref[0, 0] = n_new
        m_ref[0, 0] = m_new


def _mlstm_prompt(z, b_gate):
    B, S, _ = z.shape
    H, L = MLSTM_HEADS, MLSTM_KERNEL_CHUNK
    assert S % L == 0
    nc = S // L
    kblk, vblk = MLSTM_QK_COLS // MLSTM_DQK, (2 * MLSTM_QK_COLS) // MLSTM_DV
    gates = z[..., 2 * MLSTM_QK_COLS + 2 * MLSTM_V_COLS:] + b_gate
    gates = gates.reshape(B, nc, L, 2, H).transpose(3, 0, 4, 1, 2)[:, :, :, :, None, :]
    vt = z[..., 2 * MLSTM_QK_COLS:2 * MLSTM_QK_COLS + MLSTM_V_COLS].reshape(B, S, H, MLSTM_DV)
    vt = vt.transpose(0, 2, 3, 1)
    gate_spec = pl.BlockSpec((1, 1, 1, 1, L), lambda b, h, c: (b, h, c, 0, 0))
    state = lambda r, w: pl.BlockSpec((1, 1, r, w), lambda b, h, c: (b, h, 0, 0))
    hs, c, n, m = pl.pallas_call(
        _mlstm_chunk_body,
        grid=(B, H, nc),
        in_specs=[pl.BlockSpec((1, L, MLSTM_DQK), lambda b, h, c: (b, c, h)),
                  pl.BlockSpec((1, L, MLSTM_DQK), lambda b, h, c: (b, c, kblk + h)),
                  pl.BlockSpec((1, L, MLSTM_DV), lambda b, h, c: (b, c, vblk + h)),
                  pl.BlockSpec((1, L, MLSTM_DV), lambda b, h, c: (b, c, vblk + H + h)),
                  pl.BlockSpec((1, 1, MLSTM_DV, L), lambda b, h, c: (b, h, 0, c)),
                  gate_spec, gate_spec],
        out_specs=[pl.BlockSpec((1, L, MLSTM_DV), lambda b, h, c: (b, c, h)),
                   state(MLSTM_DV, MLSTM_DQK), state(1, MLSTM_DQK), state(1, 1)],
        out_shape=[jax.ShapeDtypeStruct((B, S, MLSTM_V_COLS), F32),
                   jax.ShapeDtypeStruct((B, H, MLSTM_DV, MLSTM_DQK), F32),
                   jax.ShapeDtypeStruct((B, H, 1, MLSTM_DQK), F32),
                   jax.ShapeDtypeStruct((B, H, 1, 1), F32)],
        scratch_shapes=[pltpu.VMEM((MLSTM_DV, MLSTM_DQK), F32), pltpu.VMEM((1, MLSTM_DQK), F32),
                        pltpu.VMEM((1, 1), F32)],
        compiler_params=pltpu.CompilerParams(dimension_semantics=("arbitrary",) * 3),
        name="mlstm_chunks",
    )(z, z, z, z, vt, gates[0], gates[1])
    return hs, c, n.reshape(B, H, MLSTM_DQK), m.reshape(B, H)


def _mlstm_step_body(q_ref, k_ref, v_ref, og_ref, ig_ref, fg_ref, c_ref, n_ref, m_ref,
                     h_ref, c_out, n_out, m_out):
    eye = (lax.broadcasted_iota(jnp.int32, (MLSTM_DV, MLSTM_DV), 0)
           == lax.broadcasted_iota(jnp.int32, (MLSTM_DV, MLSTM_DV), 1))
    for h in range(MLSTM_HEADS):
        c, n, m = c_ref[0, h], n_ref[0, h], m_ref[0, h]
        q = q_ref[0, h]
        k = k_ref[0, h] * (MLSTM_DQK ** -0.5)
        v = v_ref[0, h]
        i_g = ig_ref[0, h]
        inter = _log_sigmoid(fg_ref[0, h]) + m
        mt = jnp.maximum(inter, i_g)
        w = jnp.exp(i_g - mt)
        gq = jnp.exp(inter - mt)
        a = w * jnp.sum(q * k, axis=1, keepdims=True)
        cq_col = jnp.sum(c * q, axis=1, keepdims=True)
        cq_row = jnp.sum(jnp.where(eye, cq_col, 0.0), axis=0, keepdims=True)
        den = a + gq * jnp.sum(n * q, axis=1, keepdims=True)
        hc = (a * v + gq * cq_row) / jnp.maximum(jnp.abs(den), jnp.exp(-mt))
        h_ref[0, h] = hc * jax.nn.sigmoid(og_ref[0, h])
        v_col = jnp.sum(jnp.where(eye, v, 0.0), axis=1, keepdims=True)
        c_out[0, h] = gq * c + (w * v_col) * k
        n_out[0, h] = gq * n + w * k
        m_out[0, h] = mt


def _mlstm_sample(z, b_gate, c0, n0, m0):
    B, T, _ = z.shape
    assert T == 1
    H = MLSTM_HEADS
    z = z.reshape(B, -1)
    qk, hv = MLSTM_QK_COLS, MLSTM_V_COLS
    q = z[:, :qk].reshape(B, H, 1, MLSTM_DQK)
    k = z[:, qk:2 * qk].reshape(B, H, 1, MLSTM_DQK)
    v = z[:, 2 * qk:2 * qk + hv].reshape(B, H, 1, MLSTM_DV)
    og = z[:, 2 * qk + hv:2 * qk + 2 * hv].reshape(B, H, 1, MLSTM_DV)
    gates = z[:, 2 * qk + 2 * hv:] + b_gate
    ig = gates[:, :H].reshape(B, H, 1, 1)
    fg = gates[:, H:].reshape(B, H, 1, 1)
    spec = lambda r, w: pl.BlockSpec((1, H, r, w), lambda b: (b, 0, 0, 0))
    shapes = [(1, MLSTM_DV), (MLSTM_DV, MLSTM_DQK), (1, MLSTM_DQK), (1, 1)]
    hs, c, n, m = pl.pallas_call(
        _mlstm_step_body,
        grid=(B,),
        in_specs=[spec(1, MLSTM_DQK), spec(1, MLSTM_DQK), spec(1, MLSTM_DV), spec(1, MLSTM_DV),
                  spec(1, 1), spec(1, 1), spec(MLSTM_DV, MLSTM_DQK), spec(1, MLSTM_DQK), spec(1, 1)],
        out_specs=[spec(*s) for s in shapes],
        out_shape=[jax.ShapeDtypeStruct((B, H) + s, F32) for s in shapes],
        compiler_params=pltpu.CompilerParams(dimension_semantics=("arbitrary",)),
        name="mlstm_step",
    )(q, k, v, og, ig, fg, c0, n0.reshape(B, H, 1, MLSTM_DQK), m0.reshape(B, H, 1, 1))
    return hs.reshape(B, T, hv), c, n.reshape(B, H, MLSTM_DQK), m.reshape(B, H)


def _rmsnorm(x, g):
    return x * lax.rsqrt(jnp.mean(x * x, axis=-1, keepdims=True) + RMS_EPS) * g


def _rope(x, pos):
    half = x.shape[-1] // 2
    freqs = ROPE_THETA ** (-jnp.arange(half, dtype=F32) / half)
    ang = pos.astype(F32)[:, None] * freqs[None, :]
    cos = jnp.cos(ang)[None, :, None, :]
    sin = jnp.sin(ang)[None, :, None, :]
    x1, x2 = x[..., :half], x[..., half:]
    return jnp.concatenate([x1 * cos - x2 * sin, x1 * sin + x2 * cos], axis=-1)


def _alibi_slopes(n):
    return 2.0 ** (-8.0 * jnp.arange(1, n + 1, dtype=F32) / n)


def _to_blocks(a, axis):
    n = a.shape[axis] // Q_BLOCK
    a = a.reshape(a.shape[:axis] + (n, Q_BLOCK) + a.shape[axis + 1:])
    return jnp.moveaxis(a, axis, 0)


def _from_blocks(a, axis):
    a = jnp.moveaxis(a, 0, axis)
    return a.reshape(a.shape[:axis] + (-1,) + a.shape[axis + 2:])


def _attend(q, ks, vs, kposs, q_pos, scale, slopes=None, window=None):
    B, Tq, H, Dk = q.shape
    kvh = ks[0].shape[2]
    G = H // kvh
    qg = q.reshape(B, Tq, kvh, G, Dk)
    scores = []
    for k, kp in zip(ks, kposs):
        s = jnp.einsum('bqgnd,bkgd->bgnqk', qg, k).astype(F32) * scale
        dist = q_pos[:, None] - kp[None, :]
        mask = (dist >= 0) & (kp[None, :] >= 0)
        if window is not None:
            mask = mask & (dist < window)
        if slopes is not None:
            s = s - slopes.reshape(kvh, G)[None, :, :, None, None] * dist.astype(F32)
        scores.append(jnp.where(mask, s, NEG))
    p = jax.nn.softmax(jnp.concatenate(scores, axis=-1), axis=-1)
    out, off = 0.0, 0
    for v in vs:
        n = v.shape[1]
        out = out + jnp.einsum('bgnqk,bkgd->bqgnd', p[..., off:off + n], v)
        off += n
    return out.reshape(B, Tq, H, -1)


def _mla_prompt(a, g_q, g_kv, w_uq, w_uk, w_uv):
    B, S, _ = a.shape
    H = MLA_HEADS
    pos = jnp.arange(S, dtype=jnp.int32)
    a2 = a.reshape(B * S, -1)
    q = _norm_proj(a2[:, :MLA_Q_LORA], g_q, w_uq).reshape(B, S, H, MLA_NOPE_DIM + MLA_ROPE_DIM)
    w_kv = jnp.concatenate([w_uk.reshape(MLA_KV_LORA, -1), w_uv.reshape(MLA_KV_LORA, -1)], axis=1)
    kv, ckv = _norm_proj(a2[:, MLA_Q_LORA:MLA_Q_LORA + MLA_KV_LORA], g_kv, w_kv, with_normed=True)
    kpe = _rope(a[..., MLA_Q_LORA + MLA_KV_LORA:][:, :, None, :], pos)
    q_pe = _rope(q[..., MLA_NOPE_DIM:], pos)
    zpad = jnp.zeros((B, S, H, MLA_QK_PAD - MLA_NOPE_DIM - MLA_ROPE_DIM), F32)
    qc = jnp.concatenate([q[..., :MLA_NOPE_DIM] * MLA_SCALE, q_pe * MLA_SCALE, zpad], axis=-1)
    k_nope = kv[:, :H * MLA_NOPE_DIM].reshape(B, S, H, MLA_NOPE_DIM)
    kc = jnp.concatenate([k_nope, jnp.broadcast_to(kpe, (B, S, H, MLA_ROPE_DIM)), zpad], axis=-1)
    v = kv[:, H * MLA_NOPE_DIM:].reshape(B, S, H, MLA_V_DIM)
    to_heads = lambda t: t.transpose(0, 2, 1, 3).astype(BF16)
    o = _mla_flash(to_heads(qc), to_heads(kc), to_heads(v))
    new_rows = jnp.concatenate([ckv.reshape(B, S, -1), kpe[:, :, 0]], axis=-1)
    return o, new_rows


def _mla_sample(a, cache, j, page_table, g_q, g_kv, w_uq, w_uk, w_uv):
    B, T, _ = a.shape
    assert T == 1
    H = MLA_HEADS
    past_len = page_table.shape[1] * cache.shape[2]
    pos = past_len + jnp.arange(T, dtype=jnp.int32)
    a2 = a.reshape(B * T, -1)
    q = _norm_proj(a2[:, :MLA_Q_LORA], g_q, w_uq).reshape(B, T, H, MLA_NOPE_DIM + MLA_ROPE_DIM)
    w_kv = jnp.concatenate([w_uk.reshape(MLA_KV_LORA, -1), w_uv.reshape(MLA_KV_LORA, -1)], axis=1)
    _, ckv = _norm_proj(a2[:, MLA_Q_LORA:MLA_Q_LORA + MLA_KV_LORA], g_kv, w_kv, with_normed=True)
    kpe = _rope(a[..., MLA_Q_LORA + MLA_KV_LORA:][:, :, None, :], pos)
    q_pe = _rope(q[..., MLA_NOPE_DIM:], pos)
    new_rows = jnp.concatenate([ckv.reshape(B, T, -1), kpe[:, :, 0]], axis=-1)
    q_nope = q[:, 0, :, :MLA_NOPE_DIM].transpose(1, 0, 2)
    q_lat = _heads_matmul(q_nope, w_uk.transpose(1, 2, 0)).transpose(1, 0, 2)
    q_abs = jnp.concatenate([q_lat, q_pe[:, 0]], axis=-1) * MLA_SCALE
    o_lat = _mla_decode(q_abs, new_rows, cache, j, page_table)
    o = _heads_matmul(o_lat.transpose(1, 0, 2), w_uv.transpose(1, 0, 2))
    return o.transpose(1, 0, 2).reshape(B, T, -1), new_rows


def _mlstm_split(z, b_gate):
    B, T, _ = z.shape
    qk = MLSTM_HEADS * MLSTM_DQK
    hv = MLSTM_HEADS * MLSTM_DV
    q = z[..., :qk].reshape(B, T, MLSTM_HEADS, MLSTM_DQK)
    k = z[..., qk:2 * qk].reshape(B, T, MLSTM_HEADS, MLSTM_DQK) * (MLSTM_DQK ** -0.5)
    v = z[..., 2 * qk:2 * qk + hv].reshape(B, T, MLSTM_HEADS, MLSTM_DV)
    o = jax.nn.sigmoid(z[..., 2 * qk + hv:2 * qk + 2 * hv]).reshape(B, T, MLSTM_HEADS, MLSTM_DV)
    gt = z[..., 2 * qk + 2 * hv:] + b_gate
    ig = gt[..., :MLSTM_HEADS]
    lf = jax.nn.log_sigmoid(gt[..., MLSTM_HEADS:])
    return q, k, v, o, ig, lf


def _mlstm_chunkwise(q, k, v, ig, lf, c0, n0, m0):
    B, T, H, _ = q.shape
    L = MLSTM_CHUNK if T % MLSTM_CHUNK == 0 else T
    nc = T // L

    def split(a):
        a = a.reshape((B, nc, L) + a.shape[2:])
        return jnp.moveaxis(a, 1, 0)

    causal = jnp.tril(jnp.ones((L, L), dtype=bool))

    def step(carry, xs):
        c, n, m = carry
        qc, kc, vc, ic, fc = xs
        b = jnp.cumsum(fc, axis=1).transpose(0, 2, 1)
        it = ic.transpose(0, 2, 1)
        d = jnp.where(causal, b[:, :, :, None] - b[:, :, None, :] + it[:, :, None, :], NEG)
        inter = b + m[:, :, None]
        mt = jnp.maximum(inter, d.max(axis=-1))
        w = jnp.exp(d - mt[..., None])
        g = jnp.exp(inter - mt)
        a = w * jnp.einsum('bthd,bshd->bhts', qc, kc)
        num = jnp.einsum('bhts,bshe->bthe', a, vc) + jnp.einsum('bht,bhed,bthd->bthe', g, c, qc)
        den = a.sum(axis=-1) + g * jnp.einsum('bhd,bthd->bht', n, qc)
        hc = num / jnp.maximum(jnp.abs(den), jnp.exp(-mt)).transpose(0, 2, 1)[..., None]
        m_new = mt[:, :, -1]
        wl = jnp.exp(b[:, :, -1:] - b + it - m_new[:, :, None])
        gl = jnp.exp(b[:, :, -1] + m - m_new)
        c_new = gl[:, :, None, None] * c + jnp.einsum('bhs,bshe,bshd->bhed', wl, vc, kc)
        n_new = gl[:, :, None] * n + jnp.einsum('bhs,bshd->bhd', wl, kc)
        return (c_new, n_new, m_new), hc

    (c, n, m), hs = lax.scan(step, (c0, n0, m0), (split(q), split(k), split(v), split(ig), split(lf)))
    hs = jnp.moveaxis(hs, 0, 1).reshape(B, T, H, -1)
    return hs, c, n, m


def _mlstm_mix(z, c0, n0, m0, b_gate):
    B, T, _ = z.shape
    q, k, v, o, ig, lf = _mlstm_split(z, b_gate)
    hs, c, n, m = _mlstm_chunkwise(q, k, v, ig, lf, c0, n0, m0)
    return (hs * o).reshape(B, T, -1), c, n, m


def _nsa_split(z):
    B, T, _ = z.shape
    q = z[..., :NSA_Q_COLS].reshape(B, T, NSA_HEADS, NSA_HEAD_DIM)
    kv = z[..., NSA_Q_COLS:NSA_Q_COLS + 3 * NSA_KV_COLS].reshape(B, T, 3, NSA_KV_HEADS, 2, NSA_HEAD_DIM)
    g = jax.nn.sigmoid(z[..., NSA_Q_COLS + 3 * NSA_KV_COLS:]).reshape(B, T, NSA_HEADS, 3)
    return q, kv[:, :, 0], kv[:, :, 1], kv[:, :, 2], g


def _summarize(kv, w_cmp):
    B, T = kv.shape[:2]
    nb = T // CMP_BLOCK
    blk = kv[:, :nb * CMP_BLOCK].reshape(B, nb, CMP_BLOCK, NSA_KV_HEADS, 2, NSA_HEAD_DIM)
    return jnp.einsum('bnlgcd,clde->bngce', blk, w_cmp)


def _nsa_compressed(q, summ, q_pos, slopes):
    B, T, H, Dh = q.shape
    nbc = summ.shape[1]
    G = H // NSA_KV_HEADS
    qg = q.reshape(B, T, NSA_KV_HEADS, G, Dh)
    s = jnp.einsum('btgnd,bjgd->bgntj', qg, summ[..., 0, :]) * NSA_SCALE
    end = (jnp.arange(nbc, dtype=jnp.int32) + 1) * CMP_BLOCK - 1
    dist = q_pos[:, None] - end[None, :]
    mask = dist >= 0
    s = jnp.where(mask, s - slopes.reshape(NSA_KV_HEADS, G)[None, :, :, None, None] * dist.astype(F32), NEG)
    p = jnp.where(mask, jax.nn.softmax(s, axis=-1), 0.0)
    o = jnp.einsum('bgntj,bjgd->btgnd', p, summ[..., 1, :]).reshape(B, T, H, Dh)
    return o, p.sum(axis=2)


def _nsa_select(imp, q_pos, nb):
    nbc = imp.shape[-1]
    score = jnp.pad(imp, ((0, 0), (0, 0), (0, 0), (0, nb - nbc)))
    j = jnp.arange(nb, dtype=jnp.int32)[None, :]
    cur = (q_pos // SEL_BLOCK)[:, None]
    forced = (j == 0) | (j == cur) | (j == cur - 1)
    score = jnp.where(forced, FORCE, jnp.where(j <= cur, score, -FORCE))
    return lax.top_k(score, min(TOP_K_BLOCKS, nb))[1]


def _sel_attn(q, q_pos, kg, vg, idx, slopes):
    B, Tq, H, Dh = q.shape
    G = H // NSA_KV_HEADS
    qg = q.reshape(B, Tq, NSA_KV_HEADS, G, Dh)
    s = jnp.einsum('btgnd,bgtkld->bgntkl', qg, kg) * NSA_SCALE
    rows = idx[..., None] * SEL_BLOCK + jnp.arange(SEL_BLOCK, dtype=jnp.int32)
    dist = (q_pos[None, None, :, None, None] - rows)[:, :, None]
    sl = slopes.reshape(NSA_KV_HEADS, G)[None, :, :, None, None, None]
    s = jnp.where(dist >= 0, s - sl * dist.astype(F32), NEG)
    shp = s.shape
    p = jax.nn.softmax(s.reshape(shp[:4] + (-1,)), axis=-1).reshape(shp)
    o = jnp.einsum('bgntkl,bgtkld->btgnd', p, vg)
    return o.reshape(B, Tq, H, Dh)


def _nsa_merge(g, o_cmp, o_sel, o_win):
    o = g[..., 0:1] * o_cmp + g[..., 1:2] * o_sel + g[..., 2:3] * o_win
    B, T = o.shape[:2]
    return o.reshape(B, T, -1)


def _nsa_seq_layout(kv):
    return kv.transpose(3, 0, 2, 1, 4).astype(BF16)


def _nsa_prompt(z, w_bd):
    B, S, _ = z.shape
    nb = S // CMP_BLOCK
    q = z[..., :NSA_Q_COLS].reshape(B, S, NSA_HEADS, NSA_HEAD_DIM).transpose(0, 2, 1, 3).astype(BF16)
    kv = z[..., NSA_Q_COLS:NSA_Q_COLS + 3 * NSA_KV_COLS].reshape(B, S, 3, NSA_KV_HEADS, 2, NSA_HEAD_DIM)
    kv_c, kv_s, kv_w = kv[:, :, 0], kv[:, :, 1], kv[:, :, 2]
    summ = _summarize_blocks(kv_c.reshape(B * nb, CMP_BLOCK * NSA_KV_COLS), w_bd)
    summ = summ.reshape(B, nb, NSA_KV_HEADS, 2, NSA_HEAD_DIM).transpose(3, 0, 2, 1, 4).astype(BF16)
    gate = z[..., NSA_Q_COLS + 3 * NSA_KV_COLS:].reshape(B, S, NSA_KV_HEADS, 3 * NSA_GROUP).transpose(0, 2, 1, 3)
    y = _nsa_prompt_attend(q, summ, _nsa_seq_layout(kv_s), _nsa_seq_layout(kv_w), gate)
    return y, kv_c, kv_s, kv_w[:, -min(WINDOW, S):]


def _nsa_sample(z, cache_cmp, cache_sel, win_state, j, page_table, w_bd):
    B, T, _ = z.shape
    assert T == 1 and T < CMP_BLOCK
    G, Dh = NSA_KV_HEADS, NSA_HEAD_DIM
    n_pool, page = cache_cmp.shape[1], cache_cmp.shape[2]
    n_pages = page_table.shape[1]
    per_page = page // CMP_BLOCK
    q, kv_c, kv_s, kv_w, gate = _nsa_split(z)
    pool = cache_cmp[j].reshape(n_pool * per_page, CMP_BLOCK * NSA_KV_COLS)
    summ_pool = _summarize_blocks(pool, w_bd).reshape(n_pool, per_page, G, 2, Dh)
    summ = summ_pool[page_table].reshape(B, n_pages * per_page, G, 2, Dh)
    summ = summ.transpose(3, 0, 2, 1, 4).astype(BF16)
    qg = q.reshape(B, G, NSA_GROUP, Dh) * NSA_SCALE
    o_cmp, idx = _nsa_cmp_sample(qg, summ)
    q_pad = jnp.concatenate([qg, jnp.zeros_like(qg)], axis=-1)
    o_sel = _nsa_sel_sample(q_pad, kv_s.reshape(B, G, 1, NSA_KV_PAIR), cache_sel, j, page_table, idx)
    o_win, new_win = _nsa_win_sample(q_pad, kv_w.reshape(B, 1, NSA_KV_COLS),
                                     win_state.reshape(B, win_state.shape[1], NSA_KV_COLS))
    heads = lambda o: o.reshape(B, T, NSA_HEADS, -1)[..., -Dh:]
    y = _nsa_merge(gate, heads(o_cmp), heads(o_sel), heads(o_win))
    return y, kv_c, kv_s, new_win.reshape(win_state.shape)


def kernel(x_prompt, x_sample, cache_mla_kv, state_mlstm_c, state_mlstm_n, state_mlstm_m, cache_nsa_cmp,
           cache_nsa_sel, state_nsa_win, page_table, norm_g, final_norm_g, mla_w_a, mla_g_q, mla_g_kv,
           mla_w_uq, mla_w_uk, mla_w_uv, mla_w_o, mlstm_w_in, mlstm_b_gate, mlstm_w_out, nsa_w_in,
           nsa_w_cmp, nsa_w_out, mlp_w1, mlp_w2):
    B, S, D = x_prompt.shape
    Bs, Ts, _ = x_sample.shape
    xp = x_prompt.reshape(B * S, D)
    xs = x_sample.reshape(Bs * Ts, D)
    mla_p, mla_s = [], []
    mc_p, mn_p, mm_p, mc_s, mn_s, mm_s = [], [], [], [], [], []
    cmp_p, cmp_s, sel_p, sel_s, win_p, win_s = [], [], [], [], [], []
    for i in range(DEPTH):
        j = i // N_MIXERS
        g0 = norm_g[i, 0]
        if i % N_MIXERS == 0:
            ap = _norm_proj(xp, g0, mla_w_a[j]).reshape(B, S, -1)
            as_ = _norm_proj(xs, g0, mla_w_a[j]).reshape(Bs, Ts, -1)
            w = (mla_g_q[j], mla_g_kv[j], mla_w_uq[j], mla_w_uk[j], mla_w_uv[j])
            op, rp = _mla_prompt(ap, *w)
            os_, rs = _mla_sample(as_, cache_mla_kv, j, page_table, *w)
            mla_p.append(rp)
            mla_s.append(rs)
            w_out = mla_w_o[j]
        elif i % N_MIXERS == 1:
            zp = _norm_proj(xp, g0, mlstm_w_in[j]).reshape(B, S, -1)
            zs = _norm_proj(xs, g0, mlstm_w_in[j]).reshape(Bs, Ts, -1)
            op, cp, nst_p, mp = _mlstm_prompt(zp, mlstm_b_gate[j])
            os_, cs, nst_s, ms = _mlstm_sample(zs, mlstm_b_gate[j], state_mlstm_c[j], state_mlstm_n[j],
                                               state_mlstm_m[j])
            mc_p.append(cp)
            mn_p.append(nst_p)
            mm_p.append(mp)
            mc_s.append(cs)
            mn_s.append(nst_s)
            mm_s.append(ms)
            w_out = mlstm_w_out[j]
        else:
            zp = _norm_proj(xp, g0, nsa_w_in[j]).reshape(B, S, -1)
            zs = _norm_proj(xs, g0, nsa_w_in[j]).reshape(Bs, Ts, -1)
            w_bd = _summ_weights(nsa_w_cmp[j])
            op, kcp, ksp, kwp = _nsa_prompt(zp, w_bd)
            os_, kcs, kss, kws = _nsa_sample(zs, cache_nsa_cmp, cache_nsa_sel, state_nsa_win[j], j,
                                             page_table, w_bd)
            cmp_p.append(kcp)
            cmp_s.append(kcs)
            sel_p.append(ksp)
            sel_s.append(kss)
            win_p.append(kwp)
            win_s.append(kws)
            w_out = nsa_w_out[j]
        xp = _proj_res(op.reshape(B * S, -1), w_out, xp)
        xs = _proj_res(os_.reshape(Bs * Ts, -1), w_out, xs)
        xp = _mlp_res(xp, norm_g[i, 1], mlp_w1[i], mlp_w2[i])
        xs = _mlp_res(xs, norm_g[i, 1], mlp_w1[i], mlp_w2[i])
    y_prompt = _final_norm(xp, final_norm_g).reshape(B, S, D)
    y_sample = _final_norm(xs, final_norm_g).reshape(Bs, Ts, D)
    return (y_prompt, y_sample,
            jnp.stack(mla_p), jnp.stack(mla_s),
            jnp.stack(mc_p), jnp.stack(mn_p), jnp.stack(mm_p),
            jnp.stack(mc_s), jnp.stack(mn_s), jnp.stack(mm_s),
            jnp.stack(cmp_p), jnp.stack(cmp_s),
            jnp.stack(sel_p), jnp.stack(sel_s),
            jnp.stack(win_p), jnp.stack(win_s))
```

```python
import functools

import jax
import jax.numpy as jnp
from jax import lax
from jax.experimental import pallas as pl
from jax.experimental.pallas import tpu as pltpu

F32 = jnp.float32
BF16 = jnp.bfloat16

D_MODEL = 1024
DEPTH = 4
N_MIXERS = 3
PAGE_SIZE = 128

MLA_HEADS = 16
MLA_NOPE_DIM = 64
MLA_ROPE_DIM = 32
MLA_V_DIM = 64
MLA_Q_LORA = 384
MLA_KV_LORA = 256
MLA_SCALE = (MLA_NOPE_DIM + MLA_ROPE_DIM) ** -0.5
ROPE_THETA = 10000.0

MLSTM_HEADS = 4
MLSTM_DQK = 128
MLSTM_DV = 256
MLSTM_CHUNK = 64

NSA_HEADS = 16
NSA_KV_HEADS = 4
NSA_HEAD_DIM = 64
CMP_BLOCK = 64
SEL_BLOCK = 64
TOP_K_BLOCKS = 16
WINDOW = 512
NSA_Q_COLS = NSA_HEADS * NSA_HEAD_DIM
NSA_KV_COLS = NSA_KV_HEADS * 2 * NSA_HEAD_DIM
NSA_SCALE = NSA_HEAD_DIM ** -0.5

D_FF = 4 * D_MODEL
Q_BLOCK = 128
RMS_EPS = 1e-6
NEG = -1e30
FORCE = 1e4

VMEM_LIMIT_BYTES = 56 * 1024 * 1024
FF_CHUNK = 512


def _row_tile(m):
    for t in (512, 256, 128):
        if m % t == 0:
            return t
    return m


def _rms_rows(x, g):
    return x * lax.rsqrt(jnp.mean(x * x, axis=-1, keepdims=True) + RMS_EPS) * g


def _norm_proj_body(x_ref, g_ref, w_ref, o_ref, *h_ref):
    h = _rms_rows(x_ref[...], g_ref[...])
    o_ref[...] = jnp.dot(h.astype(BF16), w_ref[...], preferred_element_type=F32)
    if h_ref:
        h_ref[0][...] = h


def _norm_proj(x, g, w, with_normed=False):
    m, d = x.shape
    n = w.shape[1]
    tm = _row_tile(m)
    out_specs = [pl.BlockSpec((tm, n), lambda i: (i, 0))]
    out_shape = [jax.ShapeDtypeStruct((m, n), F32)]
    if with_normed:
        out_specs.append(pl.BlockSpec((tm, d), lambda i: (i, 0)))
        out_shape.append(jax.ShapeDtypeStruct((m, d), F32))
    out = pl.pallas_call(
        _norm_proj_body,
        grid=(m // tm,),
        in_specs=[pl.BlockSpec((tm, d), lambda i: (i, 0)),
                  pl.BlockSpec((1, d), lambda i: (0, 0)),
                  pl.BlockSpec((d, n), lambda i: (0, 0))],
        out_specs=out_specs,
        out_shape=out_shape,
        compiler_params=pltpu.CompilerParams(dimension_semantics=("arbitrary",),
                                             vmem_limit_bytes=VMEM_LIMIT_BYTES),
        name="norm_proj",
    )(x, g.reshape(1, d), w.astype(BF16))
    return out if with_normed else out[0]


def _proj_res_body(a_ref, w_ref, r_ref, o_ref):
    o_ref[...] = r_ref[...] + jnp.dot(a_ref[...].astype(BF16), w_ref[...], preferred_element_type=F32)


def _proj_res(a, w, res):
    m, k = a.shape
    d = w.shape[1]
    tm = _row_tile(m)
    return pl.pallas_call(
        _proj_res_body,
        grid=(m // tm,),
        in_specs=[pl.BlockSpec((tm, k), lambda i: (i, 0)),
                  pl.BlockSpec((k, d), lambda i: (0, 0)),
                  pl.BlockSpec((tm, d), lambda i: (i, 0))],
        out_specs=pl.BlockSpec((tm, d), lambda i: (i, 0)),
        out_shape=jax.ShapeDtypeStruct((m, d), F32),
        compiler_params=pltpu.CompilerParams(dimension_semantics=("arbitrary",),
                                             vmem_limit_bytes=VMEM_LIMIT_BYTES),
        name="proj_res",
    )(a, w.astype(BF16), res)


def _mlp_body(x_ref, g_ref, w1_ref, w2_ref, o_ref):
    x = x_ref[...]
    h = _rms_rows(x, g_ref[...]).astype(BF16)
    acc = x
    for c in range(D_FF // FF_CHUNK):
        a = jnp.dot(h, w1_ref[:, c * FF_CHUNK:(c + 1) * FF_CHUNK], preferred_element_type=F32)
        a = jnp.maximum(a, 0.0)
        acc = acc + jnp.dot((a * a).astype(BF16), w2_ref[c * FF_CHUNK:(c + 1) * FF_CHUNK, :],
                            preferred_element_type=F32)
    o_ref[...] = acc


def _mlp_res(x, g, w1, w2):
    m, d = x.shape
    tm = _row_tile(m)
    return pl.pallas_call(
        _mlp_body,
        grid=(m // tm,),
        in_specs=[pl.BlockSpec((tm, d), lambda i: (i, 0)),
                  pl.BlockSpec((1, d), lambda i: (0, 0)),
                  pl.BlockSpec((d, D_FF), lambda i: (0, 0)),
                  pl.BlockSpec((D_FF, d), lambda i: (0, 0))],
        out_specs=pl.BlockSpec((tm, d), lambda i: (i, 0)),
        out_shape=jax.ShapeDtypeStruct((m, d), F32),
        compiler_params=pltpu.CompilerParams(dimension_semantics=("arbitrary",),
                                             vmem_limit_bytes=VMEM_LIMIT_BYTES),
        name="mlp_res",
    )(x, g.reshape(1, d), w1.astype(BF16), w2.astype(BF16))


def _final_norm_body(x_ref, g_ref, o_ref):
    o_ref[...] = _rms_rows(x_ref[...], g_ref[...])


def _final_norm(x, g):
    m, d = x.shape
    tm = _row_tile(m)
    return pl.pallas_call(
        _final_norm_body,
        grid=(m // tm,),
        in_specs=[pl.BlockSpec((tm, d), lambda i: (i, 0)), pl.BlockSpec((1, d), lambda i: (0, 0))],
        out_specs=pl.BlockSpec((tm, d), lambda i: (i, 0)),
        out_shape=jax.ShapeDtypeStruct((m, d), F32),
        compiler_params=pltpu.CompilerParams(dimension_semantics=("arbitrary",)),
        name="final_norm",
    )(x, g.reshape(1, d))


def _dot_nt(a, b):
    return lax.dot_general(a, b, (((1,), (1,)), ((), ())), preferred_element_type=F32)


SUMM_L_PER_STEP = 8
SUMM_COLS = NSA_KV_COLS
SUMM_HALF = SUMM_COLS // 2


def _summ_weights(w_cmp):
    wk = jnp.stack([w_cmp[0], w_cmp[1], w_cmp[0], w_cmp[1]], axis=0)
    bd = jnp.einsum('kj,klde->lkdje', jnp.eye(4, dtype=F32), wk)
    return bd.reshape(CMP_BLOCK, SUMM_HALF, SUMM_HALF).astype(BF16)


def _summarize_body(x_ref, w_ref, o_ref):
    @pl.when(pl.program_id(1) == 0)
    def _():
        o_ref[...] = jnp.zeros_like(o_ref)

    lo = o_ref[:, :SUMM_HALF]
    hi = o_ref[:, SUMM_HALF:]
    for li in range(SUMM_L_PER_STEP):
        x = x_ref[:, li * SUMM_COLS:(li + 1) * SUMM_COLS].astype(BF16)
        w = w_ref[li]
        lo = lo + jnp.dot(x[:, :SUMM_HALF], w, preferred_element_type=F32)
        hi = hi + jnp.dot(x[:, SUMM_HALF:], w, preferred_element_type=F32)
    o_ref[:, :SUMM_HALF] = lo
    o_ref[:, SUMM_HALF:] = hi


def _summarize_blocks(x2d, w_bd):
    nb = x2d.shape[0]
    p = 512 if nb % 512 == 0 else nb
    step_cols = SUMM_L_PER_STEP * SUMM_COLS
    return pl.pallas_call(
        _summarize_body,
        grid=(nb // p, CMP_BLOCK // SUMM_L_PER_STEP),
        in_specs=[pl.BlockSpec((p, step_cols), lambda i, l: (i, l)),
                  pl.BlockSpec((SUMM_L_PER_STEP, SUMM_HALF, SUMM_HALF), lambda i, l: (l, 0, 0))],
        out_specs=pl.BlockSpec((p, SUMM_COLS), lambda i, l: (i, 0)),
        out_shape=jax.ShapeDtypeStruct((nb, SUMM_COLS), F32),
        compiler_params=pltpu.CompilerParams(dimension_semantics=("arbitrary", "arbitrary"),
                                             vmem_limit_bytes=VMEM_LIMIT_BYTES),
        name="nsa_summarize",
    )(x2d, w_bd)


SUMM_D_PER_STEP = 8
SUMM_PAGE_BLOCKS = PAGE_SIZE // CMP_BLOCK
SUMM_PAGE_OUT = 2 * SUMM_PAGE_BLOCKS * NSA_HEAD_DIM


def _summ_weights_paged(w_cmp):
    n = 2 * SUMM_PAGE_BLOCKS
    wk = jnp.stack([w_cmp[c] for c in range(2) for _ in range(SUMM_PAGE_BLOCKS)], axis=0)
    bd = jnp.einsum('kj,klde->dklje', jnp.eye(n, dtype=F32), wk)
    return bd.reshape(NSA_HEAD_DIM, 2 * PAGE_SIZE, SUMM_PAGE_OUT).astype(BF16)


def _summarize_pool_body(k_ref, v_ref, w_ref, o_ref):
    dg = pl.program_id(2)

    @pl.when(dg == 0)
    def _():
        o_ref[...] = jnp.zeros_like(o_ref)

    acc = o_ref[...]
    for dd in range(SUMM_D_PER_STEP):
        lhs = jnp.concatenate([k_ref[:, dd, :], v_ref[:, dd, :]], axis=1).astype(BF16)
        acc = acc + jnp.dot(lhs, w_ref[dg * SUMM_D_PER_STEP + dd], preferred_element_type=F32)
    o_ref[...] = acc


def _summarize_pool(cache, w_paged):
    pool = cache.shape[0]
    cm = _nsa_channel_major(cache)
    p = 512 if pool % 512 == 0 else pool
    d_steps = NSA_HEAD_DIM // SUMM_D_PER_STEP
    rows_per_group = NSA_KV_PAIR // SUMM_D_PER_STEP
    out = pl.pallas_call(
        _summarize_pool_body,
        grid=(pool // p, NSA_KV_HEADS, d_steps),
        in_specs=[pl.BlockSpec((p, SUMM_D_PER_STEP, PAGE_SIZE), lambda i, g, d: (i, g * rows_per_group + d, 0)),
                  pl.BlockSpec((p, SUMM_D_PER_STEP, PAGE_SIZE),
                               lambda i, g, d: (i, g * rows_per_group + d_steps + d, 0)),
                  pl.BlockSpec((NSA_HEAD_DIM, 2 * PAGE_SIZE, SUMM_PAGE_OUT), lambda i, g, d: (0, 0, 0))],
        out_specs=pl.BlockSpec((p, SUMM_PAGE_OUT), lambda i, g, d: (i, g)),
        out_shape=jax.ShapeDtypeStruct((pool, NSA_KV_HEADS * SUMM_PAGE_OUT), F32),
        compiler_params=pltpu.CompilerParams(dimension_semantics=("arbitrary",) * 3,
                                             vmem_limit_bytes=VMEM_LIMIT_BYTES),
        name="nsa_summarize_pool",
    )(cm, cm, w_paged)
    return out.reshape(pool, NSA_KV_HEADS, 2, SUMM_PAGE_BLOCKS, NSA_HEAD_DIM)


NSA_TQ = 128
NSA_TK = 512
NSA_GROUP = NSA_HEADS // NSA_KV_HEADS
NSA_BAND = WINDOW + NSA_TQ


def _softmax_rows(s, valid):
    m = jnp.max(s, axis=1, keepdims=True)
    e = jnp.where(valid, jnp.exp(s - m), 0.0)
    l = jnp.sum(e, axis=1, keepdims=True)
    return e / jnp.where(l > 0.0, l, 1.0)


def _nsa_prompt_body(q_ref, ksum_ref, vsum_ref, ks_ref, vs_ref, kw_ref, vw_ref, gate_ref, o_ref):
    g = pl.program_id(1)
    q0 = pl.program_id(2) * NSA_TQ
    rows = NSA_GROUP * NSA_TQ
    nblk = ksum_ref.shape[2]
    q = q_ref[0].reshape(rows, NSA_HEAD_DIM) * jnp.asarray(NSA_SCALE, BF16)

    row = lax.broadcasted_iota(jnp.int32, (rows, 1), 0)
    qpos = q0 + (row & (NSA_TQ - 1))
    head = g * NSA_GROUP + (row >> 7)
    slope = jnp.exp((head + 1).astype(F32) * (-8.0 / NSA_HEADS * 0.6931471805599453))

    s = _dot_nt(q, ksum_ref[0, 0])
    blk_end = (lax.broadcasted_iota(jnp.int32, (1, nblk), 1) + 1) * CMP_BLOCK - 1
    dist = qpos - blk_end
    vis = dist >= 0
    p = _softmax_rows(jnp.where(vis, s - slope * dist.astype(F32), NEG), vis)
    o_cmp = jnp.dot(p.astype(BF16), vsum_ref[0, 0], preferred_element_type=F32)
    imp = p[0:NSA_TQ]
    for h in range(1, NSA_GROUP):
        imp = imp + p[h * NSA_TQ:(h + 1) * NSA_TQ]

    tok = q0 + lax.broadcasted_iota(jnp.int32, (NSA_TQ, 1), 0)
    cur = tok >> 6
    jj = lax.broadcasted_iota(jnp.int32, (NSA_TQ, nblk), 1)
    forced = jnp.where(jj == 0, 1, jnp.where(jj == cur, 1, jnp.where(jj == cur - 1, 1, 0)))
    score = jnp.where(forced > 0, FORCE, jnp.where(jj <= cur, imp, -FORCE))
    rank = jnp.zeros((NSA_TQ, nblk), jnp.int32)
    for i in range(nblk):
        ci = score[:, i:i + 1]
        rank = rank + jnp.where(ci > score, 1, jnp.where(ci == score, jnp.where(jj > i, 1, 0), 0))
    sel = jnp.where(rank < min(TOP_K_BLOCKS, nblk), 1.0, 0.0).astype(BF16)
    sel4 = jnp.concatenate([sel] * NSA_GROUP, axis=0)

    def sel_step(kt, carry):
        m, l, acc = carry
        k0 = pl.multiple_of(kt * NSA_TK, NSA_TK)
        k = ks_ref[0, 0, pl.ds(k0, NSA_TK), :]
        v = vs_ref[0, 0, pl.ds(k0, NSA_TK), :]
        kcol = k0 + lax.broadcasted_iota(jnp.int32, (nblk, NSA_TK), 1)
        expand = jnp.where((kcol >> 6) == lax.broadcasted_iota(jnp.int32, (nblk, NSA_TK), 0), 1.0, 0.0)
        picked = jnp.dot(sel4, expand.astype(BF16), preferred_element_type=F32)
        dist = qpos - (k0 + lax.broadcasted_iota(jnp.int32, (1, NSA_TK), 1))
        valid = jnp.where(dist >= 0, picked, 0.0) > 0.5
        s = jnp.where(valid, _dot_nt(q, k) - slope * dist.astype(F32), NEG)
        m_new = jnp.maximum(m, jnp.max(s, axis=1, keepdims=True))
        alpha = jnp.exp(m - m_new)
        e = jnp.where(valid, jnp.exp(s - m_new), 0.0)
        l = alpha * l + jnp.sum(e, axis=1, keepdims=True)
        acc = alpha * acc + jnp.dot(e.astype(BF16), v, preferred_element_type=F32)
        return m_new, l, acc

    n_kt = (q0 + NSA_TQ + NSA_TK - 1) // NSA_TK
    init = (jnp.full((rows, 1), NEG, F32), jnp.zeros((rows, 1), F32), jnp.zeros((rows, NSA_HEAD_DIM), F32))
    _, l, acc = lax.fori_loop(0, n_kt, sel_step, init)
    o_sel = acc / jnp.where(l > 0.0, l, 1.0)

    w0 = pl.multiple_of(jnp.maximum(q0 - WINDOW, 0), NSA_TQ)
    kw = kw_ref[0, 0, pl.ds(w0, NSA_BAND), :]
    vw = vw_ref[0, 0, pl.ds(w0, NSA_BAND), :]
    dist = qpos - (w0 + lax.broadcasted_iota(jnp.int32, (1, NSA_BAND), 1))
    valid = jnp.where(dist >= 0, jnp.where(dist < WINDOW, 1, 0), 0) > 0
    p = _softmax_rows(jnp.where(valid, _dot_nt(q, kw) - slope * dist.astype(F32), NEG), valid)
    o_win = jnp.dot(p.astype(BF16), vw, preferred_element_type=F32)

    gate = jax.nn.sigmoid(gate_ref[0, 0])
    outs = []
    for h in range(NSA_GROUP):
        r = slice(h * NSA_TQ, (h + 1) * NSA_TQ)
        outs.append(gate[:, 3 * h:3 * h + 1] * o_cmp[r] + gate[:, 3 * h + 1:3 * h + 2] * o_sel[r]
                    + gate[:, 3 * h + 2:3 * h + 3] * o_win[r])
    o_ref[0] = jnp.concatenate(outs, axis=-1)


def _nsa_prompt_attend(q, summ, kv_s, kv_w, gate):
    b, h, s, dh = q.shape
    g = NSA_KV_HEADS
    nblk = summ.shape[3]
    assert s % NSA_TK == 0 and s >= NSA_BAND
    seq_spec = pl.BlockSpec((1, 1, s, dh), lambda bi, gi, qi: (bi, gi, 0, 0))
    sum_spec = pl.BlockSpec((1, 1, nblk, dh), lambda bi, gi, qi: (bi, gi, 0, 0))
    return pl.pallas_call(
        _nsa_prompt_body,
        grid=(b, g, s // NSA_TQ),
        in_specs=[pl.BlockSpec((1, NSA_GROUP, NSA_TQ, dh), lambda bi, gi, qi: (bi, gi, qi, 0)),
                  sum_spec, sum_spec, seq_spec, seq_spec, seq_spec, seq_spec,
                  pl.BlockSpec((1, 1, NSA_TQ, 3 * NSA_GROUP), lambda bi, gi, qi: (bi, gi, qi, 0))],
        out_specs=pl.BlockSpec((1, NSA_TQ, NSA_GROUP * dh), lambda bi, gi, qi: (bi, qi, gi)),
        out_shape=jax.ShapeDtypeStruct((b, s, h * dh), F32),
        compiler_params=pltpu.CompilerParams(dimension_semantics=("arbitrary",) * 3,
                                             vmem_limit_bytes=VMEM_LIMIT_BYTES),
        name="nsa_prompt_attend",
    )(q, summ[0], summ[1], kv_s[0], kv_s[1], kv_w[0], kv_w[1], gate)


def _heads_matmul_body(x_ref, w_ref, o_ref):
    o_ref[0] = jnp.dot(x_ref[0].astype(BF16), w_ref[0], preferred_element_type=F32)


def _heads_matmul(x, w):
    h, m, k = x.shape
    n = w.shape[2]
    return pl.pallas_call(
        _heads_matmul_body,
        grid=(h,),
        in_specs=[pl.BlockSpec((1, m, k), lambda i: (i, 0, 0)), pl.BlockSpec((1, k, n), lambda i: (i, 0, 0))],
        out_specs=pl.BlockSpec((1, m, n), lambda i: (i, 0, 0)),
        out_shape=jax.ShapeDtypeStruct((h, m, n), F32),
        compiler_params=pltpu.CompilerParams(dimension_semantics=("arbitrary",)),
        name="heads_matmul",
    )(x, w.astype(BF16))


MLA_TILE = 512
MLA_QK_PAD = 128
MLA_HEAD_PAIR = 2


def _mla_flash_body(q_ref, k_ref, v_ref, o_ref):
    qi = pl.program_id(2)
    t = MLA_TILE
    causal = lax.broadcasted_iota(jnp.int32, (t, t), 0) >= lax.broadcasted_iota(jnp.int32, (t, t), 1)
    outs = []
    for hh in range(MLA_HEAD_PAIR):
        q = q_ref[0, hh]

        def step(kt, carry, diagonal, hh=hh, q=q):
            m, l, acc = carry
            k0 = pl.multiple_of(kt * t, t)
            s = _dot_nt(q, k_ref[0, hh, pl.ds(k0, t), :])
            if diagonal:
                s = jnp.where(causal, s, NEG)
            m_new = jnp.maximum(m, jnp.max(s, axis=1, keepdims=True))
            alpha = jnp.exp(m - m_new)
            e = jnp.exp(s - m_new)
            l = alpha * l + jnp.sum(e, axis=1, keepdims=True)
            acc = alpha * acc + jnp.dot(e.astype(BF16), v_ref[0, hh, pl.ds(k0, t), :],
                                        preferred_element_type=F32)
            return m_new, l, acc

        init = (jnp.full((t, 1), NEG, F32), jnp.zeros((t, 1), F32), jnp.zeros((t, MLA_V_DIM), F32))
        carry = lax.fori_loop(0, qi, functools.partial(step, diagonal=False), init)
        _, l, acc = step(qi, carry, True)
        outs.append(acc / l)
    o_ref[0] = jnp.concatenate(outs, axis=-1)


def _mla_flash(q, k, v):
    b, h, s, _ = q.shape
    t = MLA_TILE
    assert s % t == 0
    kv_map = lambda bi, hi, qi: (bi, hi, 0, 0)
    return pl.pallas_call(
        _mla_flash_body,
        grid=(b, h // MLA_HEAD_PAIR, s // t),
        in_specs=[pl.BlockSpec((1, MLA_HEAD_PAIR, t, MLA_QK_PAD), lambda bi, hi, qi: (bi, hi, qi, 0)),
                  pl.BlockSpec((1, MLA_HEAD_PAIR, s, MLA_QK_PAD), kv_map),
                  pl.BlockSpec((1, MLA_HEAD_PAIR, s, MLA_V_DIM), kv_map)],
        out_specs=pl.BlockSpec((1, t, MLA_HEAD_PAIR * MLA_V_DIM), lambda bi, hi, qi: (bi, qi, hi)),
        out_shape=jax.ShapeDtypeStruct((b, s, h * MLA_V_DIM), F32),
        compiler_params=pltpu.CompilerParams(dimension_semantics=("arbitrary",) * 3,
                                             vmem_limit_bytes=VMEM_LIMIT_BYTES),
        name="mla_flash",
    )(q, k, v)


MLA_PAGES_PER_STEP = 16
MLA_ROW = MLA_KV_LORA + MLA_ROPE_DIM


def _mla_decode_body(pt_ref, q_ref, new_ref, *refs):
    del pt_ref
    npg = MLA_PAGES_PER_STEP
    pages, o_ref = refs[:npg], refs[npg]
    m_sc, l_sc, acc_sc = refs[npg + 1:]
    step = pl.program_id(1)

    @pl.when(step == 0)
    def _():
        m_sc[...] = jnp.full_like(m_sc, NEG)
        l_sc[...] = jnp.zeros_like(l_sc)
        acc_sc[...] = jnp.zeros_like(acc_sc)

    qf = q_ref[0]
    q = qf.astype(BF16)
    kt = jnp.concatenate([pages[p][0, 0].astype(BF16) for p in range(npg)], axis=1)
    s = jnp.dot(q, kt, preferred_element_type=F32)
    m = m_sc[...]
    m_new = jnp.maximum(m, jnp.max(s, axis=1, keepdims=True))
    alpha = jnp.exp(m - m_new)
    e = jnp.exp(s - m_new)
    l_new = alpha * l_sc[...] + jnp.sum(e, axis=1, keepdims=True)
    acc_new = alpha * acc_sc[...] + _dot_nt(e.astype(BF16), kt[:MLA_KV_LORA])
    m_sc[...] = m_new
    l_sc[...] = l_new
    acc_sc[...] = acc_new

    @pl.when(step == pl.num_programs(1) - 1)
    def _():
        new = new_ref[0]
        s_new = jnp.sum(qf * new, axis=1, keepdims=True)
        m_fin = jnp.maximum(m_new, s_new)
        a = jnp.exp(m_new - m_fin)
        e_new = jnp.exp(s_new - m_fin)
        o_ref[0] = (a * acc_new + e_new * new[:, :MLA_KV_LORA]) / (a * l_new + e_new)


def _mla_decode(q, new_rows, cache, layer, page_table):
    b, h, w = q.shape
    n_pages = page_table.shape[1]
    npg = MLA_PAGES_PER_STEP
    assert n_pages % npg == 0 and cache.shape[2] == PAGE_SIZE
    cache = cache.transpose(0, 1, 3, 2)

    def page_spec(p):
        return pl.BlockSpec((1, 1, w, PAGE_SIZE), lambda bi, si, pt: (layer, pt[bi, si * npg + p], 0, 0))

    grid_spec = pltpu.PrefetchScalarGridSpec(
        num_scalar_prefetch=1,
        grid=(b, n_pages // npg),
        in_specs=[pl.BlockSpec((1, h, w), lambda bi, si, pt: (bi, 0, 0)),
                  pl.BlockSpec((1, 1, w), lambda bi, si, pt: (bi, 0, 0))] + [page_spec(p) for p in range(npg)],
        out_specs=pl.BlockSpec((1, h, MLA_KV_LORA), lambda bi, si, pt: (bi, 0, 0)),
        scratch_shapes=[pltpu.VMEM((h, 1), F32), pltpu.VMEM((h, 1), F32), pltpu.VMEM((h, MLA_KV_LORA), F32)],
    )
    return pl.pallas_call(
        _mla_decode_body,
        grid_spec=grid_spec,
        out_shape=jax.ShapeDtypeStruct((b, h, MLA_KV_LORA), F32),
        compiler_params=pltpu.CompilerParams(dimension_semantics=("arbitrary", "arbitrary"),
                                             vmem_limit_bytes=VMEM_LIMIT_BYTES),
        name="mla_decode",
    )(page_table, q, new_rows, *([cache] * npg))


NSA_KV_PAIR = 2 * NSA_HEAD_DIM


def _alibi_col(first_head, n):
    head = first_head + lax.broadcasted_iota(jnp.int32, (n, 1), 0)
    return jnp.exp((head + 1).astype(F32) * (-8.0 / NSA_HEADS * 0.6931471805599453))


def _nsa_cmp_sample_body(q_ref, k_ref, v_ref, o_ref, idx_ref):
    nblk = k_ref.shape[2]
    pos = nblk * CMP_BLOCK
    blk_end = (lax.broadcasted_iota(jnp.int32, (1, nblk), 1) + 1) * CMP_BLOCK - 1
    dist = (pos - blk_end).astype(F32)
    imps = []
    for g in range(NSA_KV_HEADS):
        q = q_ref[0, g].astype(BF16)
        s = _dot_nt(q, k_ref[0, g]) - _alibi_col(g * NSA_GROUP, NSA_GROUP) * dist
        p = _softmax_rows(s, jnp.full(s.shape, True))
        o_ref[0, g] = jnp.dot(p.astype(BF16), v_ref[0, g], preferred_element_type=F32)
        imps.append(jnp.sum(p, axis=0, keepdims=True))
    imp = jnp.concatenate(imps, axis=0)
    jj = lax.broadcasted_iota(jnp.int32, (NSA_KV_HEADS, nblk), 1)
    score = jnp.where(jj == 0, FORCE, jnp.where(jj == nblk - 1, FORCE, imp))
    rank = jnp.zeros((NSA_KV_HEADS, nblk), jnp.int32)
    for i in range(nblk):
        ci = score[:, i:i + 1]
        rank = rank + jnp.where(ci > score, 1, jnp.where(ci == score, jnp.where(jj > i, 1, 0), 0))
    n_pick = min(TOP_K_BLOCKS, nblk + 1) - 1
    cols = [jnp.sum(jnp.where(rank == r, jj, 0), axis=1, keepdims=True) for r in range(n_pick)]
    cols.append(jnp.full((NSA_KV_HEADS, 1), nblk, jnp.int32))
    idx_ref[0] = jnp.concatenate(cols, axis=1)


def _nsa_cmp_sample(q, summ):
    b = q.shape[0]
    nblk = summ.shape[3]
    n_sel = min(TOP_K_BLOCKS, nblk + 1)
    sum_spec = pl.BlockSpec((1, NSA_KV_HEADS, nblk, NSA_HEAD_DIM), lambda i: (i, 0, 0, 0))
    return pl.pallas_call(
        _nsa_cmp_sample_body,
        grid=(b,),
        in_specs=[pl.BlockSpec((1, NSA_KV_HEADS, NSA_GROUP, NSA_HEAD_DIM), lambda i: (i, 0, 0, 0)),
                  sum_spec, sum_spec],
        out_specs=[pl.BlockSpec((1, NSA_KV_HEADS, NSA_GROUP, NSA_HEAD_DIM), lambda i: (i, 0, 0, 0)),
                   pl.BlockSpec((1, NSA_KV_HEADS, n_sel), lambda i: (i, 0, 0))],
        out_shape=[jax.ShapeDtypeStruct((b, NSA_KV_HEADS, NSA_GROUP, NSA_HEAD_DIM), F32),
                   jax.ShapeDtypeStruct((b, NSA_KV_HEADS, n_sel), jnp.int32)],
        compiler_params=pltpu.CompilerParams(dimension_semantics=("arbitrary",)),
        name="nsa_cmp_sample",
    )(q, summ[0], summ[1])


def _nsa_sel_sample_body(n_past, idx_ref, pt_ref, q_ref, new_ref, *refs):
    del pt_ref
    n_sel = len(refs) - 1
    pages, o_ref = refs[:n_sel], refs[n_sel]
    bi, g = pl.program_id(0), pl.program_id(1)
    pos = n_past * SEL_BLOCK
    per_page = PAGE_SIZE // SEL_BLOCK
    qf = q_ref[0, 0]
    q = qf.astype(BF16)
    slope = _alibi_col(g * NSA_GROUP, NSA_GROUP)
    tok = lax.broadcasted_iota(jnp.int32, (1, PAGE_SIZE), 1)
    vts, scores = [], []
    for k in range(n_sel):
        j = idx_ref[bi, g * n_sel + k]
        kv = pages[k][0].astype(BF16)
        dist = (pos - ((j // per_page) * PAGE_SIZE + tok)).astype(F32)
        s = jnp.dot(q, kv[:NSA_HEAD_DIM], preferred_element_type=F32) - slope * dist
        keep = jnp.where(j < n_past, jnp.where((tok // SEL_BLOCK) == (j % per_page), 1, 0), 0) > 0
        scores.append(jnp.where(keep, s, NEG))
        vts.append(kv[NSA_HEAD_DIM:])
    new = new_ref[0, 0]
    s_new = jnp.sum(qf * new[0:1], axis=1, keepdims=True)
    m = s_new
    for s in scores:
        m = jnp.maximum(m, jnp.max(s, axis=1, keepdims=True))
    e_new = jnp.exp(s_new - m)
    l = e_new
    acc = e_new * new[1:2]
    for s, vt in zip(scores, vts):
        e = jnp.exp(s - m)
        l = l + jnp.sum(e, axis=1, keepdims=True)
        acc = acc + _dot_nt(e.astype(BF16), vt)
    o_ref[0, 0] = acc / l


def _nsa_channel_major(cache):
    lead = cache.ndim - 4
    perm = tuple(range(lead)) + (lead + 1, lead + 2, lead + 3, lead)
    t = cache.transpose(perm)
    return t.reshape(t.shape[:lead] + (NSA_KV_COLS, t.shape[-1]))


def _nsa_sel_sample(q, new_kv, cache_sel, layer, page_table, idx):
    b, g, n_sel = idx.shape
    n_pages = page_table.shape[1]
    per_page = PAGE_SIZE // SEL_BLOCK
    n_past = n_pages * per_page
    cache = _nsa_channel_major(cache_sel)

    def page_spec(k):
        def index(bi, gi, idx_ref, pt_ref):
            j = jnp.minimum(idx_ref[bi, gi * n_sel + k], n_past - 1)
            return (layer, pt_ref[bi, j // per_page], gi, 0)
        return pl.BlockSpec((None, 1, NSA_KV_PAIR, PAGE_SIZE), index)

    grp_spec = lambda rows: pl.BlockSpec((1, 1, rows, NSA_HEAD_DIM), lambda bi, gi, i_, p_: (bi, gi, 0, 0))
    grid_spec = pltpu.PrefetchScalarGridSpec(
        num_scalar_prefetch=2,
        grid=(b, g),
        in_specs=[grp_spec(NSA_GROUP), grp_spec(2)] + [page_spec(k) for k in range(n_sel)],
        out_specs=grp_spec(NSA_GROUP),
    )
    return pl.pallas_call(
        functools.partial(_nsa_sel_sample_body, n_past),
        grid_spec=grid_spec,
        out_shape=jax.ShapeDtypeStruct((b, g, NSA_GROUP, NSA_HEAD_DIM), F32),
        compiler_params=pltpu.CompilerParams(dimension_semantics=("arbitrary", "arbitrary")),
        name="nsa_sel_sample",
    )(idx.reshape(b, g * n_sel), page_table, q, new_kv, *([cache] * n_sel))


def _nsa_win_sample_body(q_ref, new_ref, win_ref, o_ref, nwin_ref):
    wbuf = win_ref.shape[2]
    win = win_ref[0]
    tok = lax.broadcasted_iota(jnp.int32, (1, wbuf), 1)
    dist = wbuf - tok
    valid = dist < WINDOW
    for g in range(NSA_KV_HEADS):
        r0 = g * NSA_KV_PAIR
        qf = q_ref[0, g]
        new = new_ref[0, g]
        kt = win[r0:r0 + NSA_HEAD_DIM].astype(BF16)
        vt = win[r0 + NSA_HEAD_DIM:r0 + NSA_KV_PAIR].astype(BF16)
        slope = _alibi_col(g * NSA_GROUP, NSA_GROUP)
        s = jnp.dot(qf.astype(BF16), kt, preferred_element_type=F32) - slope * dist.astype(F32)
        s = jnp.where(valid, s, NEG)
        s_new = jnp.sum(qf * new[0:1], axis=1, keepdims=True)
        m = jnp.maximum(s_new, jnp.max(s, axis=1, keepdims=True))
        e = jnp.exp(s - m)
        e_new = jnp.exp(s_new - m)
        acc = e_new * new[1:2] + _dot_nt(e.astype(BF16), vt)
        o_ref[0, g] = acc / (e_new + jnp.sum(e, axis=1, keepdims=True))
    cols = win.shape[0]
    eye = (lax.broadcasted_iota(jnp.int32, (cols, cols), 0) == lax.broadcasted_iota(jnp.int32, (cols, cols), 1))
    new_row = jnp.concatenate([new_ref[0, g][c:c + 1] for g in range(NSA_KV_HEADS) for c in range(2)], axis=1)
    new_col = jnp.sum(jnp.where(eye, new_row, 0.0), axis=1, keepdims=True)
    nwin_ref[0] = jnp.where(tok == wbuf - 1, new_col, pltpu.roll(win, wbuf - 1, 1))


def _nsa_win_sample(q, new_kv, win_state):
    b, wbuf = win_state.shape[:2]
    assert wbuf == WINDOW
    win = _nsa_channel_major(win_state)
    q_spec = pl.BlockSpec((1, NSA_KV_HEADS, NSA_GROUP, NSA_HEAD_DIM), lambda i: (i, 0, 0, 0))
    win_spec = pl.BlockSpec((1, NSA_KV_COLS, wbuf), lambda i: (i, 0, 0))
    return pl.pallas_call(
        _nsa_win_sample_body,
        grid=(b,),
        in_specs=[q_spec, pl.BlockSpec((1, NSA_KV_HEADS, 2, NSA_HEAD_DIM), lambda i: (i, 0, 0, 0)), win_spec],
        out_specs=[q_spec, win_spec],
        out_shape=[jax.ShapeDtypeStruct((b, NSA_KV_HEADS, NSA_GROUP, NSA_HEAD_DIM), F32),
                   jax.ShapeDtypeStruct((b, NSA_KV_COLS, wbuf), F32)],
        compiler_params=pltpu.CompilerParams(dimension_semantics=("arbitrary",)),
        name="nsa_win_sample",
    )(q, new_kv, win)


MLSTM_KERNEL_CHUNK = 256
MLSTM_QK_COLS = MLSTM_HEADS * MLSTM_DQK
MLSTM_V_COLS = MLSTM_HEADS * MLSTM_DV


def _log_sigmoid(x):
    return jnp.minimum(x, 0.0) - jnp.log(1.0 + jnp.exp(-jnp.abs(x)))


def _mlstm_chunk_body(q_ref, k_ref, v_ref, og_ref, vt_ref, ig_ref, fg_ref,
                      h_ref, c_ref, n_ref, m_ref, c_sc, n_sc, m_sc):
    chunk = pl.program_id(2)

    @pl.when(chunk == 0)
    def _():
        c_sc[...] = jnp.zeros_like(c_sc)
        n_sc[...] = jnp.zeros_like(n_sc)
        m_sc[...] = jnp.zeros_like(m_sc)

    L = q_ref.shape[1]
    q = q_ref[0]
    qb = q.astype(BF16)
    kb = (k_ref[0] * (MLSTM_DQK ** -0.5)).astype(BF16)
    i_row = ig_ref[0, 0, 0]
    f_row = _log_sigmoid(fg_ref[0, 0, 0])
    tt = lax.broadcasted_iota(jnp.int32, (L, L), 0)
    ss = lax.broadcasted_iota(jnp.int32, (L, L), 1)
    tri = ss <= tt
    b_col = jnp.sum(jnp.where(tri, f_row, 0.0), axis=1, keepdims=True)
    b_row = jnp.sum(jnp.where(tt == ss, b_col, 0.0), axis=0, keepdims=True)
    m = m_sc[...]
    c = c_sc[...]
    n = n_sc[...]
    d = jnp.where(tri, b_col - b_row + i_row, NEG)
    inter = b_col + m
    mt = jnp.maximum(inter, jnp.max(d, axis=1, keepdims=True))
    w = jnp.exp(d - mt)
    gq = jnp.exp(inter - mt)
    a = w * _dot_nt(qb, kb)
    num = (jnp.dot(a.astype(BF16), v_ref[0].astype(BF16), preferred_element_type=F32)
           + gq * _dot_nt(qb, c.astype(BF16)))
    den = jnp.sum(a, axis=1, keepdims=True) + gq * jnp.sum(q * n, axis=1, keepdims=True)
    hc = num / jnp.maximum(jnp.abs(den), jnp.exp(-mt))
    h_ref[0] = hc * jax.nn.sigmoid(og_ref[0])

    b_last = b_col[L - 1:L, :]
    m_new = mt[L - 1:L, :]
    wl = jnp.exp(b_last - b_row + i_row - m_new)
    gl = jnp.exp(b_last + m - m_new)
    c_new = gl * c + jnp.dot((vt_ref[0, 0] * wl).astype(BF16), kb, preferred_element_type=F32)
    wl8 = jnp.broadcast_to(wl, (8, L)).astype(BF16)
    n_new = gl * n + jnp.dot(wl8, kb, preferred_element_type=F32)[0:1]
    c_sc[...] = c_new
    n_sc[...] = n_new
    m_sc[...] = m_new

    @pl.when(chunk == pl.num_programs(2) - 1)
    def _():
        c_ref[0, 0] = c_new
        n_ref[0, 0] = n_new
        m_ref[0, 0] = m_new


def _mlstm_prompt(z, b_gate):
    B, S, _ = z.shape
    H, L = MLSTM_HEADS, MLSTM_KERNEL_CHUNK
    assert S % L == 0
    nc = S // L
    kblk, vblk = MLSTM_QK_COLS // MLSTM_DQK, (2 * MLSTM_QK_COLS) // MLSTM_DV
    gates = z[..., 2 * MLSTM_QK_COLS + 2 * MLSTM_V_COLS:] + b_gate
    gates = gates.reshape(B, nc, L, 2, H).transpose(3, 0, 4, 1, 2)[:, :, :, :, None, :]
    vt = z[..., 2 * MLSTM_QK_COLS:2 * MLSTM_QK_COLS + MLSTM_V_COLS].reshape(B, S, H, MLSTM_DV)
    vt = vt.transpose(0, 2, 3, 1)
    gate_spec = pl.BlockSpec((1, 1, 1, 1, L), lambda b, h, c: (b, h, c, 0, 0))
    state = lambda r, w: pl.BlockSpec((1, 1, r, w), lambda b, h, c: (b, h, 0, 0))
    hs, c, n, m = pl.pallas_call(
        _mlstm_chunk_body,
        grid=(B, H, nc),
        in_specs=[pl.BlockSpec((1, L, MLSTM_DQK), lambda b, h, c: (b, c, h)),
                  pl.BlockSpec((1, L, MLSTM_DQK), lambda b, h, c: (b, c, kblk + h)),
                  pl.BlockSpec((1, L, MLSTM_DV), lambda b, h, c: (b, c, vblk + h)),
                  pl.BlockSpec((1, L, MLSTM_DV), lambda b, h, c: (b, c, vblk + H + h)),
                  pl.BlockSpec((1, 1, MLSTM_DV, L), lambda b, h, c: (b, h, 0, c)),
                  gate_spec, gate_spec],
        out_specs=[pl.BlockSpec((1, L, MLSTM_DV), lambda b, h, c: (b, c, h)),
                   state(MLSTM_DV, MLSTM_DQK), state(1, MLSTM_DQK), state(1, 1)],
        out_shape=[jax.ShapeDtypeStruct((B, S, MLSTM_V_COLS), F32),
                   jax.ShapeDtypeStruct((B, H, MLSTM_DV, MLSTM_DQK), F32),
                   jax.ShapeDtypeStruct((B, H, 1, MLSTM_DQK), F32),
                   jax.ShapeDtypeStruct((B, H, 1, 1), F32)],
        scratch_shapes=[pltpu.VMEM((MLSTM_DV, MLSTM_DQK), F32), pltpu.VMEM((1, MLSTM_DQK), F32),
                        pltpu.VMEM((1, 1), F32)],
        compiler_params=pltpu.CompilerParams(dimension_semantics=("arbitrary",) * 3),
        name="mlstm_chunks",
    )(z, z, z, z, vt, gates[0], gates[1])
    return hs, c, n.reshape(B, H, MLSTM_DQK), m.reshape(B, H)


def _mlstm_step_body(q_ref, k_ref, v_ref, og_ref, ig_ref, fg_ref, c_ref, n_ref, m_ref,
                     h_ref, c_out, n_out, m_out):
    eye = (lax.broadcasted_iota(jnp.int32, (MLSTM_DV, MLSTM_DV), 0)
           == lax.broadcasted_iota(jnp.int32, (MLSTM_DV, MLSTM_DV), 1))
    for h in range(MLSTM_HEADS):
        c, n, m = c_ref[0, h], n_ref[0, h], m_ref[0, h]
        q = q_ref[0, h]
        k = k_ref[0, h] * (MLSTM_DQK ** -0.5)
        v = v_ref[0, h]
        i_g = ig_ref[0, h]
        inter = _log_sigmoid(fg_ref[0, h]) + m
        mt = jnp.maximum(inter, i_g)
        w = jnp.exp(i_g - mt)
        gq = jnp.exp(inter - mt)
        a = w * jnp.sum(q * k, axis=1, keepdims=True)
        cq_col = jnp.sum(c * q, axis=1, keepdims=True)
        cq_row = jnp.sum(jnp.where(eye, cq_col, 0.0), axis=0, keepdims=True)
        den = a + gq * jnp.sum(n * q, axis=1, keepdims=True)
        hc = (a * v + gq * cq_row) / jnp.maximum(jnp.abs(den), jnp.exp(-mt))
        h_ref[0, h] = hc * jax.nn.sigmoid(og_ref[0, h])
        v_col = jnp.sum(jnp.where(eye, v, 0.0), axis=1, keepdims=True)
        c_out[0, h] = gq * c + (w * v_col) * k
        n_out[0, h] = gq * n + w * k
        m_out[0, h] = mt


def _mlstm_sample(z, b_gate, c0, n0, m0):
    B, T, _ = z.shape
    assert T == 1
    H = MLSTM_HEADS
    z = z.reshape(B, -1)
    qk, hv = MLSTM_QK_COLS, MLSTM_V_COLS
    q = z[:, :qk].reshape(B, H, 1, MLSTM_DQK)
    k = z[:, qk:2 * qk].reshape(B, H, 1, MLSTM_DQK)
    v = z[:, 2 * qk:2 * qk + hv].reshape(B, H, 1, MLSTM_DV)
    og = z[:, 2 * qk + hv:2 * qk + 2 * hv].reshape(B, H, 1, MLSTM_DV)
    gates = z[:, 2 * qk + 2 * hv:] + b_gate
    ig = gates[:, :H].reshape(B, H, 1, 1)
    fg = gates[:, H:].reshape(B, H, 1, 1)
    spec = lambda r, w: pl.BlockSpec((1, H, r, w), lambda b: (b, 0, 0, 0))
    shapes = [(1, MLSTM_DV), (MLSTM_DV, MLSTM_DQK), (1, MLSTM_DQK), (1, 1)]
    hs, c, n, m = pl.pallas_call(
        _mlstm_step_body,
        grid=(B,),
        in_specs=[spec(1, MLSTM_DQK), spec(1, MLSTM_DQK), spec(1, MLSTM_DV), spec(1, MLSTM_DV),
                  spec(1, 1), spec(1, 1), spec(MLSTM_DV, MLSTM_DQK), spec(1, MLSTM_DQK), spec(1, 1)],
        out_specs=[spec(*s) for s in shapes],
        out_shape=[jax.ShapeDtypeStruct((B, H) + s, F32) for s in shapes],
        compiler_params=pltpu.CompilerParams(dimension_semantics=("arbitrary",)),
        name="mlstm_step",
    )(q, k, v, og, ig, fg, c0, n0.reshape(B, H, 1, MLSTM_DQK), m0.reshape(B, H, 1, 1))
    return hs.reshape(B, T, hv), c, n.reshape(B, H, MLSTM_DQK), m.reshape(B, H)


def _rmsnorm(x, g):
    return x * lax.rsqrt(jnp.mean(x * x, axis=-1, keepdims=True) + RMS_EPS) * g


def _rope(x, pos):
    half = x.shape[-1] // 2
    freqs = ROPE_THETA ** (-jnp.arange(half, dtype=F32) / half)
    ang = pos.astype(F32)[:, None] * freqs[None, :]
    cos = jnp.cos(ang)[None, :, None, :]
    sin = jnp.sin(ang)[None, :, None, :]
    x1, x2 = x[..., :half], x[..., half:]
    return jnp.concatenate([x1 * cos - x2 * sin, x1 * sin + x2 * cos], axis=-1)


def _alibi_slopes(n):
    return 2.0 ** (-8.0 * jnp.arange(1, n + 1, dtype=F32) / n)


def _to_blocks(a, axis):
    n = a.shape[axis] // Q_BLOCK
    a = a.reshape(a.shape[:axis] + (n, Q_BLOCK) + a.shape[axis + 1:])
    return jnp.moveaxis(a, axis, 0)


def _from_blocks(a, axis):
    a = jnp.moveaxis(a, 0, axis)
    return a.reshape(a.shape[:axis] + (-1,) + a.shape[axis + 2:])


def _attend(q, ks, vs, kposs, q_pos, scale, slopes=None, window=None):
    B, Tq, H, Dk = q.shape
    kvh = ks[0].shape[2]
    G = H // kvh
    qg = q.reshape(B, Tq, kvh, G, Dk)
    scores = []
    for k, kp in zip(ks, kposs):
        s = jnp.einsum('bqgnd,bkgd->bgnqk', qg, k).astype(F32) * scale
        dist = q_pos[:, None] - kp[None, :]
        mask = (dist >= 0) & (kp[None, :] >= 0)
        if window is not None:
            mask = mask & (dist < window)
        if slopes is not None:
            s = s - slopes.reshape(kvh, G)[None, :, :, None, None] * dist.astype(F32)
        scores.append(jnp.where(mask, s, NEG))
    p = jax.nn.softmax(jnp.concatenate(scores, axis=-1), axis=-1)
    out, off = 0.0, 0
    for v in vs:
        n = v.shape[1]
        out = out + jnp.einsum('bgnqk,bkgd->bqgnd', p[..., off:off + n], v)
        off += n
    return out.reshape(B, Tq, H, -1)


def _mla_prompt(a, g_q, g_kv, w_uq, w_uk, w_uv):
    B, S, _ = a.shape
    H = MLA_HEADS
    pos = jnp.arange(S, dtype=jnp.int32)
    a2 = a.reshape(B * S, -1)
    q = _norm_proj(a2[:, :MLA_Q_LORA], g_q, w_uq).reshape(B, S, H, MLA_NOPE_DIM + MLA_ROPE_DIM)
    w_kv = jnp.concatenate([w_uk.reshape(MLA_KV_LORA, -1), w_uv.reshape(MLA_KV_LORA, -1)], axis=1)
    kv, ckv = _norm_proj(a2[:, MLA_Q_LORA:MLA_Q_LORA + MLA_KV_LORA], g_kv, w_kv, with_normed=True)
    kpe = _rope(a[..., MLA_Q_LORA + MLA_KV_LORA:][:, :, None, :], pos)
    q_pe = _rope(q[..., MLA_NOPE_DIM:], pos)
    zpad = jnp.zeros((B, S, H, MLA_QK_PAD - MLA_NOPE_DIM - MLA_ROPE_DIM), F32)
    qc = jnp.concatenate([q[..., :MLA_NOPE_DIM] * MLA_SCALE, q_pe * MLA_SCALE, zpad], axis=-1)
    k_nope = kv[:, :H * MLA_NOPE_DIM].reshape(B, S, H, MLA_NOPE_DIM)
    kc = jnp.concatenate([k_nope, jnp.broadcast_to(kpe, (B, S, H, MLA_ROPE_DIM)), zpad], axis=-1)
    v = kv[:, H * MLA_NOPE_DIM:].reshape(B, S, H, MLA_V_DIM)
    to_heads = lambda t: t.transpose(0, 2, 1, 3).astype(BF16)
    o = _mla_flash(to_heads(qc), to_heads(kc), to_heads(v))
    new_rows = jnp.concatenate([ckv.reshape(B, S, -1), kpe[:, :, 0]], axis=-1)
    return o, new_rows


def _mla_sample(a, cache, j, page_table, g_q, g_kv, w_uq, w_uk, w_uv):
    B, T, _ = a.shape
    assert T == 1
    H = MLA_HEADS
    past_len = page_table.shape[1] * cache.shape[2]
    pos = past_len + jnp.arange(T, dtype=jnp.int32)
    a2 = a.reshape(B * T, -1)
    q = _norm_proj(a2[:, :MLA_Q_LORA], g_q, w_uq).reshape(B, T, H, MLA_NOPE_DIM + MLA_ROPE_DIM)
    w_kv = jnp.concatenate([w_uk.reshape(MLA_KV_LORA, -1), w_uv.reshape(MLA_KV_LORA, -1)], axis=1)
    _, ckv = _norm_proj(a2[:, MLA_Q_LORA:MLA_Q_LORA + MLA_KV_LORA], g_kv, w_kv, with_normed=True)
    kpe = _rope(a[..., MLA_Q_LORA + MLA_KV_LORA:][:, :, None, :], pos)
    q_pe = _rope(q[..., MLA_NOPE_DIM:], pos)
    new_rows = jnp.concatenate([ckv.reshape(B, T, -1), kpe[:, :, 0]], axis=-1)
    q_nope = q[:, 0, :, :MLA_NOPE_DIM].transpose(1, 0, 2)
    q_lat = _heads_matmul(q_nope, w_uk.transpose(1, 2, 0)).transpose(1, 0, 2)
    q_abs = jnp.concatenate([q_lat, q_pe[:, 0]], axis=-1) * MLA_SCALE
    o_lat = _mla_decode(q_abs, new_rows, cache, j, page_table)
    o = _heads_matmul(o_lat.transpose(1, 0, 2), w_uv.transpose(1, 0, 2))
    return o.transpose(1, 0, 2).reshape(B, T, -1), new_rows


def _mlstm_split(z, b_gate):
    B, T, _ = z.shape
    qk = MLSTM_HEADS * MLSTM_DQK
    hv = MLSTM_HEADS * MLSTM_DV
    q = z[..., :qk].reshape(B, T, MLSTM_HEADS, MLSTM_DQK)
    k = z[..., qk:2 * qk].reshape(B, T, MLSTM_HEADS, MLSTM_DQK) * (MLSTM_DQK ** -0.5)
    v = z[..., 2 * qk:2 * qk + hv].reshape(B, T, MLSTM_HEADS, MLSTM_DV)
    o = jax.nn.sigmoid(z[..., 2 * qk + hv:2 * qk + 2 * hv]).reshape(B, T, MLSTM_HEADS, MLSTM_DV)
    gt = z[..., 2 * qk + 2 * hv:] + b_gate
    ig = gt[..., :MLSTM_HEADS]
    lf = jax.nn.log_sigmoid(gt[..., MLSTM_HEADS:])
    return q, k, v, o, ig, lf


def _mlstm_chunkwise(q, k, v, ig, lf, c0, n0, m0):
    B, T, H, _ = q.shape
    L = MLSTM_CHUNK if T % MLSTM_CHUNK == 0 else T
    nc = T // L

    def split(a):
        a = a.reshape((B, nc, L) + a.shape[2:])
        return jnp.moveaxis(a, 1, 0)

    causal = jnp.tril(jnp.ones((L, L), dtype=bool))

    def step(carry, xs):
        c, n, m = carry
        qc, kc, vc, ic, fc = xs
        b = jnp.cumsum(fc, axis=1).transpose(0, 2, 1)
        it = ic.transpose(0, 2, 1)
        d = jnp.where(causal, b[:, :, :, None] - b[:, :, None, :] + it[:, :, None, :], NEG)
        inter = b + m[:, :, None]
        mt = jnp.maximum(inter, d.max(axis=-1))
        w = jnp.exp(d - mt[..., None])
        g = jnp.exp(inter - mt)
        a = w * jnp.einsum('bthd,bshd->bhts', qc, kc)
        num = jnp.einsum('bhts,bshe->bthe', a, vc) + jnp.einsum('bht,bhed,bthd->bthe', g, c, qc)
        den = a.sum(axis=-1) + g * jnp.einsum('bhd,bthd->bht', n, qc)
        hc = num / jnp.maximum(jnp.abs(den), jnp.exp(-mt)).transpose(0, 2, 1)[..., None]
        m_new = mt[:, :, -1]
        wl = jnp.exp(b[:, :, -1:] - b + it - m_new[:, :, None])
        gl = jnp.exp(b[:, :, -1] + m - m_new)
        c_new = gl[:, :, None, None] * c + jnp.einsum('bhs,bshe,bshd->bhed', wl, vc, kc)
        n_new = gl[:, :, None] * n + jnp.einsum('bhs,bshd->bhd', wl, kc)
        return (c_new, n_new, m_new), hc

    (c, n, m), hs = lax.scan(step, (c0, n0, m0), (split(q), split(k), split(v), split(ig), split(lf)))
    hs = jnp.moveaxis(hs, 0, 1).reshape(B, T, H, -1)
    return hs, c, n, m


def _mlstm_mix(z, c0, n0, m0, b_gate):
    B, T, _ = z.shape
    q, k, v, o, ig, lf = _mlstm_split(z, b_gate)
    hs, c, n, m = _mlstm_chunkwise(q, k, v, ig, lf, c0, n0, m0)
    return (hs * o).reshape(B, T, -1), c, n, m


def _nsa_split(z):
    B, T, _ = z.shape
    q = z[..., :NSA_Q_COLS].reshape(B, T, NSA_HEADS, NSA_HEAD_DIM)
    kv = z[..., NSA_Q_COLS:NSA_Q_COLS + 3 * NSA_KV_COLS].reshape(B, T, 3, NSA_KV_HEADS, 2, NSA_HEAD_DIM)
    g = jax.nn.sigmoid(z[..., NSA_Q_COLS + 3 * NSA_KV_COLS:]).reshape(B, T, NSA_HEADS, 3)
    return q, kv[:, :, 0], kv[:, :, 1], kv[:, :, 2], g


def _summarize(kv, w_cmp):
    B, T = kv.shape[:2]
    nb = T // CMP_BLOCK
    blk = kv[:, :nb * CMP_BLOCK].reshape(B, nb, CMP_BLOCK, NSA_KV_HEADS, 2, NSA_HEAD_DIM)
    return jnp.einsum('bnlgcd,clde->bngce', blk, w_cmp)


def _nsa_compressed(q, summ, q_pos, slopes):
    B, T, H, Dh = q.shape
    nbc = summ.shape[1]
    G = H // NSA_KV_HEADS
    qg = q.reshape(B, T, NSA_KV_HEADS, G, Dh)
    s = jnp.einsum('btgnd,bjgd->bgntj', qg, summ[..., 0, :]) * NSA_SCALE
    end = (jnp.arange(nbc, dtype=jnp.int32) + 1) * CMP_BLOCK - 1
    dist = q_pos[:, None] - end[None, :]
    mask = dist >= 0
    s = jnp.where(mask, s - slopes.reshape(NSA_KV_HEADS, G)[None, :, :, None, None] * dist.astype(F32), NEG)
    p = jnp.where(mask, jax.nn.softmax(s, axis=-1), 0.0)
    o = jnp.einsum('bgntj,bjgd->btgnd', p, summ[..., 1, :]).reshape(B, T, H, Dh)
    return o, p.sum(axis=2)


def _nsa_select(imp, q_pos, nb):
    nbc = imp.shape[-1]
    score = jnp.pad(imp, ((0, 0), (0, 0), (0, 0), (0, nb - nbc)))
    j = jnp.arange(nb, dtype=jnp.int32)[None, :]
    cur = (q_pos // SEL_BLOCK)[:, None]
    forced = (j == 0) | (j == cur) | (j == cur - 1)
    score = jnp.where(forced, FORCE, jnp.where(j <= cur, score, -FORCE))
    return lax.top_k(score, min(TOP_K_BLOCKS, nb))[1]


def _sel_attn(q, q_pos, kg, vg, idx, slopes):
    B, Tq, H, Dh = q.shape
    G = H // NSA_KV_HEADS
    qg = q.reshape(B, Tq, NSA_KV_HEADS, G, Dh)
    s = jnp.einsum('btgnd,bgtkld->bgntkl', qg, kg) * NSA_SCALE
    rows = idx[..., None] * SEL_BLOCK + jnp.arange(SEL_BLOCK, dtype=jnp.int32)
    dist = (q_pos[None, None, :, None, None] - rows)[:, :, None]
    sl = slopes.reshape(NSA_KV_HEADS, G)[None, :, :, None, None, None]
    s = jnp.where(dist >= 0, s - sl * dist.astype(F32), NEG)
    shp = s.shape
    p = jax.nn.softmax(s.reshape(shp[:4] + (-1,)), axis=-1).reshape(shp)
    o = jnp.einsum('bgntkl,bgtkld->btgnd', p, vg)
    return o.reshape(B, Tq, H, Dh)


def _nsa_merge(g, o_cmp, o_sel, o_win):
    o = g[..., 0:1] * o_cmp + g[..., 1:2] * o_sel + g[..., 2:3] * o_win
    B, T = o.shape[:2]
    return o.reshape(B, T, -1)


def _nsa_seq_layout(kv):
    return kv.transpose(3, 0, 2, 1, 4).astype(BF16)


def _nsa_prompt(z, w_bd):
    B, S, _ = z.shape
    nb = S // CMP_BLOCK
    q = z[..., :NSA_Q_COLS].reshape(B, S, NSA_HEADS, NSA_HEAD_DIM).transpose(0, 2, 1, 3).astype(BF16)
    kv = z[..., NSA_Q_COLS:NSA_Q_COLS + 3 * NSA_KV_COLS].reshape(B, S, 3, NSA_KV_HEADS, 2, NSA_HEAD_DIM)
    kv_c, kv_s, kv_w = kv[:, :, 0], kv[:, :, 1], kv[:, :, 2]
    summ = _summarize_blocks(kv_c.reshape(B * nb, CMP_BLOCK * NSA_KV_COLS), w_bd)
    summ = summ.reshape(B, nb, NSA_KV_HEADS, 2, NSA_HEAD_DIM).transpose(3, 0, 2, 1, 4).astype(BF16)
    gate = z[..., NSA_Q_COLS + 3 * NSA_KV_COLS:].reshape(B, S, NSA_KV_HEADS, 3 * NSA_GROUP).transpose(0, 2, 1, 3)
    y = _nsa_prompt_attend(q, summ, _nsa_seq_layout(kv_s), _nsa_seq_layout(kv_w), gate)
    return y, kv_c, kv_s, kv_w[:, -min(WINDOW, S):]


def _nsa_sample(z, cache_cmp, cache_sel, win_state, j, page_table, w_paged):
    B, T, _ = z.shape
    assert T == 1 and T < CMP_BLOCK
    G, Dh = NSA_KV_HEADS, NSA_HEAD_DIM
    assert cache_cmp.shape[2] == PAGE_SIZE
    n_pages = page_table.shape[1]
    q, kv_c, kv_s, kv_w, gate = _nsa_split(z)
    summ = _summarize_pool(cache_cmp[j], w_paged)[page_table]
    summ = summ.transpose(3, 0, 2, 1, 4, 5).reshape(2, B, G, n_pages * SUMM_PAGE_BLOCKS, Dh).astype(BF16)
    qg = q.reshape(B, G, NSA_GROUP, Dh) * NSA_SCALE
    o_cmp, idx = _nsa_cmp_sample(qg, summ)
    o_sel = _nsa_sel_sample(qg, kv_s.reshape(B, G, 2, Dh), cache_sel, j, page_table, idx)
    o_win, new_win = _nsa_win_sample(qg, kv_w.reshape(B, G, 2, Dh), win_state)
    heads = lambda o: o.reshape(B, T, NSA_HEADS, Dh)
    y = _nsa_merge(gate, heads(o_cmp), heads(o_sel), heads(o_win))
    new_win = new_win.reshape(B, G, 2, Dh, new_win.shape[-1]).transpose(0, 4, 1, 2, 3)
    return y, kv_c, kv_s, new_win


def kernel(x_prompt, x_sample, cache_mla_kv, state_mlstm_c, state_mlstm_n, state_mlstm_m, cache_nsa_cmp,
           cache_nsa_sel, state_nsa_win, page_table, norm_g, final_norm_g, mla_w_a, mla_g_q, mla_g_kv,
           mla_w_uq, mla_w_uk, mla_w_uv, mla_w_o, mlstm_w_in, mlstm_b_gate, mlstm_w_out, nsa_w_in,
           nsa_w_cmp, nsa_w_out, mlp_w1, mlp_w2):
    B, S, D = x_prompt.shape
    Bs, Ts, _ = x_sample.shape
    xp = x_prompt.reshape(B * S, D)
    xs = x_sample.reshape(Bs * Ts, D)
    mla_p, mla_s = [], []
    mc_p, mn_p, mm_p, mc_s, mn_s, mm_s = [], [], [], [], [], []
    cmp_p, cmp_s, sel_p, sel_s, win_p, win_s = [], [], [], [], [], []
    for i in range(DEPTH):
        j = i // N_MIXERS
        g0 = norm_g[i, 0]
        if i % N_MIXERS == 0:
            ap = _norm_proj(xp, g0, mla_w_a[j]).reshape(B, S, -1)
            as_ = _norm_proj(xs, g0, mla_w_a[j]).reshape(Bs, Ts, -1)
            w = (mla_g_q[j], mla_g_kv[j], mla_w_uq[j], mla_w_uk[j], mla_w_uv[j])
            op, rp = _mla_prompt(ap, *w)
            os_, rs = _mla_sample(as_, cache_mla_kv, j, page_table, *w)
            mla_p.append(rp)
            mla_s.append(rs)
            w_out = mla_w_o[j]
        elif i % N_MIXERS == 1:
            zp = _norm_proj(xp, g0, mlstm_w_in[j]).reshape(B, S, -1)
            zs = _norm_proj(xs, g0, mlstm_w_in[j]).reshape(Bs, Ts, -1)
            op, cp, nst_p, mp = _mlstm_prompt(zp, mlstm_b_gate[j])
            os_, cs, nst_s, ms = _mlstm_sample(zs, mlstm_b_gate[j], state_mlstm_c[j], state_mlstm_n[j],
                                               state_mlstm_m[j])
            mc_p.append(cp)
            mn_p.append(nst_p)
            mm_p.append(mp)
            mc_s.append(cs)
            mn_s.append(nst_s)
            mm_s.append(ms)
            w_out = mlstm_w_out[j]
        else:
            zp = _norm_proj(xp, g0, nsa_w_in[j]).reshape(B, S, -1)
            zs = _norm_proj(xs, g0, nsa_w_in[j]).reshape(Bs, Ts, -1)
            op, kcp, ksp, kwp = _nsa_prompt(zp, _summ_weights(nsa_w_cmp[j]))
            os_, kcs, kss, kws = _nsa_sample(zs, cache_nsa_cmp, cache_nsa_sel, state_nsa_win[j], j,
                                             page_table, _summ_weights_paged(nsa_w_cmp[j]))
            cmp_p.append(kcp)
            cmp_s.append(kcs)
            sel_p.append(ksp)
            sel_s.append(kss)
            win_p.append(kwp)
            win_s.append(kws)
            w_out = nsa_w_out[j]
        xp = _proj_res(op.reshape(B * S, -1), w_out, xp)
        xs = _proj_res(os_.reshape(Bs * Ts, -1), w_out, xs)
        xp = _mlp_res(xp, norm_g[i, 1], mlp_w1[i], mlp_w2[i])
        xs = _mlp_res(xs, norm_g[i, 1], mlp_w1[i], mlp_w2[i])
    y_prompt = _final_norm(xp, final_norm_g).reshape(B, S, D)
    y_sample = _final_norm(xs, final_norm_g).reshape(Bs, Ts, D)
    return (y_prompt, y_sample,
            jnp.stack(mla_p), jnp.stack(mla_s),
            jnp.stack(mc_p), jnp.stack(mn_p), jnp.stack(mm_p),
            jnp.stack(mc_s), jnp.stack(mn_s), jnp.stack(mm_s),
            jnp.stack(cmp_p), jnp.stack(cmp_s),
            jnp.stack(sel_p), jnp.stack(sel_s),
            jnp.stack(win_p), jnp.stack(win_s))
```

```python
import functools

import jax
import jax.numpy as jnp
from jax import lax
from jax.experimental import pallas as pl
from jax.experimental.pallas import tpu as pltpu

F32 = jnp.float32
BF16 = jnp.bfloat16

D_MODEL = 1024
DEPTH = 4
N_MIXERS = 3
PAGE_SIZE = 128

MLA_HEADS = 16
MLA_NOPE_DIM = 64
MLA_ROPE_DIM = 32
MLA_V_DIM = 64
MLA_Q_LORA = 384
MLA_KV_LORA = 256
MLA_SCALE = (MLA_NOPE_DIM + MLA_ROPE_DIM) ** -0.5
ROPE_THETA = 10000.0

MLSTM_HEADS = 4
MLSTM_DQK = 128
MLSTM_DV = 256
MLSTM_CHUNK = 64

NSA_HEADS = 16
NSA_KV_HEADS = 4
NSA_HEAD_DIM = 64
CMP_BLOCK = 64
SEL_BLOCK = 64
TOP_K_BLOCKS = 16
WINDOW = 512
NSA_Q_COLS = NSA_HEADS * NSA_HEAD_DIM
NSA_KV_COLS = NSA_KV_HEADS * 2 * NSA_HEAD_DIM
NSA_SCALE = NSA_HEAD_DIM ** -0.5

D_FF = 4 * D_MODEL
Q_BLOCK = 128
RMS_EPS = 1e-6
NEG = -1e30
FORCE = 1e4

VMEM_LIMIT_BYTES = 56 * 1024 * 1024
FF_CHUNK = 512


def _row_tile(m):
    for t in (512, 256, 128):
        if m % t == 0:
            return t
    return m


def _rms_rows(x, g):
    return x * lax.rsqrt(jnp.mean(x * x, axis=-1, keepdims=True) + RMS_EPS) * g


def _norm_proj_body(x_ref, g_ref, w_ref, o_ref, *h_ref):
    h = _rms_rows(x_ref[...], g_ref[...])
    o_ref[...] = jnp.dot(h.astype(BF16), w_ref[...], preferred_element_type=F32)
    if h_ref:
        h_ref[0][...] = h


def _norm_proj(x, g, w, with_normed=False):
    m, d = x.shape
    n = w.shape[1]
    tm = _row_tile(m)
    out_specs = [pl.BlockSpec((tm, n), lambda i: (i, 0))]
    out_shape = [jax.ShapeDtypeStruct((m, n), F32)]
    if with_normed:
        out_specs.append(pl.BlockSpec((tm, d), lambda i: (i, 0)))
        out_shape.append(jax.ShapeDtypeStruct((m, d), F32))
    out = pl.pallas_call(
        _norm_proj_body,
        grid=(m // tm,),
        in_specs=[pl.BlockSpec((tm, d), lambda i: (i, 0)),
                  pl.BlockSpec((1, d), lambda i: (0, 0)),
                  pl.BlockSpec((d, n), lambda i: (0, 0))],
        out_specs=out_specs,
        out_shape=out_shape,
        compiler_params=pltpu.CompilerParams(dimension_semantics=("arbitrary",),
                                             vmem_limit_bytes=VMEM_LIMIT_BYTES),
        name="norm_proj",
    )(x, g.reshape(1, d), w.astype(BF16))
    return out if with_normed else out[0]


def _proj_res_body(a_ref, w_ref, r_ref, o_ref):
    o_ref[...] = r_ref[...] + jnp.dot(a_ref[...].astype(BF16), w_ref[...], preferred_element_type=F32)


def _proj_res(a, w, res):
    m, k = a.shape
    d = w.shape[1]
    tm = _row_tile(m)
    return pl.pallas_call(
        _proj_res_body,
        grid=(m // tm,),
        in_specs=[pl.BlockSpec((tm, k), lambda i: (i, 0)),
                  pl.BlockSpec((k, d), lambda i: (0, 0)),
                  pl.BlockSpec((tm, d), lambda i: (i, 0))],
        out_specs=pl.BlockSpec((tm, d), lambda i: (i, 0)),
        out_shape=jax.ShapeDtypeStruct((m, d), F32),
        compiler_params=pltpu.CompilerParams(dimension_semantics=("arbitrary",),
                                             vmem_limit_bytes=VMEM_LIMIT_BYTES),
        name="proj_res",
    )(a, w.astype(BF16), res)


def _mlp_body(x_ref, g_ref, w1_ref, w2_ref, o_ref):
    x = x_ref[...]
    h = _rms_rows(x, g_ref[...]).astype(BF16)
    acc = x
    for c in range(D_FF // FF_CHUNK):
        a = jnp.dot(h, w1_ref[:, c * FF_CHUNK:(c + 1) * FF_CHUNK], preferred_element_type=F32)
        a = jnp.maximum(a, 0.0)
        acc = acc + jnp.dot((a * a).astype(BF16), w2_ref[c * FF_CHUNK:(c + 1) * FF_CHUNK, :],
                            preferred_element_type=F32)
    o_ref[...] = acc


def _mlp_res(x, g, w1, w2):
    m, d = x.shape
    tm = _row_tile(m)
    return pl.pallas_call(
        _mlp_body,
        grid=(m // tm,),
        in_specs=[pl.BlockSpec((tm, d), lambda i: (i, 0)),
                  pl.BlockSpec((1, d), lambda i: (0, 0)),
                  pl.BlockSpec((d, D_FF), lambda i: (0, 0)),
                  pl.BlockSpec((D_FF, d), lambda i: (0, 0))],
        out_specs=pl.BlockSpec((tm, d), lambda i: (i, 0)),
        out_shape=jax.ShapeDtypeStruct((m, d), F32),
        compiler_params=pltpu.CompilerParams(dimension_semantics=("arbitrary",),
                                             vmem_limit_bytes=VMEM_LIMIT_BYTES),
        name="mlp_res",
    )(x, g.reshape(1, d), w1.astype(BF16), w2.astype(BF16))


def _final_norm_body(x_ref, g_ref, o_ref):
    o_ref[...] = _rms_rows(x_ref[...], g_ref[...])


def _final_norm(x, g):
    m, d = x.shape
    tm = _row_tile(m)
    return pl.pallas_call(
        _final_norm_body,
        grid=(m // tm,),
        in_specs=[pl.BlockSpec((tm, d), lambda i: (i, 0)), pl.BlockSpec((1, d), lambda i: (0, 0))],
        out_specs=pl.BlockSpec((tm, d), lambda i: (i, 0)),
        out_shape=jax.ShapeDtypeStruct((m, d), F32),
        compiler_params=pltpu.CompilerParams(dimension_semantics=("arbitrary",)),
        name="final_norm",
    )(x, g.reshape(1, d))


def _dot_nt(a, b):
    return lax.dot_general(a, b, (((1,), (1,)), ((), ())), preferred_element_type=F32)


SUMM_L_PER_STEP = 8
SUMM_COLS = NSA_KV_COLS
SUMM_HALF = SUMM_COLS // 2


def _summ_weights(w_cmp):
    wk = jnp.stack([w_cmp[0], w_cmp[1], w_cmp[0], w_cmp[1]], axis=0)
    bd = jnp.einsum('kj,klde->lkdje', jnp.eye(4, dtype=F32), wk)
    return bd.reshape(CMP_BLOCK, SUMM_HALF, SUMM_HALF).astype(BF16)


def _summarize_body(x_ref, w_ref, o_ref):
    @pl.when(pl.program_id(1) == 0)
    def _():
        o_ref[...] = jnp.zeros_like(o_ref)

    lo = o_ref[:, :SUMM_HALF]
    hi = o_ref[:, SUMM_HALF:]
    for li in range(SUMM_L_PER_STEP):
        x = x_ref[:, li * SUMM_COLS:(li + 1) * SUMM_COLS].astype(BF16)
        w = w_ref[li]
        lo = lo + jnp.dot(x[:, :SUMM_HALF], w, preferred_element_type=F32)
        hi = hi + jnp.dot(x[:, SUMM_HALF:], w, preferred_element_type=F32)
    o_ref[:, :SUMM_HALF] = lo
    o_ref[:, SUMM_HALF:] = hi


def _summarize_blocks(x2d, w_bd):
    nb = x2d.shape[0]
    p = 512 if nb % 512 == 0 else nb
    step_cols = SUMM_L_PER_STEP * SUMM_COLS
    return pl.pallas_call(
        _summarize_body,
        grid=(nb // p, CMP_BLOCK // SUMM_L_PER_STEP),
        in_specs=[pl.BlockSpec((p, step_cols), lambda i, l: (i, l)),
                  pl.BlockSpec((SUMM_L_PER_STEP, SUMM_HALF, SUMM_HALF), lambda i, l: (l, 0, 0))],
        out_specs=pl.BlockSpec((p, SUMM_COLS), lambda i, l: (i, 0)),
        out_shape=jax.ShapeDtypeStruct((nb, SUMM_COLS), F32),
        compiler_params=pltpu.CompilerParams(dimension_semantics=("arbitrary", "arbitrary"),
                                             vmem_limit_bytes=VMEM_LIMIT_BYTES),
        name="nsa_summarize",
    )(x2d, w_bd)


SUMM_D_PER_STEP = 8
SUMM_PAGE_BLOCKS = PAGE_SIZE // CMP_BLOCK
SUMM_PAGE_OUT = 2 * SUMM_PAGE_BLOCKS * NSA_HEAD_DIM


def _summ_weights_paged(w_cmp):
    n = 2 * SUMM_PAGE_BLOCKS
    wk = jnp.stack([w_cmp[c] for c in range(2) for _ in range(SUMM_PAGE_BLOCKS)], axis=0)
    bd = jnp.einsum('kj,klde->dklje', jnp.eye(n, dtype=F32), wk)
    return bd.reshape(NSA_HEAD_DIM, 2 * PAGE_SIZE, SUMM_PAGE_OUT).astype(BF16)


def _summarize_pool_body(k_ref, v_ref, w_ref, o_ref):
    dg = pl.program_id(2)

    @pl.when(dg == 0)
    def _():
        o_ref[...] = jnp.zeros_like(o_ref)

    acc = o_ref[...]
    for dd in range(SUMM_D_PER_STEP):
        lhs = jnp.concatenate([k_ref[:, dd, :], v_ref[:, dd, :]], axis=1).astype(BF16)
        acc = acc + jnp.dot(lhs, w_ref[dg * SUMM_D_PER_STEP + dd], preferred_element_type=F32)
    o_ref[...] = acc


def _summarize_pool(cache, w_paged):
    pool = cache.shape[0]
    cm = _nsa_channel_major(cache)
    p = 512 if pool % 512 == 0 else pool
    d_steps = NSA_HEAD_DIM // SUMM_D_PER_STEP
    rows_per_group = NSA_KV_PAIR // SUMM_D_PER_STEP
    out = pl.pallas_call(
        _summarize_pool_body,
        grid=(pool // p, NSA_KV_HEADS, d_steps),
        in_specs=[pl.BlockSpec((p, SUMM_D_PER_STEP, PAGE_SIZE), lambda i, g, d: (i, g * rows_per_group + d, 0)),
                  pl.BlockSpec((p, SUMM_D_PER_STEP, PAGE_SIZE),
                               lambda i, g, d: (i, g * rows_per_group + d_steps + d, 0)),
                  pl.BlockSpec((NSA_HEAD_DIM, 2 * PAGE_SIZE, SUMM_PAGE_OUT), lambda i, g, d: (0, 0, 0))],
        out_specs=pl.BlockSpec((p, SUMM_PAGE_OUT), lambda i, g, d: (i, g)),
        out_shape=jax.ShapeDtypeStruct((pool, NSA_KV_HEADS * SUMM_PAGE_OUT), F32),
        compiler_params=pltpu.CompilerParams(dimension_semantics=("arbitrary",) * 3,
                                             vmem_limit_bytes=VMEM_LIMIT_BYTES),
        name="nsa_summarize_pool",
    )(cm, cm, w_paged)
    return out.reshape(pool, NSA_KV_HEADS, 2, SUMM_PAGE_BLOCKS, NSA_HEAD_DIM)


NSA_TQ = 128
NSA_TK = 512
NSA_GROUP = NSA_HEADS // NSA_KV_HEADS
NSA_BAND = WINDOW + NSA_TQ


def _softmax_rows(s, valid):
    m = jnp.max(s, axis=1, keepdims=True)
    e = jnp.where(valid, jnp.exp(s - m), 0.0)
    l = jnp.sum(e, axis=1, keepdims=True)
    return e / jnp.where(l > 0.0, l, 1.0)


LOG2E = 1.4426950408889634
MASK_BIG = 1e30
M_INIT = -0.5e30


def _nsa_prompt_body(q_ref, ksum_ref, vsum_ref, ks_ref, vs_ref, kw_ref, vw_ref, gate_ref, o_ref):
    g = pl.program_id(1)
    q0 = pl.program_id(2) * NSA_TQ
    rows = NSA_GROUP * NSA_TQ
    nblk = ksum_ref.shape[2]
    dh = NSA_HEAD_DIM
    q = q_ref[0].reshape(rows, dh)

    row = lax.broadcasted_iota(jnp.int32, (rows, 1), 0)
    tok_in_tile = row & (NSA_TQ - 1)
    qpos = q0 + tok_in_tile
    head = g * NSA_GROUP + (row >> (NSA_TQ.bit_length() - 1))
    slope = jnp.exp((head + 1).astype(F32) * (-8.0 / NSA_HEADS * 0.6931471805599453)) * LOG2E

    s = _dot_nt(q, ksum_ref[0, 0])
    blk_end = (lax.broadcasted_iota(jnp.int32, (1, nblk), 1) + 1) * CMP_BLOCK - 1
    dist = qpos - blk_end
    vis = dist >= 0
    s = jnp.where(vis, s - slope * dist.astype(F32), NEG)
    e = jnp.where(vis, jnp.exp2(s - jnp.max(s, axis=1, keepdims=True)), 0.0)
    l = jnp.sum(e, axis=1, keepdims=True)
    p = e / jnp.where(l > 0.0, l, 1.0)
    o_cmp = jnp.dot(p.astype(BF16), vsum_ref[0, 0], preferred_element_type=F32)
    imp = p[0:NSA_TQ]
    for h in range(1, NSA_GROUP):
        imp = imp + p[h * NSA_TQ:(h + 1) * NSA_TQ]

    tok = q0 + lax.broadcasted_iota(jnp.int32, (NSA_TQ, 1), 0)
    cur = tok >> 6
    jj = lax.broadcasted_iota(jnp.int32, (NSA_TQ, nblk), 1)
    forced = jnp.where(jj == 0, 1, jnp.where(jj == cur, 1, jnp.where(jj == cur - 1, 1, 0)))
    score = jnp.where(forced > 0, FORCE, jnp.where(jj <= cur, imp, -FORCE))
    rank = jnp.zeros((NSA_TQ, nblk), jnp.int32)
    for i in range(nblk):
        ci = score[:, i:i + 1]
        rank = rank + jnp.where(ci > score, 1, jnp.where(ci == score, jnp.where(jj > i, 1, 0), 0))
    unpicked = jnp.where(rank < min(TOP_K_BLOCKS, nblk), 0.0, -1.0)
    if nblk < dh:
        unpicked = jnp.concatenate([unpicked, jnp.zeros((NSA_TQ, dh - nblk), F32)], axis=1)
    qs = jnp.concatenate([q, jnp.concatenate([unpicked.astype(BF16)] * NSA_GROUP, axis=0)], axis=1)

    def sel_tile(kt, carry, diagonal):
        m, acc = carry
        k0 = pl.multiple_of(kt * NSA_TK, NSA_TK)
        rel = k0 - q0 + lax.broadcasted_iota(jnp.int32, (1, NSA_TK), 1)
        s = _dot_nt(qs, ks_ref[0, 0, pl.ds(k0, NSA_TK), :]) + slope * rel.astype(F32)
        if diagonal:
            s = jnp.where(rel <= tok_in_tile, s, NEG)
        m_new = jnp.maximum(m, jnp.max(s, axis=1, keepdims=True))
        e = jnp.exp2(s - m_new).astype(BF16)
        acc = jnp.exp2(m - m_new) * acc + jnp.dot(e, vs_ref[0, 0, pl.ds(k0, NSA_TK), :],
                                                  preferred_element_type=F32)
        return m_new, acc

    n_kt = (q0 + NSA_TQ + NSA_TK - 1) // NSA_TK
    init = (jnp.full((rows, 1), M_INIT, F32), jnp.zeros((rows, 2 * dh), F32))
    carry = lax.fori_loop(0, n_kt - 1, functools.partial(sel_tile, diagonal=False), init)
    _, acc = sel_tile(n_kt - 1, carry, True)
    l = acc[:, dh:dh + 1]
    o_sel = acc[:, :dh] / jnp.where(l > 0.0, l, 1.0)

    w0 = pl.multiple_of(jnp.maximum(q0 - WINDOW, 0), NSA_TQ)
    dist = qpos - (w0 + lax.broadcasted_iota(jnp.int32, (1, NSA_BAND), 1))
    valid = jnp.where(dist >= 0, jnp.where(dist < WINDOW, 1, 0), 0) > 0
    s = jnp.where(valid, _dot_nt(q, kw_ref[0, 0, pl.ds(w0, NSA_BAND), :]) - slope * dist.astype(F32), NEG)
    e = jnp.exp2(s - jnp.max(s, axis=1, keepdims=True))
    o_win = (jnp.dot(e.astype(BF16), vw_ref[0, 0, pl.ds(w0, NSA_BAND), :], preferred_element_type=F32)
             / jnp.sum(e, axis=1, keepdims=True))

    gate = jax.nn.sigmoid(gate_ref[0, 0])
    outs = []
    for h in range(NSA_GROUP):
        r = slice(h * NSA_TQ, (h + 1) * NSA_TQ)
        outs.append(gate[:, 3 * h:3 * h + 1] * o_cmp[r] + gate[:, 3 * h + 1:3 * h + 2] * o_sel[r]
                    + gate[:, 3 * h + 2:3 * h + 3] * o_win[r])
    o_ref[0] = jnp.concatenate(outs, axis=-1)


def _nsa_prompt_attend(q, summ, kv_s, kv_w, gate):
    b, h, s, dh = q.shape
    g = NSA_KV_HEADS
    nblk = summ.shape[3]
    assert s % NSA_TK == 0 and s >= NSA_BAND and nblk <= dh
    blk = jnp.arange(s, dtype=jnp.int32)[:, None] // SEL_BLOCK
    onehot = jnp.where(blk == jnp.arange(dh, dtype=jnp.int32)[None, :], MASK_BIG, 0.0).astype(BF16)
    ks = jnp.concatenate([kv_s[0], jnp.broadcast_to(onehot, (b, g, s, dh))], axis=-1)
    ones = jnp.zeros((s, dh), BF16).at[:, 0].set(1.0)
    vs = jnp.concatenate([kv_s[1], jnp.broadcast_to(ones, (b, g, s, dh))], axis=-1)
    seq_spec = lambda w: pl.BlockSpec((1, 1, s, w), lambda bi, gi, qi: (bi, gi, 0, 0))
    sum_spec = pl.BlockSpec((1, 1, nblk, dh), lambda bi, gi, qi: (bi, gi, 0, 0))
    return pl.pallas_call(
        _nsa_prompt_body,
        grid=(b, g, s // NSA_TQ),
        in_specs=[pl.BlockSpec((1, NSA_GROUP, NSA_TQ, dh), lambda bi, gi, qi: (bi, gi, qi, 0)),
                  sum_spec, sum_spec, seq_spec(2 * dh), seq_spec(2 * dh), seq_spec(dh), seq_spec(dh),
                  pl.BlockSpec((1, 1, NSA_TQ, 3 * NSA_GROUP), lambda bi, gi, qi: (bi, gi, qi, 0))],
        out_specs=pl.BlockSpec((1, NSA_TQ, NSA_GROUP * dh), lambda bi, gi, qi: (bi, qi, gi)),
        out_shape=jax.ShapeDtypeStruct((b, s, h * dh), F32),
        compiler_params=pltpu.CompilerParams(dimension_semantics=("arbitrary",) * 3,
                                             vmem_limit_bytes=VMEM_LIMIT_BYTES),
        name="nsa_prompt_attend",
    )(q, summ[0], summ[1], ks, vs, kv_w[0], kv_w[1], gate)


def _heads_matmul_body(x_ref, w_ref, o_ref):
    o_ref[0] = jnp.dot(x_ref[0].astype(BF16), w_ref[0], preferred_element_type=F32)


def _heads_matmul(x, w):
    h, m, k = x.shape
    n = w.shape[2]
    return pl.pallas_call(
        _heads_matmul_body,
        grid=(h,),
        in_specs=[pl.BlockSpec((1, m, k), lambda i: (i, 0, 0)), pl.BlockSpec((1, k, n), lambda i: (i, 0, 0))],
        out_specs=pl.BlockSpec((1, m, n), lambda i: (i, 0, 0)),
        out_shape=jax.ShapeDtypeStruct((h, m, n), F32),
        compiler_params=pltpu.CompilerParams(dimension_semantics=("arbitrary",)),
        name="heads_matmul",
    )(x, w.astype(BF16))


MLA_TILE = 512
MLA_QK_PAD = 128
MLA_HEAD_PAIR = 2


def _mla_flash_body(q_ref, k_ref, v_ref, o_ref):
    qi = pl.program_id(2)
    t = MLA_TILE
    causal = lax.broadcasted_iota(jnp.int32, (t, t), 0) >= lax.broadcasted_iota(jnp.int32, (t, t), 1)
    qs = [q_ref[0, hh] for hh in range(MLA_HEAD_PAIR)]

    def step(kt, carry, diagonal):
        k0 = pl.multiple_of(kt * t, t)
        out = []
        for hh in range(MLA_HEAD_PAIR):
            m, acc = carry[hh]
            s = _dot_nt(qs[hh], k_ref[0, hh, pl.ds(k0, t), :])
            if diagonal:
                s = jnp.where(causal, s, NEG)
            m_new = jnp.maximum(m, jnp.max(s, axis=1, keepdims=True))
            e = jnp.exp2(s - m_new).astype(BF16)
            acc = jnp.exp2(m - m_new) * acc + jnp.dot(e, v_ref[0, hh, pl.ds(k0, t), :],
                                                      preferred_element_type=F32)
            out.append((m_new, acc))
        return tuple(out)

    init = tuple((jnp.full((t, 1), M_INIT, F32), jnp.zeros((t, MLA_QK_PAD), F32)) for _ in range(MLA_HEAD_PAIR))
    carry = lax.fori_loop(0, qi, functools.partial(step, diagonal=False), init)
    carry = step(qi, carry, True)
    o_ref[0] = jnp.concatenate([acc[:, :MLA_V_DIM] / acc[:, MLA_V_DIM:MLA_V_DIM + 1] for _, acc in carry],
                               axis=-1)


def _mla_flash(q, k, v):
    b, h, s, _ = q.shape
    t = MLA_TILE
    assert s % t == 0
    kv_map = lambda bi, hi, qi: (bi, hi, 0, 0)
    return pl.pallas_call(
        _mla_flash_body,
        grid=(b, h // MLA_HEAD_PAIR, s // t),
        in_specs=[pl.BlockSpec((1, MLA_HEAD_PAIR, t, MLA_QK_PAD), lambda bi, hi, qi: (bi, hi, qi, 0)),
                  pl.BlockSpec((1, MLA_HEAD_PAIR, s, MLA_QK_PAD), kv_map),
                  pl.BlockSpec((1, MLA_HEAD_PAIR, s, MLA_QK_PAD), kv_map)],
        out_specs=pl.BlockSpec((1, t, MLA_HEAD_PAIR * MLA_V_DIM), lambda bi, hi, qi: (bi, qi, hi)),
        out_shape=jax.ShapeDtypeStruct((b, s, h * MLA_V_DIM), F32),
        compiler_params=pltpu.CompilerParams(dimension_semantics=("arbitrary",) * 3,
                                             vmem_limit_bytes=VMEM_LIMIT_BYTES),
        name="mla_flash",
    )(q, k, v)


MLA_PAGES_PER_STEP = 32
MLA_ROW = MLA_KV_LORA + MLA_ROPE_DIM


def _mla_decode_body(pt_ref, q_ref, new_ref, *refs):
    del pt_ref
    npg = MLA_PAGES_PER_STEP
    pages, o_ref = refs[:npg], refs[npg]
    m_sc, l_sc, acc_sc = refs[npg + 1:]
    step = pl.program_id(1)

    @pl.when(step == 0)
    def _():
        m_sc[...] = jnp.full_like(m_sc, NEG)
        l_sc[...] = jnp.zeros_like(l_sc)
        acc_sc[...] = jnp.zeros_like(acc_sc)

    qf = q_ref[0]
    q = qf.astype(BF16)
    kt = jnp.concatenate([pages[p][0, 0].astype(BF16) for p in range(npg)], axis=1)
    s = jnp.dot(q, kt, preferred_element_type=F32)
    m = m_sc[...]
    m_new = jnp.maximum(m, jnp.max(s, axis=1, keepdims=True))
    alpha = jnp.exp(m - m_new)
    e = jnp.exp(s - m_new)
    l_new = alpha * l_sc[...] + jnp.sum(e, axis=1, keepdims=True)
    acc_new = alpha * acc_sc[...] + _dot_nt(e.astype(BF16), kt[:MLA_KV_LORA])
    m_sc[...] = m_new
    l_sc[...] = l_new
    acc_sc[...] = acc_new

    @pl.when(step == pl.num_programs(1) - 1)
    def _():
        new = new_ref[0]
        s_new = jnp.sum(qf * new, axis=1, keepdims=True)
        m_fin = jnp.maximum(m_new, s_new)
        a = jnp.exp(m_new - m_fin)
        e_new = jnp.exp(s_new - m_fin)
        o_ref[0] = (a * acc_new + e_new * new[:, :MLA_KV_LORA]) / (a * l_new + e_new)


def _mla_decode(q, new_rows, cache, layer, page_table):
    b, h, w = q.shape
    n_pages = page_table.shape[1]
    npg = MLA_PAGES_PER_STEP
    assert n_pages % npg == 0 and cache.shape[2] == PAGE_SIZE
    cache = cache.transpose(0, 1, 3, 2)

    def page_spec(p):
        return pl.BlockSpec((1, 1, w, PAGE_SIZE), lambda bi, si, pt: (layer, pt[bi, si * npg + p], 0, 0))

    grid_spec = pltpu.PrefetchScalarGridSpec(
        num_scalar_prefetch=1,
        grid=(b, n_pages // npg),
        in_specs=[pl.BlockSpec((1, h, w), lambda bi, si, pt: (bi, 0, 0)),
                  pl.BlockSpec((1, 1, w), lambda bi, si, pt: (bi, 0, 0))] + [page_spec(p) for p in range(npg)],
        out_specs=pl.BlockSpec((1, h, MLA_KV_LORA), lambda bi, si, pt: (bi, 0, 0)),
        scratch_shapes=[pltpu.VMEM((h, 1), F32), pltpu.VMEM((h, 1), F32), pltpu.VMEM((h, MLA_KV_LORA), F32)],
    )
    return pl.pallas_call(
        _mla_decode_body,
        grid_spec=grid_spec,
        out_shape=jax.ShapeDtypeStruct((b, h, MLA_KV_LORA), F32),
        compiler_params=pltpu.CompilerParams(dimension_semantics=("arbitrary", "arbitrary"),
                                             vmem_limit_bytes=VMEM_LIMIT_BYTES),
        name="mla_decode",
    )(page_table, q, new_rows, *([cache] * npg))


NSA_KV_PAIR = 2 * NSA_HEAD_DIM


def _div_pow2(x, d):
    assert d & (d - 1) == 0
    return x >> (d.bit_length() - 1)


def _alibi_col(first_head, n):
    head = first_head + lax.broadcasted_iota(jnp.int32, (n, 1), 0)
    return jnp.exp((head + 1).astype(F32) * (-8.0 / NSA_HEADS * 0.6931471805599453))


def _nsa_cmp_sample_body(q_ref, k_ref, v_ref, o_ref, idx_ref):
    nblk = k_ref.shape[2]
    pos = nblk * CMP_BLOCK
    blk_end = (lax.broadcasted_iota(jnp.int32, (1, nblk), 1) + 1) * CMP_BLOCK - 1
    dist = (pos - blk_end).astype(F32)
    imps = []
    for g in range(NSA_KV_HEADS):
        q = q_ref[0, g].astype(BF16)
        s = _dot_nt(q, k_ref[0, g]) - _alibi_col(g * NSA_GROUP, NSA_GROUP) * dist
        p = _softmax_rows(s, jnp.full(s.shape, True))
        o_ref[0, g] = jnp.dot(p.astype(BF16), v_ref[0, g], preferred_element_type=F32)
        imps.append(jnp.sum(p, axis=0, keepdims=True))
    imp = jnp.concatenate(imps, axis=0)
    jj = lax.broadcasted_iota(jnp.int32, (NSA_KV_HEADS, nblk), 1)
    score = jnp.where(jj == 0, FORCE, jnp.where(jj == nblk - 1, FORCE, imp))
    rank = jnp.zeros((NSA_KV_HEADS, nblk), jnp.int32)
    for i in range(nblk):
        ci = score[:, i:i + 1]
        rank = rank + jnp.where(ci > score, 1, jnp.where(ci == score, jnp.where(jj > i, 1, 0), 0))
    n_pick = min(TOP_K_BLOCKS, nblk + 1) - 1
    cols = [jnp.sum(jnp.where(rank == r, jj, 0), axis=1, keepdims=True) for r in range(n_pick)]
    cols.append(jnp.full((NSA_KV_HEADS, 1), nblk, jnp.int32))
    idx_ref[0] = jnp.concatenate(cols, axis=1)


def _nsa_cmp_sample(q, summ):
    b = q.shape[0]
    nblk = summ.shape[3]
    n_sel = min(TOP_K_BLOCKS, nblk + 1)
    sum_spec = pl.BlockSpec((1, NSA_KV_HEADS, nblk, NSA_HEAD_DIM), lambda i: (i, 0, 0, 0))
    return pl.pallas_call(
        _nsa_cmp_sample_body,
        grid=(b,),
        in_specs=[pl.BlockSpec((1, NSA_KV_HEADS, NSA_GROUP, NSA_HEAD_DIM), lambda i: (i, 0, 0, 0)),
                  sum_spec, sum_spec],
        out_specs=[pl.BlockSpec((1, NSA_KV_HEADS, NSA_GROUP, NSA_HEAD_DIM), lambda i: (i, 0, 0, 0)),
                   pl.BlockSpec((1, NSA_KV_HEADS, n_sel), lambda i: (i, 0, 0))],
        out_shape=[jax.ShapeDtypeStruct((b, NSA_KV_HEADS, NSA_GROUP, NSA_HEAD_DIM), F32),
                   jax.ShapeDtypeStruct((b, NSA_KV_HEADS, n_sel), jnp.int32)],
        compiler_params=pltpu.CompilerParams(dimension_semantics=("arbitrary",)),
        name="nsa_cmp_sample",
    )(q, summ[0], summ[1])


def _nsa_sel_sample_body(n_past, idx_ref, pt_ref, q_ref, new_ref, *refs):
    del pt_ref
    n_sel = len(refs) - 1
    pages, o_ref = refs[:n_sel], refs[n_sel]
    bi, g = pl.program_id(0), pl.program_id(1)
    pos = n_past * SEL_BLOCK
    per_page = PAGE_SIZE // SEL_BLOCK
    qf = q_ref[0, 0]
    q = qf.astype(BF16)
    slope = _alibi_col(g * NSA_GROUP, NSA_GROUP)
    tok = lax.broadcasted_iota(jnp.int32, (1, PAGE_SIZE), 1)
    vts, scores = [], []
    for k in range(n_sel):
        j = idx_ref[bi, g * n_sel + k]
        kv = pages[k][0].astype(BF16)
        dist = (pos - (_div_pow2(j, per_page) * PAGE_SIZE + tok)).astype(F32)
        s = jnp.dot(q, kv[:NSA_HEAD_DIM], preferred_element_type=F32) - slope * dist
        in_block = _div_pow2(tok, SEL_BLOCK) == (j & (per_page - 1))
        keep = jnp.where(j < n_past, jnp.where(in_block, 1, 0), 0) > 0
        scores.append(jnp.where(keep, s, NEG))
        vts.append(kv[NSA_HEAD_DIM:])
    new = new_ref[0, 0]
    s_new = jnp.sum(qf * new[0:1], axis=1, keepdims=True)
    m = s_new
    for s in scores:
        m = jnp.maximum(m, jnp.max(s, axis=1, keepdims=True))
    e_new = jnp.exp(s_new - m)
    l = e_new
    acc = e_new * new[1:2]
    for s, vt in zip(scores, vts):
        e = jnp.exp(s - m)
        l = l + jnp.sum(e, axis=1, keepdims=True)
        acc = acc + _dot_nt(e.astype(BF16), vt)
    o_ref[0, 0] = acc / l


def _nsa_channel_major(cache):
    lead = cache.ndim - 4
    perm = tuple(range(lead)) + (lead + 1, lead + 2, lead + 3, lead)
    t = cache.transpose(perm)
    return t.reshape(t.shape[:lead] + (NSA_KV_COLS, t.shape[-1]))


def _nsa_sel_sample(q, new_kv, cache_sel, layer, page_table, idx):
    b, g, n_sel = idx.shape
    n_pages = page_table.shape[1]
    per_page = PAGE_SIZE // SEL_BLOCK
    n_past = n_pages * per_page
    cache = _nsa_channel_major(cache_sel)

    def page_spec(k):
        def index(bi, gi, idx_ref, pt_ref):
            j = jnp.minimum(idx_ref[bi, gi * n_sel + k], n_past - 1)
            return (layer, pt_ref[bi, _div_pow2(j, per_page)], gi, 0)
        return pl.BlockSpec((None, 1, NSA_KV_PAIR, PAGE_SIZE), index)

    grp_spec = lambda rows: pl.BlockSpec((1, 1, rows, NSA_HEAD_DIM), lambda bi, gi, i_, p_: (bi, gi, 0, 0))
    grid_spec = pltpu.PrefetchScalarGridSpec(
        num_scalar_prefetch=2,
        grid=(b, g),
        in_specs=[grp_spec(NSA_GROUP), grp_spec(2)] + [page_spec(k) for k in range(n_sel)],
        out_specs=grp_spec(NSA_GROUP),
    )
    return pl.pallas_call(
        functools.partial(_nsa_sel_sample_body, n_past),
        grid_spec=grid_spec,
        out_shape=jax.ShapeDtypeStruct((b, g, NSA_GROUP, NSA_HEAD_DIM), F32),
        compiler_params=pltpu.CompilerParams(dimension_semantics=("arbitrary", "arbitrary")),
        name="nsa_sel_sample",
    )(idx.reshape(b, g * n_sel), page_table, q, new_kv, *([cache] * n_sel))


def _nsa_win_sample_body(q_ref, new_ref, win_ref, o_ref, nwin_ref):
    wbuf = win_ref.shape[2]
    win = win_ref[0]
    tok = lax.broadcasted_iota(jnp.int32, (1, wbuf), 1)
    dist = wbuf - tok
    valid = dist < WINDOW
    for g in range(NSA_KV_HEADS):
        r0 = g * NSA_KV_PAIR
        qf = q_ref[0, g]
        new = new_ref[0, g]
        kt = win[r0:r0 + NSA_HEAD_DIM].astype(BF16)
        vt = win[r0 + NSA_HEAD_DIM:r0 + NSA_KV_PAIR].astype(BF16)
        slope = _alibi_col(g * NSA_GROUP, NSA_GROUP)
        s = jnp.dot(qf.astype(BF16), kt, preferred_element_type=F32) - slope * dist.astype(F32)
        s = jnp.where(valid, s, NEG)
        s_new = jnp.sum(qf * new[0:1], axis=1, keepdims=True)
        m = jnp.maximum(s_new, jnp.max(s, axis=1, keepdims=True))
        e = jnp.exp(s - m)
        e_new = jnp.exp(s_new - m)
        acc = e_new * new[1:2] + _dot_nt(e.astype(BF16), vt)
        o_ref[0, g] = acc / (e_new + jnp.sum(e, axis=1, keepdims=True))
    cols = win.shape[0]
    eye = (lax.broadcasted_iota(jnp.int32, (cols, cols), 0) == lax.broadcasted_iota(jnp.int32, (cols, cols), 1))
    new_row = jnp.concatenate([new_ref[0, g][c:c + 1] for g in range(NSA_KV_HEADS) for c in range(2)], axis=1)
    new_col = jnp.sum(jnp.where(eye, new_row, 0.0), axis=1, keepdims=True)
    nwin_ref[0] = jnp.where(tok == wbuf - 1, new_col, pltpu.roll(win, wbuf - 1, 1))


def _nsa_win_sample(q, new_kv, win_state):
    b, wbuf = win_state.shape[:2]
    assert wbuf == WINDOW
    win = _nsa_channel_major(win_state)
    q_spec = pl.BlockSpec((1, NSA_KV_HEADS, NSA_GROUP, NSA_HEAD_DIM), lambda i: (i, 0, 0, 0))
    win_spec = pl.BlockSpec((1, NSA_KV_COLS, wbuf), lambda i: (i, 0, 0))
    return pl.pallas_call(
        _nsa_win_sample_body,
        grid=(b,),
        in_specs=[q_spec, pl.BlockSpec((1, NSA_KV_HEADS, 2, NSA_HEAD_DIM), lambda i: (i, 0, 0, 0)), win_spec],
        out_specs=[q_spec, win_spec],
        out_shape=[jax.ShapeDtypeStruct((b, NSA_KV_HEADS, NSA_GROUP, NSA_HEAD_DIM), F32),
                   jax.ShapeDtypeStruct((b, NSA_KV_COLS, wbuf), F32)],
        compiler_params=pltpu.CompilerParams(dimension_semantics=("arbitrary",)),
        name="nsa_win_sample",
    )(q, new_kv, win)


MLSTM_KERNEL_CHUNK = 256
MLSTM_QK_COLS = MLSTM_HEADS * MLSTM_DQK
MLSTM_V_COLS = MLSTM_HEADS * MLSTM_DV


def _log_sigmoid(x):
    return jnp.minimum(x, 0.0) - jnp.log(1.0 + jnp.exp(-jnp.abs(x)))


def _mlstm_chunk_body(q_ref, k_ref, v_ref, og_ref, vt_ref, ig_ref, fg_ref,
                      h_ref, c_ref, n_ref, m_ref, c_sc, n_sc, m_sc):
    chunk = pl.program_id(2)

    @pl.when(chunk == 0)
    def _():
        c_sc[...] = jnp.zeros_like(c_sc)
        n_sc[...] = jnp.zeros_like(n_sc)
        m_sc[...] = jnp.zeros_like(m_sc)

    L = q_ref.shape[1]
    q = q_ref[0]
    qb = q.astype(BF16)
    kb = (k_ref[0] * (MLSTM_DQK ** -0.5)).astype(BF16)
    i_row = ig_ref[0, 0, 0]
    f_row = _log_sigmoid(fg_ref[0, 0, 0])
    tt = lax.broadcasted_iota(jnp.int32, (L, L), 0)
    ss = lax.broadcasted_iota(jnp.int32, (L, L), 1)
    tri = ss <= tt
    b_col = jnp.sum(jnp.where(tri, f_row, 0.0), axis=1, keepdims=True)
    b_row = jnp.sum(jnp.where(tt == ss, b_col, 0.0), axis=0, keepdims=True)
    m = m_sc[...]
    c = c_sc[...]
    n = n_sc[...]
    d = jnp.where(tri, b_col - b_row + i_row, NEG)
    inter = b_col + m
    mt = jnp.maximum(inter, jnp.max(d, axis=1, keepdims=True))
    w = jnp.exp(d - mt)
    gq = jnp.exp(inter - mt)
    a = w * _dot_nt(qb, kb)
    num = (jnp.dot(a.astype(BF16), v_ref[0].astype(BF16), preferred_element_type=F32)
           + gq * _dot_nt(qb, c.astype(BF16)))
    den = jnp.sum(a, axis=1, keepdims=True) + gq * jnp.sum(q * n, axis=1, keepdims=True)
    hc = num / jnp.maximum(jnp.abs(den), jnp.exp(-mt))
    h_ref[0] = hc * jax.nn.sigmoid(og_ref[0])

    b_last = b_col[L - 1:L, :]
    m_new = mt[L - 1:L, :]
    wl = jnp.exp(b_last - b_row + i_row - m_new)
    gl = jnp.exp(b_last + m - m_new)
    c_new = gl * c + jnp.dot((vt_ref[0, 0] * wl).astype(BF16), kb, preferred_element_type=F32)
    wl8 = jnp.broadcast_to(wl, (8, L)).astype(BF16)
    n_new = gl * n + jnp.dot(wl8, kb, preferred_element_type=F32)[0:1]
    c_sc[...] = c_new
    n_sc[...] = n_new
    m_sc[...] = m_new

    @pl.when(chunk == pl.num_programs(2) - 1)
    def _():
        c_ref[0, 0] = c_new
        n_ref[0, 0] = n_new
        m_ref[0, 0] = m_new


def _mlstm_prompt(z, b_gate):
    B, S, _ = z.shape
    H, L = MLSTM_HEADS, MLSTM_KERNEL_CHUNK
    assert S % L == 0
    nc = S // L
    kblk, vblk = MLSTM_QK_COLS // MLSTM_DQK, (2 * MLSTM_QK_COLS) // MLSTM_DV
    gates = z[..., 2 * MLSTM_QK_COLS + 2 * MLSTM_V_COLS:] + b_gate
    gates = gates.reshape(B, nc, L, 2, H).transpose(3, 0, 4, 1, 2)[:, :, :, :, None, :]
    vt = z[..., 2 * MLSTM_QK_COLS:2 * MLSTM_QK_COLS + MLSTM_V_COLS].reshape(B, S, H, MLSTM_DV)
    vt = vt.transpose(0, 2, 3, 1)
    gate_spec = pl.BlockSpec((1, 1, 1, 1, L), lambda b, h, c: (b, h, c, 0, 0))
    state = lambda r, w: pl.BlockSpec((1, 1, r, w), lambda b, h, c: (b, h, 0, 0))
    hs, c, n, m = pl.pallas_call(
        _mlstm_chunk_body,
        grid=(B, H, nc),
        in_specs=[pl.BlockSpec((1, L, MLSTM_DQK), lambda b, h, c: (b, c, h)),
                  pl.BlockSpec((1, L, MLSTM_DQK), lambda b, h, c: (b, c, kblk + h)),
                  pl.BlockSpec((1, L, MLSTM_DV), lambda b, h, c: (b, c, vblk + h)),
                  pl.BlockSpec((1, L, MLSTM_DV), lambda b, h, c: (b, c, vblk + H + h)),
                  pl.BlockSpec((1, 1, MLSTM_DV, L), lambda b, h, c: (b, h, 0, c)),
                  gate_spec, gate_spec],
        out_specs=[pl.BlockSpec((1, L, MLSTM_DV), lambda b, h, c: (b, c, h)),
                   state(MLSTM_DV, MLSTM_DQK), state(1, MLSTM_DQK), state(1, 1)],
        out_shape=[jax.ShapeDtypeStruct((B, S, MLSTM_V_COLS), F32),
                   jax.ShapeDtypeStruct((B, H, MLSTM_DV, MLSTM_DQK), F32),
                   jax.ShapeDtypeStruct((B, H, 1, MLSTM_DQK), F32),
                   jax.ShapeDtypeStruct((B, H, 1, 1), F32)],
        scratch_shapes=[pltpu.VMEM((MLSTM_DV, MLSTM_DQK), F32), pltpu.VMEM((1, MLSTM_DQK), F32),
                        pltpu.VMEM((1, 1), F32)],
        compiler_params=pltpu.CompilerParams(dimension_semantics=("arbitrary",) * 3),
        name="mlstm_chunks",
    )(z, z, z, z, vt, gates[0], gates[1])
    return hs, c, n.reshape(B, H, MLSTM_DQK), m.reshape(B, H)


def _mlstm_step_body(q_ref, k_ref, v_ref, og_ref, ig_ref, fg_ref, c_ref, n_ref, m_ref,
                     h_ref, c_out, n_out, m_out):
    eye = (lax.broadcasted_iota(jnp.int32, (MLSTM_DV, MLSTM_DV), 0)
           == lax.broadcasted_iota(jnp.int32, (MLSTM_DV, MLSTM_DV), 1))
    for h in range(MLSTM_HEADS):
        c, n, m = c_ref[0, h], n_ref[0, h], m_ref[0, h]
        q = q_ref[0, h]
        k = k_ref[0, h] * (MLSTM_DQK ** -0.5)
        v = v_ref[0, h]
        i_g = ig_ref[0, h]
        inter = _log_sigmoid(fg_ref[0, h]) + m
        mt = jnp.maximum(inter, i_g)
        w = jnp.exp(i_g - mt)
        gq = jnp.exp(inter - mt)
        a = w * jnp.sum(q * k, axis=1, keepdims=True)
        cq_col = jnp.sum(c * q, axis=1, keepdims=True)
        cq_row = jnp.sum(jnp.where(eye, cq_col, 0.0), axis=0, keepdims=True)
        den = a + gq * jnp.sum(n * q, axis=1, keepdims=True)
        hc = (a * v + gq * cq_row) / jnp.maximum(jnp.abs(den), jnp.exp(-mt))
        h_ref[0, h] = hc * jax.nn.sigmoid(og_ref[0, h])
        v_col = jnp.sum(jnp.where(eye, v, 0.0), axis=1, keepdims=True)
        c_out[0, h] = gq * c + (w * v_col) * k
        n_out[0, h] = gq * n + w * k
        m_out[0, h] = mt


def _mlstm_sample(z, b_gate, c0, n0, m0):
    B, T, _ = z.shape
    assert T == 1
    H = MLSTM_HEADS
    z = z.reshape(B, -1)
    qk, hv = MLSTM_QK_COLS, MLSTM_V_COLS
    q = z[:, :qk].reshape(B, H, 1, MLSTM_DQK)
    k = z[:, qk:2 * qk].reshape(B, H, 1, MLSTM_DQK)
    v = z[:, 2 * qk:2 * qk + hv].reshape(B, H, 1, MLSTM_DV)
    og = z[:, 2 * qk + hv:2 * qk + 2 * hv].reshape(B, H, 1, MLSTM_DV)
    gates = z[:, 2 * qk + 2 * hv:] + b_gate
    ig = gates[:, :H].reshape(B, H, 1, 1)
    fg = gates[:, H:].reshape(B, H, 1, 1)
    spec = lambda r, w: pl.BlockSpec((1, H, r, w), lambda b: (b, 0, 0, 0))
    shapes = [(1, MLSTM_DV), (MLSTM_DV, MLSTM_DQK), (1, MLSTM_DQK), (1, 1)]
    hs, c, n, m = pl.pallas_call(
        _mlstm_step_body,
        grid=(B,),
        in_specs=[spec(1, MLSTM_DQK), spec(1, MLSTM_DQK), spec(1, MLSTM_DV), spec(1, MLSTM_DV),
                  spec(1, 1), spec(1, 1), spec(MLSTM_DV, MLSTM_DQK), spec(1, MLSTM_DQK), spec(1, 1)],
        out_specs=[spec(*s) for s in shapes],
        out_shape=[jax.ShapeDtypeStruct((B, H) + s, F32) for s in shapes],
        compiler_params=pltpu.CompilerParams(dimension_semantics=("arbitrary",)),
        name="mlstm_step",
    )(q, k, v, og, ig, fg, c0, n0.reshape(B, H, 1, MLSTM_DQK), m0.reshape(B, H, 1, 1))
    return hs.reshape(B, T, hv), c, n.reshape(B, H, MLSTM_DQK), m.reshape(B, H)


def _rmsnorm(x, g):
    return x * lax.rsqrt(jnp.mean(x * x, axis=-1, keepdims=True) + RMS_EPS) * g


def _rope(x, pos):
    half = x.shape[-1] // 2
    freqs = ROPE_THETA ** (-jnp.arange(half, dtype=F32) / half)
    ang = pos.astype(F32)[:, None] * freqs[None, :]
    cos = jnp.cos(ang)[None, :, None, :]
    sin = jnp.sin(ang)[None, :, None, :]
    x1, x2 = x[..., :half], x[..., half:]
    return jnp.concatenate([x1 * cos - x2 * sin, x1 * sin + x2 * cos], axis=-1)


def _alibi_slopes(n):
    return 2.0 ** (-8.0 * jnp.arange(1, n + 1, dtype=F32) / n)


def _to_blocks(a, axis):
    n = a.shape[axis] // Q_BLOCK
    a = a.reshape(a.shape[:axis] + (n, Q_BLOCK) + a.shape[axis + 1:])
    return jnp.moveaxis(a, axis, 0)


def _from_blocks(a, axis):
    a = jnp.moveaxis(a, 0, axis)
    return a.reshape(a.shape[:axis] + (-1,) + a.shape[axis + 2:])


def _attend(q, ks, vs, kposs, q_pos, scale, slopes=None, window=None):
    B, Tq, H, Dk = q.shape
    kvh = ks[0].shape[2]
    G = H // kvh
    qg = q.reshape(B, Tq, kvh, G, Dk)
    scores = []
    for k, kp in zip(ks, kposs):
        s = jnp.einsum('bqgnd,bkgd->bgnqk', qg, k).astype(F32) * scale
        dist = q_pos[:, None] - kp[None, :]
        mask = (dist >= 0) & (kp[None, :] >= 0)
        if window is not None:
            mask = mask & (dist < window)
        if slopes is not None:
            s = s - slopes.reshape(kvh, G)[None, :, :, None, None] * dist.astype(F32)
        scores.append(jnp.where(mask, s, NEG))
    p = jax.nn.softmax(jnp.concatenate(scores, axis=-1), axis=-1)
    out, off = 0.0, 0
    for v in vs:
        n = v.shape[1]
        out = out + jnp.einsum('bgnqk,bkgd->bqgnd', p[..., off:off + n], v)
        off += n
    return out.reshape(B, Tq, H, -1)


def _mla_prompt(a, g_q, g_kv, w_uq, w_uk, w_uv):
    B, S, _ = a.shape
    H = MLA_HEADS
    pos = jnp.arange(S, dtype=jnp.int32)
    a2 = a.reshape(B * S, -1)
    q = _norm_proj(a2[:, :MLA_Q_LORA], g_q, w_uq).reshape(B, S, H, MLA_NOPE_DIM + MLA_ROPE_DIM)
    w_kv = jnp.concatenate([w_uk.reshape(MLA_KV_LORA, -1), w_uv.reshape(MLA_KV_LORA, -1)], axis=1)
    kv, ckv = _norm_proj(a2[:, MLA_Q_LORA:MLA_Q_LORA + MLA_KV_LORA], g_kv, w_kv, with_normed=True)
    kpe = _rope(a[..., MLA_Q_LORA + MLA_KV_LORA:][:, :, None, :], pos)
    q_pe = _rope(q[..., MLA_NOPE_DIM:], pos)
    zpad = jnp.zeros((B, S, H, MLA_QK_PAD - MLA_NOPE_DIM - MLA_ROPE_DIM), F32)
    qscale = MLA_SCALE * LOG2E
    qc = jnp.concatenate([q[..., :MLA_NOPE_DIM] * qscale, q_pe * qscale, zpad], axis=-1)
    k_nope = kv[:, :H * MLA_NOPE_DIM].reshape(B, S, H, MLA_NOPE_DIM)
    kc = jnp.concatenate([k_nope, jnp.broadcast_to(kpe, (B, S, H, MLA_ROPE_DIM)), zpad], axis=-1)
    v = kv[:, H * MLA_NOPE_DIM:].reshape(B, S, H, MLA_V_DIM)
    ones = jnp.zeros((MLA_QK_PAD - MLA_V_DIM,), F32).at[0].set(1.0)
    vc = jnp.concatenate([v, jnp.broadcast_to(ones, (B, S, H, MLA_QK_PAD - MLA_V_DIM))], axis=-1)
    to_heads = lambda t: t.transpose(0, 2, 1, 3).astype(BF16)
    o = _mla_flash(to_heads(qc), to_heads(kc), to_heads(vc))
    new_rows = jnp.concatenate([ckv.reshape(B, S, -1), kpe[:, :, 0]], axis=-1)
    return o, new_rows


def _mla_sample(a, cache, j, page_table, g_q, g_kv, w_uq, w_uk, w_uv):
    B, T, _ = a.shape
    assert T == 1
    H = MLA_HEADS
    past_len = page_table.shape[1] * cache.shape[2]
    pos = past_len + jnp.arange(T, dtype=jnp.int32)
    a2 = a.reshape(B * T, -1)
    q = _norm_proj(a2[:, :MLA_Q_LORA], g_q, w_uq).reshape(B, T, H, MLA_NOPE_DIM + MLA_ROPE_DIM)
    w_kv = jnp.concatenate([w_uk.reshape(MLA_KV_LORA, -1), w_uv.reshape(MLA_KV_LORA, -1)], axis=1)
    _, ckv = _norm_proj(a2[:, MLA_Q_LORA:MLA_Q_LORA + MLA_KV_LORA], g_kv, w_kv, with_normed=True)
    kpe = _rope(a[..., MLA_Q_LORA + MLA_KV_LORA:][:, :, None, :], pos)
    q_pe = _rope(q[..., MLA_NOPE_DIM:], pos)
    new_rows = jnp.concatenate([ckv.reshape(B, T, -1), kpe[:, :, 0]], axis=-1)
    q_nope = q[:, 0, :, :MLA_NOPE_DIM].transpose(1, 0, 2)
    q_lat = _heads_matmul(q_nope, w_uk.transpose(1, 2, 0)).transpose(1, 0, 2)
    q_abs = jnp.concatenate([q_lat, q_pe[:, 0]], axis=-1) * MLA_SCALE
    o_lat = _mla_decode(q_abs, new_rows, cache, j, page_table)
    o = _heads_matmul(o_lat.transpose(1, 0, 2), w_uv.transpose(1, 0, 2))
    return o.transpose(1, 0, 2).reshape(B, T, -1), new_rows


def _mlstm_split(z, b_gate):
    B, T, _ = z.shape
    qk = MLSTM_HEADS * MLSTM_DQK
    hv = MLSTM_HEADS * MLSTM_DV
    q = z[..., :qk].reshape(B, T, MLSTM_HEADS, MLSTM_DQK)
    k = z[..., qk:2 * qk].reshape(B, T, MLSTM_HEADS, MLSTM_DQK) * (MLSTM_DQK ** -0.5)
    v = z[..., 2 * qk:2 * qk + hv].reshape(B, T, MLSTM_HEADS, MLSTM_DV)
    o = jax.nn.sigmoid(z[..., 2 * qk + hv:2 * qk + 2 * hv]).reshape(B, T, MLSTM_HEADS, MLSTM_DV)
    gt = z[..., 2 * qk + 2 * hv:] + b_gate
    ig = gt[..., :MLSTM_HEADS]
    lf = jax.nn.log_sigmoid(gt[..., MLSTM_HEADS:])
    return q, k, v, o, ig, lf


def _mlstm_chunkwise(q, k, v, ig, lf, c0, n0, m0):
    B, T, H, _ = q.shape
    L = MLSTM_CHUNK if T % MLSTM_CHUNK == 0 else T
    nc = T // L

    def split(a):
        a = a.reshape((B, nc, L) + a.shape[2:])
        return jnp.moveaxis(a, 1, 0)

    causal = jnp.tril(jnp.ones((L, L), dtype=bool))

    def step(carry, xs):
        c, n, m = carry
        qc, kc, vc, ic, fc = xs
        b = jnp.cumsum(fc, axis=1).transpose(0, 2, 1)
        it = ic.transpose(0, 2, 1)
        d = jnp.where(causal, b[:, :, :, None] - b[:, :, None, :] + it[:, :, None, :], NEG)
        inter = b + m[:, :, None]
        mt = jnp.maximum(inter, d.max(axis=-1))
        w = jnp.exp(d - mt[..., None])
        g = jnp.exp(inter - mt)
        a = w * jnp.einsum('bthd,bshd->bhts', qc, kc)
        num = jnp.einsum('bhts,bshe->bthe', a, vc) + jnp.einsum('bht,bhed,bthd->bthe', g, c, qc)
        den = a.sum(axis=-1) + g * jnp.einsum('bhd,bthd->bht', n, qc)
        hc = num / jnp.maximum(jnp.abs(den), jnp.exp(-mt)).transpose(0, 2, 1)[..., None]
        m_new = mt[:, :, -1]
        wl = jnp.exp(b[:, :, -1:] - b + it - m_new[:, :, None])
        gl = jnp.exp(b[:, :, -1] + m - m_new)
        c_new = gl[:, :, None, None] * c + jnp.einsum('bhs,bshe,bshd->bhed', wl, vc, kc)
        n_new = gl[:, :, None] * n + jnp.einsum('bhs,bshd->bhd', wl, kc)
        return (c_new, n_new, m_new), hc

    (c, n, m), hs = lax.scan(step, (c0, n0, m0), (split(q), split(k), split(v), split(ig), split(lf)))
    hs = jnp.moveaxis(hs, 0, 1).reshape(B, T, H, -1)
    return hs, c, n, m


def _mlstm_mix(z, c0, n0, m0, b_gate):
    B, T, _ = z.shape
    q, k, v, o, ig, lf = _mlstm_split(z, b_gate)
    hs, c, n, m = _mlstm_chunkwise(q, k, v, ig, lf, c0, n0, m0)
    return (hs * o).reshape(B, T, -1), c, n, m


def _nsa_split(z):
    B, T, _ = z.shape
    q = z[..., :NSA_Q_COLS].reshape(B, T, NSA_HEADS, NSA_HEAD_DIM)
    kv = z[..., NSA_Q_COLS:NSA_Q_COLS + 3 * NSA_KV_COLS].reshape(B, T, 3, NSA_KV_HEADS, 2, NSA_HEAD_DIM)
    g = jax.nn.sigmoid(z[..., NSA_Q_COLS + 3 * NSA_KV_COLS:]).reshape(B, T, NSA_HEADS, 3)
    return q, kv[:, :, 0], kv[:, :, 1], kv[:, :, 2], g


def _summarize(kv, w_cmp):
    B, T = kv.shape[:2]
    nb = T // CMP_BLOCK
    blk = kv[:, :nb * CMP_BLOCK].reshape(B, nb, CMP_BLOCK, NSA_KV_HEADS, 2, NSA_HEAD_DIM)
    return jnp.einsum('bnlgcd,clde->bngce', blk, w_cmp)


def _nsa_compressed(q, summ, q_pos, slopes):
    B, T, H, Dh = q.shape
    nbc = summ.shape[1]
    G = H // NSA_KV_HEADS
    qg = q.reshape(B, T, NSA_KV_HEADS, G, Dh)
    s = jnp.einsum('btgnd,bjgd->bgntj', qg, summ[..., 0, :]) * NSA_SCALE
    end = (jnp.arange(nbc, dtype=jnp.int32) + 1) * CMP_BLOCK - 1
    dist = q_pos[:, None] - end[None, :]
    mask = dist >= 0
    s = jnp.where(mask, s - slopes.reshape(NSA_KV_HEADS, G)[None, :, :, None, None] * dist.astype(F32), NEG)
    p = jnp.where(mask, jax.nn.softmax(s, axis=-1), 0.0)
    o = jnp.einsum('bgntj,bjgd->btgnd', p, summ[..., 1, :]).reshape(B, T, H, Dh)
    return o, p.sum(axis=2)


def _nsa_select(imp, q_pos, nb):
    nbc = imp.shape[-1]
    score = jnp.pad(imp, ((0, 0), (0, 0), (0, 0), (0, nb - nbc)))
    j = jnp.arange(nb, dtype=jnp.int32)[None, :]
    cur = (q_pos // SEL_BLOCK)[:, None]
    forced = (j == 0) | (j == cur) | (j == cur - 1)
    score = jnp.where(forced, FORCE, jnp.where(j <= cur, score, -FORCE))
    return lax.top_k(score, min(TOP_K_BLOCKS, nb))[1]


def _sel_attn(q, q_pos, kg, vg, idx, slopes):
    B, Tq, H, Dh = q.shape
    G = H // NSA_KV_HEADS
    qg = q.reshape(B, Tq, NSA_KV_HEADS, G, Dh)
    s = jnp.einsum('btgnd,bgtkld->bgntkl', qg, kg) * NSA_SCALE
    rows = idx[..., None] * SEL_BLOCK + jnp.arange(SEL_BLOCK, dtype=jnp.int32)
    dist = (q_pos[None, None, :, None, None] - rows)[:, :, None]
    sl = slopes.reshape(NSA_KV_HEADS, G)[None, :, :, None, None, None]
    s = jnp.where(dist >= 0, s - sl * dist.astype(F32), NEG)
    shp = s.shape
    p = jax.nn.softmax(s.reshape(shp[:4] + (-1,)), axis=-1).reshape(shp)
    o = jnp.einsum('bgntkl,bgtkld->btgnd', p, vg)
    return o.reshape(B, Tq, H, Dh)


def _nsa_merge(g, o_cmp, o_sel, o_win):
    o = g[..., 0:1] * o_cmp + g[..., 1:2] * o_sel + g[..., 2:3] * o_win
    B, T = o.shape[:2]
    return o.reshape(B, T, -1)


def _nsa_seq_layout(kv):
    return kv.transpose(3, 0, 2, 1, 4).astype(BF16)


def _nsa_prompt(z, w_bd):
    B, S, _ = z.shape
    nb = S // CMP_BLOCK
    q = (z[..., :NSA_Q_COLS] * (NSA_SCALE * LOG2E)).astype(BF16)
    q = q.reshape(B, S, NSA_HEADS, NSA_HEAD_DIM).transpose(0, 2, 1, 3)
    kv = z[..., NSA_Q_COLS:NSA_Q_COLS + 3 * NSA_KV_COLS].reshape(B, S, 3, NSA_KV_HEADS, 2, NSA_HEAD_DIM)
    kv_c, kv_s, kv_w = kv[:, :, 0], kv[:, :, 1], kv[:, :, 2]
    summ = _summarize_blocks(kv_c.reshape(B * nb, CMP_BLOCK * NSA_KV_COLS), w_bd)
    summ = summ.reshape(B, nb, NSA_KV_HEADS, 2, NSA_HEAD_DIM).transpose(3, 0, 2, 1, 4).astype(BF16)
    gate = z[..., NSA_Q_COLS + 3 * NSA_KV_COLS:].reshape(B, S, NSA_KV_HEADS, 3 * NSA_GROUP).transpose(0, 2, 1, 3)
    y = _nsa_prompt_attend(q, summ, _nsa_seq_layout(kv_s), _nsa_seq_layout(kv_w), gate)
    return y, kv_c, kv_s, kv_w[:, -min(WINDOW, S):]


def _nsa_sample(z, cache_cmp, cache_sel, win_state, j, page_table, w_paged):
    B, T, _ = z.shape
    assert T == 1 and T < CMP_BLOCK
    G, Dh = NSA_KV_HEADS, NSA_HEAD_DIM
    assert cache_cmp.shape[2] == PAGE_SIZE
    n_pages = page_table.shape[1]
    q, kv_c, kv_s, kv_w, gate = _nsa_split(z)
    summ = _summarize_pool(cache_cmp[j], w_paged)[page_table]
    summ = summ.transpose(3, 0, 2, 1, 4, 5).reshape(2, B, G, n_pages * SUMM_PAGE_BLOCKS, Dh).astype(BF16)
    qg = q.reshape(B, G, NSA_GROUP, Dh) * NSA_SCALE
    o_cmp, idx = _nsa_cmp_sample(qg, summ)
    o_sel = _nsa_sel_sample(qg, kv_s.reshape(B, G, 2, Dh), cache_sel, j, page_table, idx)
    o_win, new_win = _nsa_win_sample(qg, kv_w.reshape(B, G, 2, Dh), win_state)
    heads = lambda o: o.reshape(B, T, NSA_HEADS, Dh)
    y = _nsa_merge(gate, heads(o_cmp), heads(o_sel), heads(o_win))
    new_win = new_win.reshape(B, G, 2, Dh, new_win.shape[-1]).transpose(0, 4, 1, 2, 3)
    return y, kv_c, kv_s, new_win


def kernel(x_prompt, x_sample, cache_mla_kv, state_mlstm_c, state_mlstm_n, state_mlstm_m, cache_nsa_cmp,
           cache_nsa_sel, state_nsa_win, page_table, norm_g, final_norm_g, mla_w_a, mla_g_q, mla_g_kv,
           mla_w_uq, mla_w_uk, mla_w_uv, mla_w_o, mlstm_w_in, mlstm_b_gate, mlstm_w_out, nsa_w_in,
           nsa_w_cmp, nsa_w_out, mlp_w1, mlp_w2):
    B, S, D = x_prompt.shape
    Bs, Ts, _ = x_sample.shape
    xp = x_prompt.reshape(B * S, D)
    xs = x_sample.reshape(Bs * Ts, D)
    mla_p, mla_s = [], []
    mc_p, mn_p, mm_p, mc_s, mn_s, mm_s = [], [], [], [], [], []
    cmp_p, cmp_s, sel_p, sel_s, win_p, win_s = [], [], [], [], [], []
    for i in range(DEPTH):
        j = i // N_MIXERS
        g0 = norm_g[i, 0]
        if i % N_MIXERS == 0:
            ap = _norm_proj(xp, g0, mla_w_a[j]).reshape(B, S, -1)
            as_ = _norm_proj(xs, g0, mla_w_a[j]).reshape(Bs, Ts, -1)
            w = (mla_g_q[j], mla_g_kv[j], mla_w_uq[j], mla_w_uk[j], mla_w_uv[j])
            op, rp = _mla_prompt(ap, *w)
            os_, rs = _mla_sample(as_, cache_mla_kv, j, page_table, *w)
            mla_p.append(rp)
            mla_s.append(rs)
            w_out = mla_w_o[j]
        elif i % N_MIXERS == 1:
            zp = _norm_proj(xp, g0, mlstm_w_in[j]).reshape(B, S, -1)
            zs = _norm_proj(xs, g0, mlstm_w_in[j]).reshape(Bs, Ts, -1)
            op, cp, nst_p, mp = _mlstm_prompt(zp, mlstm_b_gate[j])
            os_, cs, nst_s, ms = _mlstm_sample(zs, mlstm_b_gate[j], state_mlstm_c[j], state_mlstm_n[j],
                                               state_mlstm_m[j])
            mc_p.append(cp)
            mn_p.append(nst_p)
            mm_p.append(mp)
            mc_s.append(cs)
            mn_s.append(nst_s)
            mm_s.append(ms)
            w_out = mlstm_w_out[j]
        else:
            zp = _norm_proj(xp, g0, nsa_w_in[j]).reshape(B, S, -1)
            zs = _norm_proj(xs, g0, nsa_w_in[j]).reshape(Bs, Ts, -1)
            op, kcp, ksp, kwp = _nsa_prompt(zp, _summ_weights(nsa_w_cmp[j]))
            os_, kcs, kss, kws = _nsa_sample(zs, cache_nsa_cmp, cache_nsa_sel, state_nsa_win[j], j,
                                             page_table, _summ_weights_paged(nsa_w_cmp[j]))
            cmp_p.append(kcp)
            cmp_s.append(kcs)
            sel_p.append(ksp)
            sel_s.append(kss)
            win_p.append(kwp)
            win_s.append(kws)
            w_out = nsa_w_out[j]
        xp = _proj_res(op.reshape(B * S, -1), w_out, xp)
        xs = _proj_res(os_.reshape(Bs * Ts, -1), w_out, xs)
        xp = _mlp_res(xp, norm_g[i, 1], mlp_w1[i], mlp_w2[i])
        xs = _mlp_res(xs, norm_g[i, 1], mlp_w1[i], mlp_w2[i])
    y_prompt = _final_norm(xp, final_norm_g).reshape(B, S, D)
    y_sample = _final_norm(xs, final_norm_g).reshape(Bs, Ts, D)
    return (y_prompt, y_sample,
            jnp.stack(mla_p), jnp.stack(mla_s),
            jnp.stack(mc_p), jnp.stack(mn_p), jnp.stack(mm_p),
            jnp.stack(mc_s), jnp.stack(mn_s), jnp.stack(mm_s),
            jnp.stack(cmp_p), jnp.stack(cmp_s),
            jnp.stack(sel_p), jnp.stack(sel_s),
            jnp.stack(win_p), jnp.stack(win_s))
```

```python
import functools

import jax
import jax.numpy as jnp
from jax import lax
from jax.experimental import pallas as pl
from jax.experimental.pallas import tpu as pltpu

F32 = jnp.float32
BF16 = jnp.bfloat16

D_MODEL = 1024
DEPTH = 4
N_MIXERS = 3
PAGE_SIZE = 128

MLA_HEADS = 16
MLA_NOPE_DIM = 64
MLA_ROPE_DIM = 32
MLA_V_DIM = 64
MLA_Q_LORA = 384
MLA_KV_LORA = 256
MLA_SCALE = (MLA_NOPE_DIM + MLA_ROPE_DIM) ** -0.5
ROPE_THETA = 10000.0

MLSTM_HEADS = 4
MLSTM_DQK = 128
MLSTM_DV = 256
MLSTM_CHUNK = 64

NSA_HEADS = 16
NSA_KV_HEADS = 4
NSA_HEAD_DIM = 64
CMP_BLOCK = 64
SEL_BLOCK = 64
TOP_K_BLOCKS = 16
WINDOW = 512
NSA_Q_COLS = NSA_HEADS * NSA_HEAD_DIM
NSA_KV_COLS = NSA_KV_HEADS * 2 * NSA_HEAD_DIM
NSA_SCALE = NSA_HEAD_DIM ** -0.5

D_FF = 4 * D_MODEL
Q_BLOCK = 128
RMS_EPS = 1e-6
NEG = -1e30
FORCE = 1e4

VMEM_LIMIT_BYTES = 56 * 1024 * 1024
FF_CHUNK = 512


def _row_tile(m):
    for t in (512, 256, 128):
        if m % t == 0:
            return t
    return m


def _rms_rows(x, g):
    return x * lax.rsqrt(jnp.mean(x * x, axis=-1, keepdims=True) + RMS_EPS) * g


def _norm_proj_body(x_ref, g_ref, w_ref, o_ref, *h_ref):
    h = _rms_rows(x_ref[...], g_ref[...])
    o_ref[...] = jnp.dot(h.astype(BF16), w_ref[...], preferred_element_type=F32)
    if h_ref:
        h_ref[0][...] = h


def _norm_proj(x, g, w, with_normed=False):
    m, d = x.shape
    n = w.shape[1]
    tm = _row_tile(m)
    out_specs = [pl.BlockSpec((tm, n), lambda i: (i, 0))]
    out_shape = [jax.ShapeDtypeStruct((m, n), F32)]
    if with_normed:
        out_specs.append(pl.BlockSpec((tm, d), lambda i: (i, 0)))
        out_shape.append(jax.ShapeDtypeStruct((m, d), F32))
    out = pl.pallas_call(
        _norm_proj_body,
        grid=(m // tm,),
        in_specs=[pl.BlockSpec((tm, d), lambda i: (i, 0)),
                  pl.BlockSpec((1, d), lambda i: (0, 0)),
                  pl.BlockSpec((d, n), lambda i: (0, 0))],
        out_specs=out_specs,
        out_shape=out_shape,
        compiler_params=pltpu.CompilerParams(dimension_semantics=("arbitrary",),
                                             vmem_limit_bytes=VMEM_LIMIT_BYTES),
        name="norm_proj",
    )(x, g.reshape(1, d), w.astype(BF16))
    return out if with_normed else out[0]


def _proj_res_body(a_ref, w_ref, r_ref, o_ref):
    o_ref[...] = r_ref[...] + jnp.dot(a_ref[...].astype(BF16), w_ref[...], preferred_element_type=F32)


def _proj_res(a, w, res):
    m, k = a.shape
    d = w.shape[1]
    tm = _row_tile(m)
    return pl.pallas_call(
        _proj_res_body,
        grid=(m // tm,),
        in_specs=[pl.BlockSpec((tm, k), lambda i: (i, 0)),
                  pl.BlockSpec((k, d), lambda i: (0, 0)),
                  pl.BlockSpec((tm, d), lambda i: (i, 0))],
        out_specs=pl.BlockSpec((tm, d), lambda i: (i, 0)),
        out_shape=jax.ShapeDtypeStruct((m, d), F32),
        compiler_params=pltpu.CompilerParams(dimension_semantics=("arbitrary",),
                                             vmem_limit_bytes=VMEM_LIMIT_BYTES),
        name="proj_res",
    )(a, w.astype(BF16), res)


def _mlp_body(x_ref, g_ref, w1_ref, w2_ref, o_ref):
    x = x_ref[...]
    h = _rms_rows(x, g_ref[...]).astype(BF16)
    acc = x
    for c in range(D_FF // FF_CHUNK):
        a = jnp.dot(h, w1_ref[:, c * FF_CHUNK:(c + 1) * FF_CHUNK], preferred_element_type=F32)
        a = jnp.maximum(a, 0.0)
        acc = acc + jnp.dot((a * a).astype(BF16), w2_ref[c * FF_CHUNK:(c + 1) * FF_CHUNK, :],
                            preferred_element_type=F32)
    o_ref[...] = acc


def _mlp_res(x, g, w1, w2):
    m, d = x.shape
    tm = _row_tile(m)
    return pl.pallas_call(
        _mlp_body,
        grid=(m // tm,),
        in_specs=[pl.BlockSpec((tm, d), lambda i: (i, 0)),
                  pl.BlockSpec((1, d), lambda i: (0, 0)),
                  pl.BlockSpec((d, D_FF), lambda i: (0, 0)),
                  pl.BlockSpec((D_FF, d), lambda i: (0, 0))],
        out_specs=pl.BlockSpec((tm, d), lambda i: (i, 0)),
        out_shape=jax.ShapeDtypeStruct((m, d), F32),
        compiler_params=pltpu.CompilerParams(dimension_semantics=("arbitrary",),
                                             vmem_limit_bytes=VMEM_LIMIT_BYTES),
        name="mlp_res",
    )(x, g.reshape(1, d), w1.astype(BF16), w2.astype(BF16))


def _final_norm_body(x_ref, g_ref, o_ref):
    o_ref[...] = _rms_rows(x_ref[...], g_ref[...])


def _final_norm(x, g):
    m, d = x.shape
    tm = _row_tile(m)
    return pl.pallas_call(
        _final_norm_body,
        grid=(m // tm,),
        in_specs=[pl.BlockSpec((tm, d), lambda i: (i, 0)), pl.BlockSpec((1, d), lambda i: (0, 0))],
        out_specs=pl.BlockSpec((tm, d), lambda i: (i, 0)),
        out_shape=jax.ShapeDtypeStruct((m, d), F32),
        compiler_params=pltpu.CompilerParams(dimension_semantics=("arbitrary",)),
        name="final_norm",
    )(x, g.reshape(1, d))


def _dot_nt(a, b):
    return lax.dot_general(a, b, (((1,), (1,)), ((), ())), preferred_element_type=F32)


SUMM_L_PER_STEP = 8
SUMM_COLS = NSA_KV_COLS
SUMM_HALF = SUMM_COLS // 2


def _summ_weights(w_cmp):
    wk = jnp.stack([w_cmp[0], w_cmp[1], w_cmp[0], w_cmp[1]], axis=0)
    bd = jnp.einsum('kj,klde->lkdje', jnp.eye(4, dtype=F32), wk)
    return bd.reshape(CMP_BLOCK, SUMM_HALF, SUMM_HALF).astype(BF16)


def _summarize_body(x_ref, w_ref, o_ref):
    @pl.when(pl.program_id(1) == 0)
    def _():
        o_ref[...] = jnp.zeros_like(o_ref)

    lo = o_ref[:, :SUMM_HALF]
    hi = o_ref[:, SUMM_HALF:]
    for li in range(SUMM_L_PER_STEP):
        x = x_ref[:, li * SUMM_COLS:(li + 1) * SUMM_COLS].astype(BF16)
        w = w_ref[li]
        lo = lo + jnp.dot(x[:, :SUMM_HALF], w, preferred_element_type=F32)
        hi = hi + jnp.dot(x[:, SUMM_HALF:], w, preferred_element_type=F32)
    o_ref[:, :SUMM_HALF] = lo
    o_ref[:, SUMM_HALF:] = hi


def _summarize_blocks(x2d, w_bd):
    nb = x2d.shape[0]
    p = 512 if nb % 512 == 0 else nb
    step_cols = SUMM_L_PER_STEP * SUMM_COLS
    return pl.pallas_call(
        _summarize_body,
        grid=(nb // p, CMP_BLOCK // SUMM_L_PER_STEP),
        in_specs=[pl.BlockSpec((p, step_cols), lambda i, l: (i, l)),
                  pl.BlockSpec((SUMM_L_PER_STEP, SUMM_HALF, SUMM_HALF), lambda i, l: (l, 0, 0))],
        out_specs=pl.BlockSpec((p, SUMM_COLS), lambda i, l: (i, 0)),
        out_shape=jax.ShapeDtypeStruct((nb, SUMM_COLS), F32),
        compiler_params=pltpu.CompilerParams(dimension_semantics=("arbitrary", "arbitrary"),
                                             vmem_limit_bytes=VMEM_LIMIT_BYTES),
        name="nsa_summarize",
    )(x2d, w_bd)


SUMM_D_PER_STEP = 8
SUMM_PAGE_BLOCKS = PAGE_SIZE // CMP_BLOCK
SUMM_PAGE_OUT = 2 * SUMM_PAGE_BLOCKS * NSA_HEAD_DIM


def _summ_weights_paged(w_cmp):
    n = 2 * SUMM_PAGE_BLOCKS
    wk = jnp.stack([w_cmp[c] for c in range(2) for _ in range(SUMM_PAGE_BLOCKS)], axis=0)
    bd = jnp.einsum('kj,klde->dklje', jnp.eye(n, dtype=F32), wk)
    return bd.reshape(NSA_HEAD_DIM, 2 * PAGE_SIZE, SUMM_PAGE_OUT).astype(BF16)


def _rows_by_channel(x):
    p, n, lanes = x.shape
    assert n == 8 and p % 8 == 0
    x4 = x.reshape(p // 8, 8, 8, lanes)
    parts = [x4[:, i] for i in range(8)]
    sub = lax.broadcasted_iota(jnp.int32, (1, 8, lanes), 1)
    for s in (4, 2, 1):
        low = (sub & s) == 0
        nxt = list(parts)
        for i in range(8):
            if i & s == 0:
                a, b = parts[i], parts[i + s]
                nxt[i] = jnp.where(low, a, pltpu.roll(b, s, 1))
                nxt[i + s] = jnp.where(low, pltpu.roll(a, 8 - s, 1), b)
        parts = nxt
    return [t.reshape(p, lanes) for t in parts]


def _summarize_pool_body(k_ref, v_ref, w_ref, o_ref):
    dg = pl.program_id(2)

    @pl.when(dg == 0)
    def _():
        o_ref[...] = jnp.zeros_like(o_ref)

    acc = o_ref[...]
    ks = _rows_by_channel(k_ref[...])
    vs = _rows_by_channel(v_ref[...])
    for dd in range(SUMM_D_PER_STEP):
        lhs = jnp.concatenate([ks[dd], vs[dd]], axis=1).astype(BF16)
        acc = acc + jnp.dot(lhs, w_ref[dg * SUMM_D_PER_STEP + dd], preferred_element_type=F32)
    o_ref[...] = acc


def _summarize_pool(cache, w_paged):
    pool = cache.shape[0]
    cm = _nsa_channel_major(cache)
    p = 512 if pool % 512 == 0 else pool
    d_steps = NSA_HEAD_DIM // SUMM_D_PER_STEP
    rows_per_group = NSA_KV_PAIR // SUMM_D_PER_STEP
    out = pl.pallas_call(
        _summarize_pool_body,
        grid=(pool // p, NSA_KV_HEADS, d_steps),
        in_specs=[pl.BlockSpec((p, SUMM_D_PER_STEP, PAGE_SIZE), lambda i, g, d: (i, g * rows_per_group + d, 0)),
                  pl.BlockSpec((p, SUMM_D_PER_STEP, PAGE_SIZE),
                               lambda i, g, d: (i, g * rows_per_group + d_steps + d, 0)),
                  pl.BlockSpec((NSA_HEAD_DIM, 2 * PAGE_SIZE, SUMM_PAGE_OUT), lambda i, g, d: (0, 0, 0))],
        out_specs=pl.BlockSpec((p, SUMM_PAGE_OUT), lambda i, g, d: (i, g)),
        out_shape=jax.ShapeDtypeStruct((pool, NSA_KV_HEADS * SUMM_PAGE_OUT), F32),
        compiler_params=pltpu.CompilerParams(dimension_semantics=("arbitrary",) * 3,
                                             vmem_limit_bytes=VMEM_LIMIT_BYTES),
        name="nsa_summarize_pool",
    )(cm, cm, w_paged)
    return out.reshape(pool, NSA_KV_HEADS, 2, SUMM_PAGE_BLOCKS, NSA_HEAD_DIM)


NSA_TQ = 128
NSA_TK = 512
NSA_GROUP = NSA_HEADS // NSA_KV_HEADS
NSA_BAND = WINDOW + NSA_TQ


def _softmax_rows(s, valid):
    m = jnp.max(s, axis=1, keepdims=True)
    e = jnp.where(valid, jnp.exp(s - m), 0.0)
    l = jnp.sum(e, axis=1, keepdims=True)
    return e / jnp.where(l > 0.0, l, 1.0)


LOG2E = 1.4426950408889634
MASK_BIG = 1e30
M_INIT = -0.5e30


def _nsa_prompt_body(q_ref, ksum_ref, vsum_ref, ks_ref, vs_ref, kw_ref, vw_ref, gate_ref, o_ref):
    g = pl.program_id(1)
    q0 = pl.program_id(2) * NSA_TQ
    rows = NSA_GROUP * NSA_TQ
    nblk = ksum_ref.shape[2]
    dh = NSA_HEAD_DIM
    q = q_ref[0].reshape(rows, dh)

    row = lax.broadcasted_iota(jnp.int32, (rows, 1), 0)
    tok_in_tile = row & (NSA_TQ - 1)
    qpos = q0 + tok_in_tile
    head = g * NSA_GROUP + (row >> (NSA_TQ.bit_length() - 1))
    slope = jnp.exp((head + 1).astype(F32) * (-8.0 / NSA_HEADS * 0.6931471805599453)) * LOG2E

    s = _dot_nt(q, ksum_ref[0, 0])
    blk_end = (lax.broadcasted_iota(jnp.int32, (1, nblk), 1) + 1) * CMP_BLOCK - 1
    dist = qpos - blk_end
    vis = dist >= 0
    s = jnp.where(vis, s - slope * dist.astype(F32), NEG)
    e = jnp.where(vis, jnp.exp2(s - jnp.max(s, axis=1, keepdims=True)), 0.0)
    l = jnp.sum(e, axis=1, keepdims=True)
    p = e / jnp.where(l > 0.0, l, 1.0)
    o_cmp = jnp.dot(p.astype(BF16), vsum_ref[0, 0], preferred_element_type=F32)
    imp = p[0:NSA_TQ]
    for h in range(1, NSA_GROUP):
        imp = imp + p[h * NSA_TQ:(h + 1) * NSA_TQ]

    imp_t = jnp.concatenate([imp, jnp.zeros((NSA_TQ, NSA_TQ - nblk), F32)], axis=1).T[:nblk]
    cur = (q0 + lax.broadcasted_iota(jnp.int32, (1, NSA_TQ), 1)) >> 6
    jj = lax.broadcasted_iota(jnp.int32, (nblk, NSA_TQ), 0)
    forced = jnp.where(jj == 0, 1, jnp.where(jj == cur, 1, jnp.where(jj == cur - 1, 1, 0)))
    score = jnp.where(forced > 0, FORCE, jnp.where(jj <= cur, imp_t, -FORCE))
    rank = jnp.zeros((nblk, NSA_TQ), jnp.int32)
    for i in range(nblk):
        ci = score[i:i + 1, :]
        rank = rank + jnp.where(ci > score, 1, jnp.where(ci == score, jnp.where(jj > i, 1, 0), 0))
    unpicked_t = jnp.where(rank < min(TOP_K_BLOCKS, nblk), 0.0, -1.0)
    unpicked = jnp.concatenate([unpicked_t, jnp.zeros((NSA_TQ - nblk, NSA_TQ), F32)], axis=0).T[:, :dh]
    qs = jnp.concatenate([q, jnp.concatenate([unpicked.astype(BF16)] * NSA_GROUP, axis=0)], axis=1)

    def sel_tile(kt, carry, diagonal):
        m, acc = carry
        k0 = pl.multiple_of(kt * NSA_TK, NSA_TK)
        rel = k0 - q0 + lax.broadcasted_iota(jnp.int32, (1, NSA_TK), 1)
        s = _dot_nt(qs, ks_ref[0, 0, pl.ds(k0, NSA_TK), :]) + slope * rel.astype(F32)
        if diagonal:
            s = jnp.where(rel <= tok_in_tile, s, NEG)
        m_new = jnp.maximum(m, jnp.max(s, axis=1, keepdims=True))
        e = jnp.exp2(s - m_new).astype(BF16)
        acc = jnp.exp2(m - m_new) * acc + jnp.dot(e, vs_ref[0, 0, pl.ds(k0, NSA_TK), :],
                                                  preferred_element_type=F32)
        return m_new, acc

    n_kt = (q0 + NSA_TQ + NSA_TK - 1) // NSA_TK
    init = (jnp.full((rows, 1), M_INIT, F32), jnp.zeros((rows, 2 * dh), F32))
    carry = lax.fori_loop(0, n_kt - 1, functools.partial(sel_tile, diagonal=False), init)
    _, acc = sel_tile(n_kt - 1, carry, True)
    l = acc[:, dh:dh + 1]
    o_sel = acc[:, :dh] / jnp.where(l > 0.0, l, 1.0)

    w0 = pl.multiple_of(jnp.maximum(q0 - WINDOW, 0), NSA_TQ)
    dist = qpos - (w0 + lax.broadcasted_iota(jnp.int32, (1, NSA_BAND), 1))
    valid = jnp.where(dist >= 0, jnp.where(dist < WINDOW, 1, 0), 0) > 0
    s = jnp.where(valid, _dot_nt(q, kw_ref[0, 0, pl.ds(w0, NSA_BAND), :]) - slope * dist.astype(F32), NEG)
    e = jnp.exp2(s - jnp.max(s, axis=1, keepdims=True))
    o_win = (jnp.dot(e.astype(BF16), vw_ref[0, 0, pl.ds(w0, NSA_BAND), :], preferred_element_type=F32)
             / jnp.sum(e, axis=1, keepdims=True))

    gate = jax.nn.sigmoid(gate_ref[0, 0])
    outs = []
    for h in range(NSA_GROUP):
        r = slice(h * NSA_TQ, (h + 1) * NSA_TQ)
        outs.append(gate[:, 3 * h:3 * h + 1] * o_cmp[r] + gate[:, 3 * h + 1:3 * h + 2] * o_sel[r]
                    + gate[:, 3 * h + 2:3 * h + 3] * o_win[r])
    o_ref[0] = jnp.concatenate(outs, axis=-1)


def _nsa_prompt_attend(q, summ, kv_s, kv_w, gate):
    b, h, s, dh = q.shape
    g = NSA_KV_HEADS
    nblk = summ.shape[3]
    assert s % NSA_TK == 0 and s >= NSA_BAND and nblk <= dh
    blk = jnp.arange(s, dtype=jnp.int32)[:, None] // SEL_BLOCK
    onehot = jnp.where(blk == jnp.arange(dh, dtype=jnp.int32)[None, :], MASK_BIG, 0.0).astype(BF16)
    ks = jnp.concatenate([kv_s[0], jnp.broadcast_to(onehot, (b, g, s, dh))], axis=-1)
    ones = jnp.zeros((s, dh), BF16).at[:, 0].set(1.0)
    vs = jnp.concatenate([kv_s[1], jnp.broadcast_to(ones, (b, g, s, dh))], axis=-1)
    seq_spec = lambda w: pl.BlockSpec((1, 1, s, w), lambda bi, gi, qi: (bi, gi, 0, 0))
    sum_spec = pl.BlockSpec((1, 1, nblk, dh), lambda bi, gi, qi: (bi, gi, 0, 0))
    return pl.pallas_call(
        _nsa_prompt_body,
        grid=(b, g, s // NSA_TQ),
        in_specs=[pl.BlockSpec((1, NSA_GROUP, NSA_TQ, dh), lambda bi, gi, qi: (bi, gi, qi, 0)),
                  sum_spec, sum_spec, seq_spec(2 * dh), seq_spec(2 * dh), seq_spec(dh), seq_spec(dh),
                  pl.BlockSpec((1, 1, NSA_TQ, 3 * NSA_GROUP), lambda bi, gi, qi: (bi, gi, qi, 0))],
        out_specs=pl.BlockSpec((1, NSA_TQ, NSA_GROUP * dh), lambda bi, gi, qi: (bi, qi, gi)),
        out_shape=jax.ShapeDtypeStruct((b, s, h * dh), F32),
        compiler_params=pltpu.CompilerParams(dimension_semantics=("arbitrary",) * 3,
                                             vmem_limit_bytes=VMEM_LIMIT_BYTES),
        name="nsa_prompt_attend",
    )(q, summ[0], summ[1], ks, vs, kv_w[0], kv_w[1], gate)


def _heads_matmul_body(x_ref, w_ref, o_ref):
    o_ref[0] = jnp.dot(x_ref[0].astype(BF16), w_ref[0], preferred_element_type=F32)


def _heads_matmul(x, w):
    h, m, k = x.shape
    n = w.shape[2]
    return pl.pallas_call(
        _heads_matmul_body,
        grid=(h,),
        in_specs=[pl.BlockSpec((1, m, k), lambda i: (i, 0, 0)), pl.BlockSpec((1, k, n), lambda i: (i, 0, 0))],
        out_specs=pl.BlockSpec((1, m, n), lambda i: (i, 0, 0)),
        out_shape=jax.ShapeDtypeStruct((h, m, n), F32),
        compiler_params=pltpu.CompilerParams(dimension_semantics=("arbitrary",)),
        name="heads_matmul",
    )(x, w.astype(BF16))


MLA_TILE = 512
MLA_QK_PAD = 128
MLA_HEAD_PAIR = 2


def _mla_flash_body(q_ref, k_ref, kpe_ref, v_ref, o_ref):
    qi = pl.program_id(2)
    t = MLA_TILE
    w = MLA_QK_PAD
    causal = lax.broadcasted_iota(jnp.int32, (t, t), 0) >= lax.broadcasted_iota(jnp.int32, (t, t), 1)
    qs = [q_ref[0, :, hh * w:(hh + 1) * w] for hh in range(MLA_HEAD_PAIR)]

    def step(kt, carry, diagonal):
        k0 = pl.multiple_of(kt * t, t)
        kpe = kpe_ref[0, pl.ds(k0, t), :]
        out = []
        for hh in range(MLA_HEAD_PAIR):
            m, acc = carry[hh]
            s = _dot_nt(qs[hh], k_ref[0, pl.ds(k0, t), hh * w:(hh + 1) * w] + kpe)
            if diagonal:
                s = jnp.where(causal, s, NEG)
            m_new = jnp.maximum(m, jnp.max(s, axis=1, keepdims=True))
            e = jnp.exp2(s - m_new).astype(BF16)
            acc = jnp.exp2(m - m_new) * acc + jnp.dot(e, v_ref[0, pl.ds(k0, t), hh * w:(hh + 1) * w],
                                                      preferred_element_type=F32)
            out.append((m_new, acc))
        return tuple(out)

    init = tuple((jnp.full((t, 1), M_INIT, F32), jnp.zeros((t, MLA_QK_PAD), F32)) for _ in range(MLA_HEAD_PAIR))
    carry = lax.fori_loop(0, qi, functools.partial(step, diagonal=False), init)
    carry = step(qi, carry, True)
    o_ref[0] = jnp.concatenate([acc[:, :MLA_V_DIM] / acc[:, MLA_V_DIM:MLA_V_DIM + 1] for _, acc in carry],
                               axis=-1)


def _mla_flash(q, k, kpe, v):
    b, s, hw = q.shape
    t = MLA_TILE
    pair = MLA_HEAD_PAIR * MLA_QK_PAD
    assert s % t == 0 and hw % pair == 0
    seq = lambda bi, hi, qi: (bi, 0, hi)
    return pl.pallas_call(
        _mla_flash_body,
        grid=(b, hw // pair, s // t),
        in_specs=[pl.BlockSpec((1, t, pair), lambda bi, hi, qi: (bi, qi, hi)),
                  pl.BlockSpec((1, s, pair), seq),
                  pl.BlockSpec((1, s, MLA_QK_PAD), lambda bi, hi, qi: (bi, 0, 0)),
                  pl.BlockSpec((1, s, pair), seq)],
        out_specs=pl.BlockSpec((1, t, MLA_HEAD_PAIR * MLA_V_DIM), lambda bi, hi, qi: (bi, qi, hi)),
        out_shape=jax.ShapeDtypeStruct((b, s, (hw // MLA_QK_PAD) * MLA_V_DIM), F32),
        compiler_params=pltpu.CompilerParams(dimension_semantics=("arbitrary",) * 3,
                                             vmem_limit_bytes=VMEM_LIMIT_BYTES),
        name="mla_flash",
    )(q, k, kpe, v)


def _rot_cols(w):
    half = w.shape[-1] // 2
    return jnp.concatenate([-w[..., half:], w[..., :half]], axis=-1)


def _mla_prompt_weights(w_a, w_uq, w_uk, w_uv):
    H, N, R, W = MLA_HEADS, MLA_NOPE_DIM, MLA_ROPE_DIM, MLA_QK_PAD
    d = w_a.shape[0]
    w_kpe = w_a[:, MLA_Q_LORA + MLA_KV_LORA:]
    lanes = lambda w: jnp.concatenate([jnp.zeros((d, N), F32), w, jnp.zeros((d, W - N - R), F32)], axis=1)
    w_a_ext = jnp.concatenate([w_a[:, :MLA_Q_LORA], jnp.zeros((d, W), F32),
                               w_a[:, MLA_Q_LORA:MLA_Q_LORA + MLA_KV_LORA], lanes(w_kpe), lanes(_rot_cols(w_kpe))],
                              axis=1)
    wq = w_uq.reshape(MLA_Q_LORA, H, N + R)
    zq = jnp.zeros((MLA_Q_LORA, H, W - N - R), F32)
    plain = jnp.concatenate([wq, zq], axis=-1)
    rot = jnp.concatenate([jnp.zeros((MLA_Q_LORA, H, N), F32), _rot_cols(wq[..., N:]), zq], axis=-1)
    w_q = jnp.concatenate([plain.reshape(MLA_Q_LORA, H * W), rot.reshape(MLA_Q_LORA, H * W)], axis=1)
    zk = jnp.zeros((MLA_KV_LORA, H, W - N), F32)
    zv = jnp.zeros((MLA_KV_LORA, H, W - MLA_V_DIM), F32)
    w_kv = jnp.concatenate([jnp.concatenate([w_uk, zk], axis=-1).reshape(MLA_KV_LORA, H * W),
                            jnp.concatenate([w_uv, zv], axis=-1).reshape(MLA_KV_LORA, H * W)], axis=1)
    return w_a_ext, w_q, w_kv


def _rope_tables(s):
    half = MLA_ROPE_DIM // 2
    freqs = ROPE_THETA ** (-jnp.arange(half, dtype=F32) / half)
    ang = jnp.arange(s, dtype=F32)[:, None] * freqs[None, :]
    pad = jnp.zeros((s, MLA_QK_PAD - MLA_NOPE_DIM - MLA_ROPE_DIM), F32)
    cos = jnp.concatenate([jnp.ones((s, MLA_NOPE_DIM), F32), jnp.cos(ang), jnp.cos(ang), pad], axis=1)
    sin = jnp.concatenate([jnp.zeros((s, MLA_NOPE_DIM), F32), jnp.sin(ang), jnp.sin(ang), pad], axis=1)
    return cos, sin


def _mla_q_body(x_ref, g_ref, w_ref, cos_ref, sin_ref, o_ref):
    h = _rms_rows(x_ref[...], g_ref[...]).astype(BF16)
    z = jnp.dot(h, w_ref[...], preferred_element_type=F32)
    cos = cos_ref[...] * (MLA_SCALE * LOG2E)
    sin = sin_ref[...] * (MLA_SCALE * LOG2E)
    hw = o_ref.shape[1]
    for c in range(0, hw, MLA_QK_PAD):
        o_ref[:, c:c + MLA_QK_PAD] = (z[:, c:c + MLA_QK_PAD] * cos
                                      + z[:, hw + c:hw + c + MLA_QK_PAD] * sin).astype(BF16)


def _mla_kv_body(x_ref, g_ref, w_ref, ka_ref, kb_ref, cos_ref, sin_ref, ckv_ref, k_ref, v_ref, kpe_ref, kpeb_ref):
    ckv = _rms_rows(x_ref[...], g_ref[...])
    ckv_ref[...] = ckv
    z = jnp.dot(ckv.astype(BF16), w_ref[...], preferred_element_type=F32)
    hw = k_ref.shape[1]
    k_ref[...] = z[:, :hw].astype(BF16)
    ones_col = jnp.where(lax.broadcasted_iota(jnp.int32, (1, MLA_QK_PAD), 1) == MLA_V_DIM, 1.0, 0.0)
    for c in range(0, hw, MLA_QK_PAD):
        v_ref[:, c:c + MLA_QK_PAD] = (z[:, hw + c:hw + c + MLA_QK_PAD] + ones_col).astype(BF16)
    kpe = ka_ref[...] * cos_ref[...] + kb_ref[...] * sin_ref[...]
    kpe_ref[...] = kpe
    kpeb_ref[...] = kpe.astype(BF16)


def _mla_prompt_proj(a_ext, g_q, g_kv, w_q, w_kv, s):
    m = a_ext.shape[0]
    tm = _row_tile(m)
    assert s % tm == 0
    W = MLA_QK_PAD
    hw = MLA_HEADS * W
    cos, sin = _rope_tables(s)
    n_pos = s // tm
    row = lambda width, blk: pl.BlockSpec((tm, width), lambda i: (i, blk))
    table = pl.BlockSpec((tm, W), lambda i: (i % n_pos, 0))
    full = lambda r, c: pl.BlockSpec((r, c), lambda i: (0, 0))
    params = pltpu.CompilerParams(dimension_semantics=("arbitrary",), vmem_limit_bytes=VMEM_LIMIT_BYTES)
    q = pl.pallas_call(
        _mla_q_body,
        grid=(m // tm,),
        in_specs=[row(MLA_Q_LORA, 0), full(1, MLA_Q_LORA), full(MLA_Q_LORA, 2 * hw), table, table],
        out_specs=row(hw, 0),
        out_shape=jax.ShapeDtypeStruct((m, hw), BF16),
        compiler_params=params,
        name="mla_q_proj",
    )(a_ext, g_q.reshape(1, -1), w_q.astype(BF16), cos, sin)
    ckv_blk = (MLA_Q_LORA + W) // MLA_KV_LORA
    ka_blk = (MLA_Q_LORA + W + MLA_KV_LORA) // W
    ckv, k, v, kpe, kpe_b = pl.pallas_call(
        _mla_kv_body,
        grid=(m // tm,),
        in_specs=[row(MLA_KV_LORA, ckv_blk), full(1, MLA_KV_LORA), full(MLA_KV_LORA, 2 * hw),
                  row(W, ka_blk), row(W, ka_blk + 1), table, table],
        out_specs=[row(MLA_KV_LORA, 0), row(hw, 0), row(hw, 0), row(W, 0), row(W, 0)],
        out_shape=[jax.ShapeDtypeStruct((m, MLA_KV_LORA), F32), jax.ShapeDtypeStruct((m, hw), BF16),
                   jax.ShapeDtypeStruct((m, hw), BF16), jax.ShapeDtypeStruct((m, W), F32),
                   jax.ShapeDtypeStruct((m, W), BF16)],
        compiler_params=params,
        name="mla_kv_proj",
    )(a_ext, g_kv.reshape(1, -1), w_kv.astype(BF16), a_ext, a_ext, cos, sin)
    return q, ckv, k, v, kpe, kpe_b


MLA_PAGES_PER_STEP = 32
MLA_ROW = MLA_KV_LORA + MLA_ROPE_DIM


def _mla_decode_body(pt_ref, q_ref, new_ref, *refs):
    del pt_ref
    npg = MLA_PAGES_PER_STEP
    pages, o_ref = refs[:npg], refs[npg]
    m_sc, l_sc, acc_sc = refs[npg + 1:]
    step = pl.program_id(1)

    @pl.when(step == 0)
    def _():
        m_sc[...] = jnp.full_like(m_sc, NEG)
        l_sc[...] = jnp.zeros_like(l_sc)
        acc_sc[...] = jnp.zeros_like(acc_sc)

    qf = q_ref[0]
    q = qf.astype(BF16)
    kt = jnp.concatenate([pages[p][0, 0].astype(BF16) for p in range(npg)], axis=1)
    s = jnp.dot(q, kt, preferred_element_type=F32)
    m = m_sc[...]
    m_new = jnp.maximum(m, jnp.max(s, axis=1, keepdims=True))
    alpha = jnp.exp(m - m_new)
    e = jnp.exp(s - m_new)
    l_new = alpha * l_sc[...] + jnp.sum(e, axis=1, keepdims=True)
    acc_new = alpha * acc_sc[...] + _dot_nt(e.astype(BF16), kt[:MLA_KV_LORA])
    m_sc[...] = m_new
    l_sc[...] = l_new
    acc_sc[...] = acc_new

    @pl.when(step == pl.num_programs(1) - 1)
    def _():
        new = new_ref[0]
        s_new = jnp.sum(qf * new, axis=1, keepdims=True)
        m_fin = jnp.maximum(m_new, s_new)
        a = jnp.exp(m_new - m_fin)
        e_new = jnp.exp(s_new - m_fin)
        o_ref[0] = (a * acc_new + e_new * new[:, :MLA_KV_LORA]) / (a * l_new + e_new)


def _mla_decode(q, new_rows, cache, layer, page_table):
    b, h, w = q.shape
    n_pages = page_table.shape[1]
    npg = MLA_PAGES_PER_STEP
    assert n_pages % npg == 0 and cache.shape[2] == PAGE_SIZE
    cache = cache.transpose(0, 1, 3, 2)

    def page_spec(p):
        return pl.BlockSpec((1, 1, w, PAGE_SIZE), lambda bi, si, pt: (layer, pt[bi, si * npg + p], 0, 0))

    grid_spec = pltpu.PrefetchScalarGridSpec(
        num_scalar_prefetch=1,
        grid=(b, n_pages // npg),
        in_specs=[pl.BlockSpec((1, h, w), lambda bi, si, pt: (bi, 0, 0)),
                  pl.BlockSpec((1, 1, w), lambda bi, si, pt: (bi, 0, 0))] + [page_spec(p) for p in range(npg)],
        out_specs=pl.BlockSpec((1, h, MLA_KV_LORA), lambda bi, si, pt: (bi, 0, 0)),
        scratch_shapes=[pltpu.VMEM((h, 1), F32), pltpu.VMEM((h, 1), F32), pltpu.VMEM((h, MLA_KV_LORA), F32)],
    )
    return pl.pallas_call(
        _mla_decode_body,
        grid_spec=grid_spec,
        out_shape=jax.ShapeDtypeStruct((b, h, MLA_KV_LORA), F32),
        compiler_params=pltpu.CompilerParams(dimension_semantics=("arbitrary", "arbitrary"),
                                             vmem_limit_bytes=VMEM_LIMIT_BYTES),
        name="mla_decode",
    )(page_table, q, new_rows, *([cache] * npg))


NSA_KV_PAIR = 2 * NSA_HEAD_DIM


def _div_pow2(x, d):
    assert d & (d - 1) == 0
    return x >> (d.bit_length() - 1)


def _alibi_col(first_head, n):
    head = first_head + lax.broadcasted_iota(jnp.int32, (n, 1), 0)
    return jnp.exp((head + 1).astype(F32) * (-8.0 / NSA_HEADS * 0.6931471805599453))


def _nsa_cmp_sample_body(q_ref, k_ref, v_ref, o_ref, idx_ref):
    nblk = k_ref.shape[2]
    pos = nblk * CMP_BLOCK
    blk_end = (lax.broadcasted_iota(jnp.int32, (1, nblk), 1) + 1) * CMP_BLOCK - 1
    dist = (pos - blk_end).astype(F32)
    imps = []
    for g in range(NSA_KV_HEADS):
        q = q_ref[0, g].astype(BF16)
        s = _dot_nt(q, k_ref[0, g]) - _alibi_col(g * NSA_GROUP, NSA_GROUP) * dist
        p = _softmax_rows(s, jnp.full(s.shape, True))
        o_ref[0, g] = jnp.dot(p.astype(BF16), v_ref[0, g], preferred_element_type=F32)
        imps.append(jnp.sum(p, axis=0, keepdims=True))
    imp = jnp.concatenate(imps, axis=0)
    jj = lax.broadcasted_iota(jnp.int32, (NSA_KV_HEADS, nblk), 1)
    score = jnp.where(jj == 0, FORCE, jnp.where(jj == nblk - 1, FORCE, imp))
    rank = jnp.zeros((NSA_KV_HEADS, nblk), jnp.int32)
    for i in range(nblk):
        ci = score[:, i:i + 1]
        rank = rank + jnp.where(ci > score, 1, jnp.where(ci == score, jnp.where(jj > i, 1, 0), 0))
    n_pick = min(TOP_K_BLOCKS, nblk + 1) - 1
    cols = [jnp.sum(jnp.where(rank == r, jj, 0), axis=1, keepdims=True) for r in range(n_pick)]
    cols.append(jnp.full((NSA_KV_HEADS, 1), nblk, jnp.int32))
    idx_ref[0] = jnp.concatenate(cols, axis=1)


def _nsa_cmp_sample(q, summ):
    b = q.shape[0]
    nblk = summ.shape[3]
    n_sel = min(TOP_K_BLOCKS, nblk + 1)
    sum_spec = pl.BlockSpec((1, NSA_KV_HEADS, nblk, NSA_HEAD_DIM), lambda i: (i, 0, 0, 0))
    return pl.pallas_call(
        _nsa_cmp_sample_body,
        grid=(b,),
        in_specs=[pl.BlockSpec((1, NSA_KV_HEADS, NSA_GROUP, NSA_HEAD_DIM), lambda i: (i, 0, 0, 0)),
                  sum_spec, sum_spec],
        out_specs=[pl.BlockSpec((1, NSA_KV_HEADS, NSA_GROUP, NSA_HEAD_DIM), lambda i: (i, 0, 0, 0)),
                   pl.BlockSpec((1, NSA_KV_HEADS, n_sel), lambda i: (i, 0, 0))],
        out_shape=[jax.ShapeDtypeStruct((b, NSA_KV_HEADS, NSA_GROUP, NSA_HEAD_DIM), F32),
                   jax.ShapeDtypeStruct((b, NSA_KV_HEADS, n_sel), jnp.int32)],
        compiler_params=pltpu.CompilerParams(dimension_semantics=("arbitrary",)),
        name="nsa_cmp_sample",
    )(q, summ[0], summ[1])


def _nsa_sel_sample_body(n_past, idx_ref, pt_ref, q_ref, new_ref, *refs):
    del pt_ref
    n_sel = len(refs) - 1
    pages, o_ref = refs[:n_sel], refs[n_sel]
    bi, g = pl.program_id(0), pl.program_id(1)
    pos = n_past * SEL_BLOCK
    per_page = PAGE_SIZE // SEL_BLOCK
    qf = q_ref[0, 0]
    q = qf.astype(BF16)
    slope = _alibi_col(g * NSA_GROUP, NSA_GROUP)
    tok = lax.broadcasted_iota(jnp.int32, (1, PAGE_SIZE), 1)
    vts, scores = [], []
    for k in range(n_sel):
        j = idx_ref[bi, g * n_sel + k]
        kv = pages[k][0].astype(BF16)
        dist = (pos - (_div_pow2(j, per_page) * PAGE_SIZE + tok)).astype(F32)
        s = jnp.dot(q, kv[:NSA_HEAD_DIM], preferred_element_type=F32) - slope * dist
        in_block = _div_pow2(tok, SEL_BLOCK) == (j & (per_page - 1))
        keep = jnp.where(j < n_past, jnp.where(in_block, 1, 0), 0) > 0
        scores.append(jnp.where(keep, s, NEG))
        vts.append(kv[NSA_HEAD_DIM:])
    new = new_ref[0, 0]
    s_new = jnp.sum(qf * new[0:1], axis=1, keepdims=True)
    m = s_new
    for s in scores:
        m = jnp.maximum(m, jnp.max(s, axis=1, keepdims=True))
    e_new = jnp.exp(s_new - m)
    l = e_new
    acc = e_new * new[1:2]
    for s, vt in zip(scores, vts):
        e = jnp.exp(s - m)
        l = l + jnp.sum(e, axis=1, keepdims=True)
        acc = acc + _dot_nt(e.astype(BF16), vt)
    o_ref[0, 0] = acc / l


def _nsa_channel_major(cache):
    lead = cache.ndim - 4
    perm = tuple(range(lead)) + (lead + 1, lead + 2, lead + 3, lead)
    t = cache.transpose(perm)
    return t.reshape(t.shape[:lead] + (NSA_KV_COLS, t.shape[-1]))


def _nsa_sel_sample(q, new_kv, cache_sel, layer, page_table, idx):
    b, g, n_sel = idx.shape
    n_pages = page_table.shape[1]
    per_page = PAGE_SIZE // SEL_BLOCK
    n_past = n_pages * per_page
    cache = _nsa_channel_major(cache_sel)

    def page_spec(k):
        def index(bi, gi, idx_ref, pt_ref):
            j = jnp.minimum(idx_ref[bi, gi * n_sel + k], n_past - 1)
            return (layer, pt_ref[bi, _div_pow2(j, per_page)], gi, 0)
        return pl.BlockSpec((None, 1, NSA_KV_PAIR, PAGE_SIZE), index)

    grp_spec = lambda rows: pl.BlockSpec((1, 1, rows, NSA_HEAD_DIM), lambda bi, gi, i_, p_: (bi, gi, 0, 0))
    grid_spec = pltpu.PrefetchScalarGridSpec(
        num_scalar_prefetch=2,
        grid=(b, g),
        in_specs=[grp_spec(NSA_GROUP), grp_spec(2)] + [page_spec(k) for k in range(n_sel)],
        out_specs=grp_spec(NSA_GROUP),
    )
    return pl.pallas_call(
        functools.partial(_nsa_sel_sample_body, n_past),
        grid_spec=grid_spec,
        out_shape=jax.ShapeDtypeStruct((b, g, NSA_GROUP, NSA_HEAD_DIM), F32),
        compiler_params=pltpu.CompilerParams(dimension_semantics=("arbitrary", "arbitrary")),
        name="nsa_sel_sample",
    )(idx.reshape(b, g * n_sel), page_table, q, new_kv, *([cache] * n_sel))


def _nsa_win_sample_body(q_ref, new_ref, win_ref, o_ref, nwin_ref):
    wbuf = win_ref.shape[2]
    win = win_ref[0]
    tok = lax.broadcasted_iota(jnp.int32, (1, wbuf), 1)
    dist = wbuf - tok
    valid = dist < WINDOW
    for g in range(NSA_KV_HEADS):
        r0 = g * NSA_KV_PAIR
        qf = q_ref[0, g]
        new = new_ref[0, g]
        kt = win[r0:r0 + NSA_HEAD_DIM].astype(BF16)
        vt = win[r0 + NSA_HEAD_DIM:r0 + NSA_KV_PAIR].astype(BF16)
        slope = _alibi_col(g * NSA_GROUP, NSA_GROUP)
        s = jnp.dot(qf.astype(BF16), kt, preferred_element_type=F32) - slope * dist.astype(F32)
        s = jnp.where(valid, s, NEG)
        s_new = jnp.sum(qf * new[0:1], axis=1, keepdims=True)
        m = jnp.maximum(s_new, jnp.max(s, axis=1, keepdims=True))
        e = jnp.exp(s - m)
        e_new = jnp.exp(s_new - m)
        acc = e_new * new[1:2] + _dot_nt(e.astype(BF16), vt)
        o_ref[0, g] = acc / (e_new + jnp.sum(e, axis=1, keepdims=True))
    cols = win.shape[0]
    eye = (lax.broadcasted_iota(jnp.int32, (cols, cols), 0) == lax.broadcasted_iota(jnp.int32, (cols, cols), 1))
    new_row = jnp.concatenate([new_ref[0, g][c:c + 1] for g in range(NSA_KV_HEADS) for c in range(2)], axis=1)
    new_col = jnp.sum(jnp.where(eye, new_row, 0.0), axis=1, keepdims=True)
    nwin_ref[0] = jnp.where(tok == wbuf - 1, new_col, pltpu.roll(win, wbuf - 1, 1))


def _nsa_win_sample(q, new_kv, win_state):
    b, wbuf = win_state.shape[:2]
    assert wbuf == WINDOW
    win = _nsa_channel_major(win_state)
    q_spec = pl.BlockSpec((1, NSA_KV_HEADS, NSA_GROUP, NSA_HEAD_DIM), lambda i: (i, 0, 0, 0))
    win_spec = pl.BlockSpec((1, NSA_KV_COLS, wbuf), lambda i: (i, 0, 0))
    return pl.pallas_call(
        _nsa_win_sample_body,
        grid=(b,),
        in_specs=[q_spec, pl.BlockSpec((1, NSA_KV_HEADS, 2, NSA_HEAD_DIM), lambda i: (i, 0, 0, 0)), win_spec],
        out_specs=[q_spec, win_spec],
        out_shape=[jax.ShapeDtypeStruct((b, NSA_KV_HEADS, NSA_GROUP, NSA_HEAD_DIM), F32),
                   jax.ShapeDtypeStruct((b, NSA_KV_COLS, wbuf), F32)],
        compiler_params=pltpu.CompilerParams(dimension_semantics=("arbitrary",)),
        name="nsa_win_sample",
    )(q, new_kv, win)


MLSTM_KERNEL_CHUNK = 256
MLSTM_QK_COLS = MLSTM_HEADS * MLSTM_DQK
MLSTM_V_COLS = MLSTM_HEADS * MLSTM_DV


def _log_sigmoid(x):
    return jnp.minimum(x, 0.0) - jnp.log(1.0 + jnp.exp(-jnp.abs(x)))


def _mlstm_chunk_body(q_ref, k_ref, v_ref, og_ref, vt_ref, ig_ref, fg_ref,
                      h_ref, c_ref, n_ref, m_ref, c_sc, n_sc, m_sc):
    chunk = pl.program_id(2)

    @pl.when(chunk == 0)
    def _():
        c_sc[...] = jnp.zeros_like(c_sc)
        n_sc[...] = jnp.zeros_like(n_sc)
        m_sc[...] = jnp.zeros_like(m_sc)

    L = q_ref.shape[1]
    q = q_ref[0]
    qb = q.astype(BF16)
    kb = (k_ref[0] * (MLSTM_DQK ** -0.5)).astype(BF16)
    i_row = ig_ref[0, 0, 0]
    f_row = _log_sigmoid(fg_ref[0, 0, 0])
    tt = lax.broadcasted_iota(jnp.int32, (L, L), 0)
    ss = lax.broadcasted_iota(jnp.int32, (L, L), 1)
    tri = ss <= tt
    b_col = jnp.sum(jnp.where(tri, f_row, 0.0), axis=1, keepdims=True)
    b_row = jnp.sum(jnp.where(tt == ss, b_col, 0.0), axis=0, keepdims=True)
    m = m_sc[...]
    c = c_sc[...]
    n = n_sc[...]
    d = jnp.where(tri, b_col - b_row + i_row, NEG)
    inter = b_col + m
    mt = jnp.maximum(inter, jnp.max(d, axis=1, keepdims=True))
    w = jnp.exp(d - mt)
    gq = jnp.exp(inter - mt)
    a = w * _dot_nt(qb, kb)
    num = (jnp.dot(a.astype(BF16), v_ref[0].astype(BF16), preferred_element_type=F32)
           + gq * _dot_nt(qb, c.astype(BF16)))
    den = jnp.sum(a, axis=1, keepdims=True) + gq * jnp.sum(q * n, axis=1, keepdims=True)
    hc = num / jnp.maximum(jnp.abs(den), jnp.exp(-mt))
    h_ref[0] = hc * jax.nn.sigmoid(og_ref[0])

    b_last = b_col[L - 1:L, :]
    m_new = mt[L - 1:L, :]
    wl = jnp.exp(b_last - b_row + i_row - m_new)
    gl = jnp.exp(b_last + m - m_new)
    c_new = gl * c + jnp.dot((vt_ref[0, 0] * wl).astype(BF16), kb, preferred_element_type=F32)
    wl8 = jnp.broadcast_to(wl, (8, L)).astype(BF16)
    n_new = gl * n + jnp.dot(wl8, kb, preferred_element_type=F32)[0:1]
    c_sc[...] = c_new
    n_sc[...] = n_new
    m_sc[...] = m_new

    @pl.when(chunk == pl.num_programs(2) - 1)
    def _():
        c_ref[0, 0] = c_new
        n_ref[0, 0] = n_new
        m_ref[0, 0] = m_new


def _mlstm_prompt(z, b_gate):
    B, S, _ = z.shape
    H, L = MLSTM_HEADS, MLSTM_KERNEL_CHUNK
    assert S % L == 0
    nc = S // L
    kblk, vblk = MLSTM_QK_COLS // MLSTM_DQK, (2 * MLSTM_QK_COLS) // MLSTM_DV
    gates = z[..., 2 * MLSTM_QK_COLS + 2 * MLSTM_V_COLS:] + b_gate
    gates = gates.reshape(B, nc, L, 2, H).transpose(3, 0, 4, 1, 2)[:, :, :, :, None, :]
    vt = z[..., 2 * MLSTM_QK_COLS:2 * MLSTM_QK_COLS + MLSTM_V_COLS].reshape(B, S, H, MLSTM_DV)
    vt = vt.transpose(0, 2, 3, 1)
    gate_spec = pl.BlockSpec((1, 1, 1, 1, L), lambda b, h, c: (b, h, c, 0, 0))
    state = lambda r, w: pl.BlockSpec((1, 1, r, w), lambda b, h, c: (b, h, 0, 0))
    hs, c, n, m = pl.pallas_call(
        _mlstm_chunk_body,
        grid=(B, H, nc),
        in_specs=[pl.BlockSpec((1, L, MLSTM_DQK), lambda b, h, c: (b, c, h)),
                  pl.BlockSpec((1, L, MLSTM_DQK), lambda b, h, c: (b, c, kblk + h)),
                  pl.BlockSpec((1, L, MLSTM_DV), lambda b, h, c: (b, c, vblk + h)),
                  pl.BlockSpec((1, L, MLSTM_DV), lambda b, h, c: (b, c, vblk + H + h)),
                  pl.BlockSpec((1, 1, MLSTM_DV, L), lambda b, h, c: (b, h, 0, c)),
                  gate_spec, gate_spec],
        out_specs=[pl.BlockSpec((1, L, MLSTM_DV), lambda b, h, c: (b, c, h)),
                   state(MLSTM_DV, MLSTM_DQK), state(1, MLSTM_DQK), state(1, 1)],
        out_shape=[jax.ShapeDtypeStruct((B, S, MLSTM_V_COLS), F32),
                   jax.ShapeDtypeStruct((B, H, MLSTM_DV, MLSTM_DQK), F32),
                   jax.ShapeDtypeStruct((B, H, 1, MLSTM_DQK), F32),
                   jax.ShapeDtypeStruct((B, H, 1, 1), F32)],
        scratch_shapes=[pltpu.VMEM((MLSTM_DV, MLSTM_DQK), F32), pltpu.VMEM((1, MLSTM_DQK), F32),
                        pltpu.VMEM((1, 1), F32)],
        compiler_params=pltpu.CompilerParams(dimension_semantics=("arbitrary",) * 3),
        name="mlstm_chunks",
    )(z, z, z, z, vt, gates[0], gates[1])
    return hs, c, n.reshape(B, H, MLSTM_DQK), m.reshape(B, H)


def _mlstm_step_body(q_ref, k_ref, v_ref, og_ref, ig_ref, fg_ref, c_ref, n_ref, m_ref,
                     h_ref, c_out, n_out, m_out):
    eye = (lax.broadcasted_iota(jnp.int32, (MLSTM_DV, MLSTM_DV), 0)
           == lax.broadcasted_iota(jnp.int32, (MLSTM_DV, MLSTM_DV), 1))
    for h in range(MLSTM_HEADS):
        c, n, m = c_ref[0, h], n_ref[0, h], m_ref[0, h]
        q = q_ref[0, h]
        k = k_ref[0, h] * (MLSTM_DQK ** -0.5)
        v = v_ref[0, h]
        i_g = ig_ref[0, h]
        inter = _log_sigmoid(fg_ref[0, h]) + m
        mt = jnp.maximum(inter, i_g)
        w = jnp.exp(i_g - mt)
        gq = jnp.exp(inter - mt)
        a = w * jnp.sum(q * k, axis=1, keepdims=True)
        cq_col = jnp.sum(c * q, axis=1, keepdims=True)
        cq_row = jnp.sum(jnp.where(eye, cq_col, 0.0), axis=0, keepdims=True)
        den = a + gq * jnp.sum(n * q, axis=1, keepdims=True)
        hc = (a * v + gq * cq_row) / jnp.maximum(jnp.abs(den), jnp.exp(-mt))
        h_ref[0, h] = hc * jax.nn.sigmoid(og_ref[0, h])
        v_col = jnp.sum(jnp.where(eye, v, 0.0), axis=1, keepdims=True)
        c_out[0, h] = gq * c + (w * v_col) * k
        n_out[0, h] = gq * n + w * k
        m_out[0, h] = mt


def _mlstm_sample(z, b_gate, c0, n0, m0):
    B, T, _ = z.shape
    assert T == 1
    H = MLSTM_HEADS
    z = z.reshape(B, -1)
    qk, hv = MLSTM_QK_COLS, MLSTM_V_COLS
    q = z[:, :qk].reshape(B, H, 1, MLSTM_DQK)
    k = z[:, qk:2 * qk].reshape(B, H, 1, MLSTM_DQK)
    v = z[:, 2 * qk:2 * qk + hv].reshape(B, H, 1, MLSTM_DV)
    og = z[:, 2 * qk + hv:2 * qk + 2 * hv].reshape(B, H, 1, MLSTM_DV)
    gates = z[:, 2 * qk + 2 * hv:] + b_gate
    ig = gates[:, :H].reshape(B, H, 1, 1)
    fg = gates[:, H:].reshape(B, H, 1, 1)
    spec = lambda r, w: pl.BlockSpec((1, H, r, w), lambda b: (b, 0, 0, 0))
    shapes = [(1, MLSTM_DV), (MLSTM_DV, MLSTM_DQK), (1, MLSTM_DQK), (1, 1)]
    hs, c, n, m = pl.pallas_call(
        _mlstm_step_body,
        grid=(B,),
        in_specs=[spec(1, MLSTM_DQK), spec(1, MLSTM_DQK), spec(1, MLSTM_DV), spec(1, MLSTM_DV),
                  spec(1, 1), spec(1, 1), spec(MLSTM_DV, MLSTM_DQK), spec(1, MLSTM_DQK), spec(1, 1)],
        out_specs=[spec(*s) for s in shapes],
        out_shape=[jax.ShapeDtypeStruct((B, H) + s, F32) for s in shapes],
        compiler_params=pltpu.CompilerParams(dimension_semantics=("arbitrary",)),
        name="mlstm_step",
    )(q, k, v, og, ig, fg, c0, n0.reshape(B, H, 1, MLSTM_DQK), m0.reshape(B, H, 1, 1))
    return hs.reshape(B, T, hv), c, n.reshape(B, H, MLSTM_DQK), m.reshape(B, H)


def _rmsnorm(x, g):
    return x * lax.rsqrt(jnp.mean(x * x, axis=-1, keepdims=True) + RMS_EPS) * g


def _rope(x, pos):
    half = x.shape[-1] // 2
    freqs = ROPE_THETA ** (-jnp.arange(half, dtype=F32) / half)
    ang = pos.astype(F32)[:, None] * freqs[None, :]
    cos = jnp.cos(ang)[None, :, None, :]
    sin = jnp.sin(ang)[None, :, None, :]
    x1, x2 = x[..., :half], x[..., half:]
    return jnp.concatenate([x1 * cos - x2 * sin, x1 * sin + x2 * cos], axis=-1)


def _alibi_slopes(n):
    return 2.0 ** (-8.0 * jnp.arange(1, n + 1, dtype=F32) / n)


def _to_blocks(a, axis):
    n = a.shape[axis] // Q_BLOCK
    a = a.reshape(a.shape[:axis] + (n, Q_BLOCK) + a.shape[axis + 1:])
    return jnp.moveaxis(a, axis, 0)


def _from_blocks(a, axis):
    a = jnp.moveaxis(a, 0, axis)
    return a.reshape(a.shape[:axis] + (-1,) + a.shape[axis + 2:])


def _attend(q, ks, vs, kposs, q_pos, scale, slopes=None, window=None):
    B, Tq, H, Dk = q.shape
    kvh = ks[0].shape[2]
    G = H // kvh
    qg = q.reshape(B, Tq, kvh, G, Dk)
    scores = []
    for k, kp in zip(ks, kposs):
        s = jnp.einsum('bqgnd,bkgd->bgnqk', qg, k).astype(F32) * scale
        dist = q_pos[:, None] - kp[None, :]
        mask = (dist >= 0) & (kp[None, :] >= 0)
        if window is not None:
            mask = mask & (dist < window)
        if slopes is not None:
            s = s - slopes.reshape(kvh, G)[None, :, :, None, None] * dist.astype(F32)
        scores.append(jnp.where(mask, s, NEG))
    p = jax.nn.softmax(jnp.concatenate(scores, axis=-1), axis=-1)
    out, off = 0.0, 0
    for v in vs:
        n = v.shape[1]
        out = out + jnp.einsum('bgnqk,bkgd->bqgnd', p[..., off:off + n], v)
        off += n
    return out.reshape(B, Tq, H, -1)


def _mla_prompt(x, g0, B, S, w_a, g_q, g_kv, w_uq, w_uk, w_uv):
    w_a_ext, w_q, w_kv = _mla_prompt_weights(w_a, w_uq, w_uk, w_uv)
    a_ext = _norm_proj(x, g0, w_a_ext)
    q, ckv, k, v, kpe, kpe_b = _mla_prompt_proj(a_ext, g_q, g_kv, w_q, w_kv, S)
    seq = lambda t: t.reshape(B, S, -1)
    o = _mla_flash(seq(q), seq(k), seq(kpe_b), seq(v))
    new_rows = jnp.concatenate([ckv, kpe[:, MLA_NOPE_DIM:MLA_NOPE_DIM + MLA_ROPE_DIM]], axis=-1)
    return o, new_rows.reshape(B, S, -1)


def _mla_sample(a, cache, j, page_table, g_q, g_kv, w_uq, w_uk, w_uv):
    B, T, _ = a.shape
    assert T == 1
    H = MLA_HEADS
    past_len = page_table.shape[1] * cache.shape[2]
    pos = past_len + jnp.arange(T, dtype=jnp.int32)
    a2 = a.reshape(B * T, -1)
    q = _norm_proj(a2[:, :MLA_Q_LORA], g_q, w_uq).reshape(B, T, H, MLA_NOPE_DIM + MLA_ROPE_DIM)
    w_kv = jnp.concatenate([w_uk.reshape(MLA_KV_LORA, -1), w_uv.reshape(MLA_KV_LORA, -1)], axis=1)
    _, ckv = _norm_proj(a2[:, MLA_Q_LORA:MLA_Q_LORA + MLA_KV_LORA], g_kv, w_kv, with_normed=True)
    kpe = _rope(a[..., MLA_Q_LORA + MLA_KV_LORA:][:, :, None, :], pos)
    q_pe = _rope(q[..., MLA_NOPE_DIM:], pos)
    new_rows = jnp.concatenate([ckv.reshape(B, T, -1), kpe[:, :, 0]], axis=-1)
    q_nope = q[:, 0, :, :MLA_NOPE_DIM].transpose(1, 0, 2)
    q_lat = _heads_matmul(q_nope, w_uk.transpose(1, 2, 0)).transpose(1, 0, 2)
    q_abs = jnp.concatenate([q_lat, q_pe[:, 0]], axis=-1) * MLA_SCALE
    o_lat = _mla_decode(q_abs, new_rows, cache, j, page_table)
    o = _heads_matmul(o_lat.transpose(1, 0, 2), w_uv.transpose(1, 0, 2))
    return o.transpose(1, 0, 2).reshape(B, T, -1), new_rows


def _mlstm_split(z, b_gate):
    B, T, _ = z.shape
    qk = MLSTM_HEADS * MLSTM_DQK
    hv = MLSTM_HEADS * MLSTM_DV
    q = z[..., :qk].reshape(B, T, MLSTM_HEADS, MLSTM_DQK)
    k = z[..., qk:2 * qk].reshape(B, T, MLSTM_HEADS, MLSTM_DQK) * (MLSTM_DQK ** -0.5)
    v = z[..., 2 * qk:2 * qk + hv].reshape(B, T, MLSTM_HEADS, MLSTM_DV)
    o = jax.nn.sigmoid(z[..., 2 * qk + hv:2 * qk + 2 * hv]).reshape(B, T, MLSTM_HEADS, MLSTM_DV)
    gt = z[..., 2 * qk + 2 * hv:] + b_gate
    ig = gt[..., :MLSTM_HEADS]
    lf = jax.nn.log_sigmoid(gt[..., MLSTM_HEADS:])
    return q, k, v, o, ig, lf


def _mlstm_chunkwise(q, k, v, ig, lf, c0, n0, m0):
    B, T, H, _ = q.shape
    L = MLSTM_CHUNK if T % MLSTM_CHUNK == 0 else T
    nc = T // L

    def split(a):
        a = a.reshape((B, nc, L) + a.shape[2:])
        return jnp.moveaxis(a, 1, 0)

    causal = jnp.tril(jnp.ones((L, L), dtype=bool))

    def step(carry, xs):
        c, n, m = carry
        qc, kc, vc, ic, fc = xs
        b = jnp.cumsum(fc, axis=1).transpose(0, 2, 1)
        it = ic.transpose(0, 2, 1)
        d = jnp.where(causal, b[:, :, :, None] - b[:, :, None, :] + it[:, :, None, :], NEG)
        inter = b + m[:, :, None]
        mt = jnp.maximum(inter, d.max(axis=-1))
        w = jnp.exp(d - mt[..., None])
        g = jnp.exp(inter - mt)
        a = w * jnp.einsum('bthd,bshd->bhts', qc, kc)
        num = jnp.einsum('bhts,bshe->bthe', a, vc) + jnp.einsum('bht,bhed,bthd->bthe', g, c, qc)
        den = a.sum(axis=-1) + g * jnp.einsum('bhd,bthd->bht', n, qc)
        hc = num / jnp.maximum(jnp.abs(den), jnp.exp(-mt)).transpose(0, 2, 1)[..., None]
        m_new = mt[:, :, -1]
        wl = jnp.exp(b[:, :, -1:] - b + it - m_new[:, :, None])
        gl = jnp.exp(b[:, :, -1] + m - m_new)
        c_new = gl[:, :, None, None] * c + jnp.einsum('bhs,bshe,bshd->bhed', wl, vc, kc)
        n_new = gl[:, :, None] * n + jnp.einsum('bhs,bshd->bhd', wl, kc)
        return (c_new, n_new, m_new), hc

    (c, n, m), hs = lax.scan(step, (c0, n0, m0), (split(q), split(k), split(v), split(ig), split(lf)))
    hs = jnp.moveaxis(hs, 0, 1).reshape(B, T, H, -1)
    return hs, c, n, m


def _mlstm_mix(z, c0, n0, m0, b_gate):
    B, T, _ = z.shape
    q, k, v, o, ig, lf = _mlstm_split(z, b_gate)
    hs, c, n, m = _mlstm_chunkwise(q, k, v, ig, lf, c0, n0, m0)
    return (hs * o).reshape(B, T, -1), c, n, m


def _nsa_split(z):
    B, T, _ = z.shape
    q = z[..., :NSA_Q_COLS].reshape(B, T, NSA_HEADS, NSA_HEAD_DIM)
    kv = z[..., NSA_Q_COLS:NSA_Q_COLS + 3 * NSA_KV_COLS].reshape(B, T, 3, NSA_KV_HEADS, 2, NSA_HEAD_DIM)
    g = jax.nn.sigmoid(z[..., NSA_Q_COLS + 3 * NSA_KV_COLS:]).reshape(B, T, NSA_HEADS, 3)
    return q, kv[:, :, 0], kv[:, :, 1], kv[:, :, 2], g


def _summarize(kv, w_cmp):
    B, T = kv.shape[:2]
    nb = T // CMP_BLOCK
    blk = kv[:, :nb * CMP_BLOCK].reshape(B, nb, CMP_BLOCK, NSA_KV_HEADS, 2, NSA_HEAD_DIM)
    return jnp.einsum('bnlgcd,clde->bngce', blk, w_cmp)


def _nsa_compressed(q, summ, q_pos, slopes):
    B, T, H, Dh = q.shape
    nbc = summ.shape[1]
    G = H // NSA_KV_HEADS
    qg = q.reshape(B, T, NSA_KV_HEADS, G, Dh)
    s = jnp.einsum('btgnd,bjgd->bgntj', qg, summ[..., 0, :]) * NSA_SCALE
    end = (jnp.arange(nbc, dtype=jnp.int32) + 1) * CMP_BLOCK - 1
    dist = q_pos[:, None] - end[None, :]
    mask = dist >= 0
    s = jnp.where(mask, s - slopes.reshape(NSA_KV_HEADS, G)[None, :, :, None, None] * dist.astype(F32), NEG)
    p = jnp.where(mask, jax.nn.softmax(s, axis=-1), 0.0)
    o = jnp.einsum('bgntj,bjgd->btgnd', p, summ[..., 1, :]).reshape(B, T, H, Dh)
    return o, p.sum(axis=2)


def _nsa_select(imp, q_pos, nb):
    nbc = imp.shape[-1]
    score = jnp.pad(imp, ((0, 0), (0, 0), (0, 0), (0, nb - nbc)))
    j = jnp.arange(nb, dtype=jnp.int32)[None, :]
    cur = (q_pos // SEL_BLOCK)[:, None]
    forced = (j == 0) | (j == cur) | (j == cur - 1)
    score = jnp.where(forced, FORCE, jnp.where(j <= cur, score, -FORCE))
    return lax.top_k(score, min(TOP_K_BLOCKS, nb))[1]


def _sel_attn(q, q_pos, kg, vg, idx, slopes):
    B, Tq, H, Dh = q.shape
    G = H // NSA_KV_HEADS
    qg = q.reshape(B, Tq, NSA_KV_HEADS, G, Dh)
    s = jnp.einsum('btgnd,bgtkld->bgntkl', qg, kg) * NSA_SCALE
    rows = idx[..., None] * SEL_BLOCK + jnp.arange(SEL_BLOCK, dtype=jnp.int32)
    dist = (q_pos[None, None, :, None, None] - rows)[:, :, None]
    sl = slopes.reshape(NSA_KV_HEADS, G)[None, :, :, None, None, None]
    s = jnp.where(dist >= 0, s - sl * dist.astype(F32), NEG)
    shp = s.shape
    p = jax.nn.softmax(s.reshape(shp[:4] + (-1,)), axis=-1).reshape(shp)
    o = jnp.einsum('bgntkl,bgtkld->btgnd', p, vg)
    return o.reshape(B, Tq, H, Dh)


def _nsa_merge(g, o_cmp, o_sel, o_win):
    o = g[..., 0:1] * o_cmp + g[..., 1:2] * o_sel + g[..., 2:3] * o_win
    B, T = o.shape[:2]
    return o.reshape(B, T, -1)


def _nsa_seq_layout(kv):
    return kv.transpose(3, 0, 2, 1, 4).astype(BF16)


def _nsa_prompt(z, w_bd):
    B, S, _ = z.shape
    nb = S // CMP_BLOCK
    q = (z[..., :NSA_Q_COLS] * (NSA_SCALE * LOG2E)).astype(BF16)
    q = q.reshape(B, S, NSA_HEADS, NSA_HEAD_DIM).transpose(0, 2, 1, 3)
    kv = z[..., NSA_Q_COLS:NSA_Q_COLS + 3 * NSA_KV_COLS].reshape(B, S, 3, NSA_KV_HEADS, 2, NSA_HEAD_DIM)
    kv_c, kv_s, kv_w = kv[:, :, 0], kv[:, :, 1], kv[:, :, 2]
    summ = _summarize_blocks(kv_c.reshape(B * nb, CMP_BLOCK * NSA_KV_COLS), w_bd)
    summ = summ.reshape(B, nb, NSA_KV_HEADS, 2, NSA_HEAD_DIM).transpose(3, 0, 2, 1, 4).astype(BF16)
    gate = z[..., NSA_Q_COLS + 3 * NSA_KV_COLS:].reshape(B, S, NSA_KV_HEADS, 3 * NSA_GROUP).transpose(0, 2, 1, 3)
    y = _nsa_prompt_attend(q, summ, _nsa_seq_layout(kv_s), _nsa_seq_layout(kv_w), gate)
    return y, kv_c, kv_s, kv_w[:, -min(WINDOW, S):]


def _nsa_sample(z, cache_cmp, cache_sel, win_state, j, page_table, w_paged):
    B, T, _ = z.shape
    assert T == 1 and T < CMP_BLOCK
    G, Dh = NSA_KV_HEADS, NSA_HEAD_DIM
    assert cache_cmp.shape[2] == PAGE_SIZE
    n_pages = page_table.shape[1]
    q, kv_c, kv_s, kv_w, gate = _nsa_split(z)
    summ = _summarize_pool(cache_cmp[j], w_paged)[page_table]
    summ = summ.transpose(3, 0, 2, 1, 4, 5).reshape(2, B, G, n_pages * SUMM_PAGE_BLOCKS, Dh).astype(BF16)
    qg = q.reshape(B, G, NSA_GROUP, Dh) * NSA_SCALE
    o_cmp, idx = _nsa_cmp_sample(qg, summ)
    o_sel = _nsa_sel_sample(qg, kv_s.reshape(B, G, 2, Dh), cache_sel, j, page_table, idx)
    o_win, new_win = _nsa_win_sample(qg, kv_w.reshape(B, G, 2, Dh), win_state)
    heads = lambda o: o.reshape(B, T, NSA_HEADS, Dh)
    y = _nsa_merge(gate, heads(o_cmp), heads(o_sel), heads(o_win))
    new_win = new_win.reshape(B, G, 2, Dh, new_win.shape[-1]).transpose(0, 4, 1, 2, 3)
    return y, kv_c, kv_s, new_win


def kernel(x_prompt, x_sample, cache_mla_kv, state_mlstm_c, state_mlstm_n, state_mlstm_m, cache_nsa_cmp,
           cache_nsa_sel, state_nsa_win, page_table, norm_g, final_norm_g, mla_w_a, mla_g_q, mla_g_kv,
           mla_w_uq, mla_w_uk, mla_w_uv, mla_w_o, mlstm_w_in, mlstm_b_gate, mlstm_w_out, nsa_w_in,
           nsa_w_cmp, nsa_w_out, mlp_w1, mlp_w2):
    B, S, D = x_prompt.shape
    Bs, Ts, _ = x_sample.shape
    xp = x_prompt.reshape(B * S, D)
    xs = x_sample.reshape(Bs * Ts, D)
    mla_p, mla_s = [], []
    mc_p, mn_p, mm_p, mc_s, mn_s, mm_s = [], [], [], [], [], []
    cmp_p, cmp_s, sel_p, sel_s, win_p, win_s = [], [], [], [], [], []
    for i in range(DEPTH):
        j = i // N_MIXERS
        g0 = norm_g[i, 0]
        if i % N_MIXERS == 0:
            as_ = _norm_proj(xs, g0, mla_w_a[j]).reshape(Bs, Ts, -1)
            w = (mla_g_q[j], mla_g_kv[j], mla_w_uq[j], mla_w_uk[j], mla_w_uv[j])
            op, rp = _mla_prompt(xp, g0, B, S, mla_w_a[j], *w)
            os_, rs = _mla_sample(as_, cache_mla_kv, j, page_table, *w)
            mla_p.append(rp)
            mla_s.append(rs)
            w_out = mla_w_o[j]
        elif i % N_MIXERS == 1:
            zp = _norm_proj(xp, g0, mlstm_w_in[j]).reshape(B, S, -1)
            zs = _norm_proj(xs, g0, mlstm_w_in[j]).reshape(Bs, Ts, -1)
            op, cp, nst_p, mp = _mlstm_prompt(zp, mlstm_b_gate[j])
            os_, cs, nst_s, ms = _mlstm_sample(zs, mlstm_b_gate[j], state_mlstm_c[j], state_mlstm_n[j],
                                               state_mlstm_m[j])
            mc_p.append(cp)
            mn_p.append(nst_p)
            mm_p.append(mp)
            mc_s.append(cs)
            mn_s.append(nst_s)
            mm_s.append(ms)
            w_out = mlstm_w_out[j]
        else:
            zp = _norm_proj(xp, g0, nsa_w_in[j]).reshape(B, S, -1)
            zs = _norm_proj(xs, g0, nsa_w_in[j]).reshape(Bs, Ts, -1)
            op, kcp, ksp, kwp = _nsa_prompt(zp, _summ_weights(nsa_w_cmp[j]))
            os_, kcs, kss, kws = _nsa_sample(zs, cache_nsa_cmp, cache_nsa_sel, state_nsa_win[j], j,
                                             page_table, _summ_weights_paged(nsa_w_cmp[j]))
            cmp_p.append(kcp)
            cmp_s.append(kcs)
            sel_p.append(ksp)
            sel_s.append(kss)
            win_p.append(kwp)
            win_s.append(kws)
            w_out = nsa_w_out[j]
        xp = _proj_res(op.reshape(B * S, -1), w_out, xp)
        xs = _proj_res(os_.reshape(Bs * Ts, -1), w_out, xs)
        xp = _mlp_res(xp, norm_g[i, 1], mlp_w1[i], mlp_w2[i])
        xs = _mlp_res(xs, norm_g[i, 1], mlp_w1[i], mlp_w2[i])
    y_prompt = _final_norm(xp, final_norm_g).reshape(B, S, D)
    y_sample = _final_norm(xs, final_norm_g).reshape(Bs, Ts, D)
    return (y_prompt, y_sample,
            jnp.stack(mla_p), jnp.stack(mla_s),
            jnp.stack(mc_p), jnp.stack(mn_p), jnp.stack(mm_p),
            jnp.stack(mc_s), jnp.stack(mn_s), jnp.stack(mm_s),
            jnp.stack(cmp_p), jnp.stack(cmp_s),
            jnp.stack(sel_p), jnp.stack(sel_s),
            jnp.stack(win_p), jnp.stack(win_s))
```

```python
import functools

import jax
import jax.numpy as jnp
from jax import lax
from jax.experimental import pallas as pl
from jax.experimental.pallas import tpu as pltpu

F32 = jnp.float32
BF16 = jnp.bfloat16

D_MODEL = 1024
DEPTH = 4
N_MIXERS = 3
PAGE_SIZE = 128

MLA_HEADS = 16
MLA_NOPE_DIM = 64
MLA_ROPE_DIM = 32
MLA_V_DIM = 64
MLA_Q_LORA = 384
MLA_KV_LORA = 256
MLA_SCALE = (MLA_NOPE_DIM + MLA_ROPE_DIM) ** -0.5
ROPE_THETA = 10000.0

MLSTM_HEADS = 4
MLSTM_DQK = 128
MLSTM_DV = 256

NSA_HEADS = 16
NSA_KV_HEADS = 4
NSA_HEAD_DIM = 64
CMP_BLOCK = 64
SEL_BLOCK = 64
TOP_K_BLOCKS = 16
WINDOW = 512
NSA_Q_COLS = NSA_HEADS * NSA_HEAD_DIM
NSA_KV_COLS = NSA_KV_HEADS * 2 * NSA_HEAD_DIM
NSA_SCALE = NSA_HEAD_DIM ** -0.5

D_FF = 4 * D_MODEL
RMS_EPS = 1e-6
NEG = -1e30
FORCE = 1e4

VMEM_LIMIT_BYTES = 56 * 1024 * 1024
FF_CHUNK = 512


def _row_tile(m):
    for t in (512, 256, 128):
        if m % t == 0:
            return t
    return m


def _rms_rows(x, g):
    return x * lax.rsqrt(jnp.mean(x * x, axis=-1, keepdims=True) + RMS_EPS) * g


def _norm_proj_body(x_ref, g_ref, w_ref, o_ref, *h_ref):
    h = _rms_rows(x_ref[...], g_ref[...])
    o_ref[...] = jnp.dot(h.astype(BF16), w_ref[...], preferred_element_type=F32)
    if h_ref:
        h_ref[0][...] = h


def _norm_proj(x, g, w, with_normed=False):
    m, d = x.shape
    n = w.shape[1]
    tm = _row_tile(m)
    out_specs = [pl.BlockSpec((tm, n), lambda i: (i, 0))]
    out_shape = [jax.ShapeDtypeStruct((m, n), F32)]
    if with_normed:
        out_specs.append(pl.BlockSpec((tm, d), lambda i: (i, 0)))
        out_shape.append(jax.ShapeDtypeStruct((m, d), F32))
    out = pl.pallas_call(
        _norm_proj_body,
        grid=(m // tm,),
        in_specs=[pl.BlockSpec((tm, d), lambda i: (i, 0)),
                  pl.BlockSpec((1, d), lambda i: (0, 0)),
                  pl.BlockSpec((d, n), lambda i: (0, 0))],
        out_specs=out_specs,
        out_shape=out_shape,
        compiler_params=pltpu.CompilerParams(dimension_semantics=("arbitrary",),
                                             vmem_limit_bytes=VMEM_LIMIT_BYTES),
        name="norm_proj",
    )(x, g.reshape(1, d), w.astype(BF16))
    return out if with_normed else out[0]


def _proj_res_body(a_ref, w_ref, r_ref, o_ref):
    o_ref[...] = r_ref[...] + jnp.dot(a_ref[...].astype(BF16), w_ref[...], preferred_element_type=F32)


def _proj_res(a, w, res):
    m, k = a.shape
    d = w.shape[1]
    tm = _row_tile(m)
    return pl.pallas_call(
        _proj_res_body,
        grid=(m // tm,),
        in_specs=[pl.BlockSpec((tm, k), lambda i: (i, 0)),
                  pl.BlockSpec((k, d), lambda i: (0, 0)),
                  pl.BlockSpec((tm, d), lambda i: (i, 0))],
        out_specs=pl.BlockSpec((tm, d), lambda i: (i, 0)),
        out_shape=jax.ShapeDtypeStruct((m, d), F32),
        compiler_params=pltpu.CompilerParams(dimension_semantics=("arbitrary",),
                                             vmem_limit_bytes=VMEM_LIMIT_BYTES),
        name="proj_res",
    )(a, w.astype(BF16), res)


def _mlp_body(x_ref, g_ref, w1_ref, w2_ref, o_ref):
    x = x_ref[...]
    h = _rms_rows(x, g_ref[...]).astype(BF16)
    acc = x
    for c in range(D_FF // FF_CHUNK):
        a = jnp.dot(h, w1_ref[:, c * FF_CHUNK:(c + 1) * FF_CHUNK], preferred_element_type=F32)
        a = jnp.maximum(a, 0.0)
        acc = acc + jnp.dot((a * a).astype(BF16), w2_ref[c * FF_CHUNK:(c + 1) * FF_CHUNK, :],
                            preferred_element_type=F32)
    o_ref[...] = acc


def _mlp_res(x, g, w1, w2):
    m, d = x.shape
    tm = _row_tile(m)
    return pl.pallas_call(
        _mlp_body,
        grid=(m // tm,),
        in_specs=[pl.BlockSpec((tm, d), lambda i: (i, 0)),
                  pl.BlockSpec((1, d), lambda i: (0, 0)),
                  pl.BlockSpec((d, D_FF), lambda i: (0, 0)),
                  pl.BlockSpec((D_FF, d), lambda i: (0, 0))],
        out_specs=pl.BlockSpec((tm, d), lambda i: (i, 0)),
        out_shape=jax.ShapeDtypeStruct((m, d), F32),
        compiler_params=pltpu.CompilerParams(dimension_semantics=("arbitrary",),
                                             vmem_limit_bytes=VMEM_LIMIT_BYTES),
        name="mlp_res",
    )(x, g.reshape(1, d), w1.astype(BF16), w2.astype(BF16))


def _final_norm_body(x_ref, g_ref, o_ref):
    o_ref[...] = _rms_rows(x_ref[...], g_ref[...])


def _final_norm(x, g):
    m, d = x.shape
    tm = _row_tile(m)
    return pl.pallas_call(
        _final_norm_body,
        grid=(m // tm,),
        in_specs=[pl.BlockSpec((tm, d), lambda i: (i, 0)), pl.BlockSpec((1, d), lambda i: (0, 0))],
        out_specs=pl.BlockSpec((tm, d), lambda i: (i, 0)),
        out_shape=jax.ShapeDtypeStruct((m, d), F32),
        compiler_params=pltpu.CompilerParams(dimension_semantics=("arbitrary",)),
        name="final_norm",
    )(x, g.reshape(1, d))


def _dot_nt(a, b):
    return lax.dot_general(a, b, (((1,), (1,)), ((), ())), preferred_element_type=F32)


SUMM_L_PER_STEP = 8
SUMM_COLS = NSA_KV_COLS
SUMM_HALF = SUMM_COLS // 2


def _summ_weights(w_cmp):
    wk = jnp.stack([w_cmp[0], w_cmp[1], w_cmp[0], w_cmp[1]], axis=0)
    bd = jnp.einsum('kj,klde->lkdje', jnp.eye(4, dtype=F32), wk)
    return bd.reshape(CMP_BLOCK, SUMM_HALF, SUMM_HALF).astype(BF16)


def _summarize_body(x_ref, w_ref, o_ref):
    @pl.when(pl.program_id(1) == 0)
    def _():
        o_ref[...] = jnp.zeros_like(o_ref)

    lo = o_ref[:, :SUMM_HALF]
    hi = o_ref[:, SUMM_HALF:]
    for li in range(SUMM_L_PER_STEP):
        x = x_ref[:, li * SUMM_COLS:(li + 1) * SUMM_COLS].astype(BF16)
        w = w_ref[li]
        lo = lo + jnp.dot(x[:, :SUMM_HALF], w, preferred_element_type=F32)
        hi = hi + jnp.dot(x[:, SUMM_HALF:], w, preferred_element_type=F32)
    o_ref[:, :SUMM_HALF] = lo
    o_ref[:, SUMM_HALF:] = hi


def _summarize_blocks(x2d, w_bd):
    nb = x2d.shape[0]
    p = 512 if nb % 512 == 0 else nb
    step_cols = SUMM_L_PER_STEP * SUMM_COLS
    return pl.pallas_call(
        _summarize_body,
        grid=(nb // p, CMP_BLOCK // SUMM_L_PER_STEP),
        in_specs=[pl.BlockSpec((p, step_cols), lambda i, l: (i, l)),
                  pl.BlockSpec((SUMM_L_PER_STEP, SUMM_HALF, SUMM_HALF), lambda i, l: (l, 0, 0))],
        out_specs=pl.BlockSpec((p, SUMM_COLS), lambda i, l: (i, 0)),
        out_shape=jax.ShapeDtypeStruct((nb, SUMM_COLS), F32),
        compiler_params=pltpu.CompilerParams(dimension_semantics=("arbitrary", "arbitrary"),
                                             vmem_limit_bytes=VMEM_LIMIT_BYTES),
        name="nsa_summarize",
    )(x2d, w_bd)


SUMM_D_PER_STEP = 8
SUMM_PAGE_BLOCKS = PAGE_SIZE // CMP_BLOCK
SUMM_PAGE_OUT = 2 * SUMM_PAGE_BLOCKS * NSA_HEAD_DIM


def _summ_weights_paged(w_cmp):
    n = 2 * SUMM_PAGE_BLOCKS
    wk = jnp.stack([w_cmp[c] for c in range(2) for _ in range(SUMM_PAGE_BLOCKS)], axis=0)
    bd = jnp.einsum('kj,klde->dklje', jnp.eye(n, dtype=F32), wk)
    return bd.reshape(NSA_HEAD_DIM, 2 * PAGE_SIZE, SUMM_PAGE_OUT).astype(BF16)


def _rows_by_channel(x):
    p, n, lanes = x.shape
    assert n == 8 and p % 8 == 0
    x4 = x.reshape(p // 8, 8, 8, lanes)
    parts = [x4[:, i] for i in range(8)]
    sub = lax.broadcasted_iota(jnp.int32, (1, 8, lanes), 1)
    for s in (4, 2, 1):
        low = (sub & s) == 0
        nxt = list(parts)
        for i in range(8):
            if i & s == 0:
                a, b = parts[i], parts[i + s]
                nxt[i] = jnp.where(low, a, pltpu.roll(b, s, 1))
                nxt[i + s] = jnp.where(low, pltpu.roll(a, 8 - s, 1), b)
        parts = nxt
    return [t.reshape(p, lanes) for t in parts]


def _summarize_pool_body(k_ref, v_ref, w_ref, o_ref):
    dg = pl.program_id(2)

    @pl.when(dg == 0)
    def _():
        o_ref[...] = jnp.zeros_like(o_ref)

    acc = o_ref[...]
    ks = _rows_by_channel(k_ref[...])
    vs = _rows_by_channel(v_ref[...])
    for dd in range(SUMM_D_PER_STEP):
        lhs = jnp.concatenate([ks[dd], vs[dd]], axis=1).astype(BF16)
        acc = acc + jnp.dot(lhs, w_ref[dg * SUMM_D_PER_STEP + dd], preferred_element_type=F32)
    o_ref[...] = acc


def _summarize_pool(cache, w_paged):
    pool = cache.shape[0]
    cm = _nsa_channel_major(cache)
    p = 512 if pool % 512 == 0 else pool
    d_steps = NSA_HEAD_DIM // SUMM_D_PER_STEP
    rows_per_group = NSA_KV_PAIR // SUMM_D_PER_STEP
    out = pl.pallas_call(
        _summarize_pool_body,
        grid=(pool // p, NSA_KV_HEADS, d_steps),
        in_specs=[pl.BlockSpec((p, SUMM_D_PER_STEP, PAGE_SIZE), lambda i, g, d: (i, g * rows_per_group + d, 0)),
                  pl.BlockSpec((p, SUMM_D_PER_STEP, PAGE_SIZE),
                               lambda i, g, d: (i, g * rows_per_group + d_steps + d, 0)),
                  pl.BlockSpec((NSA_HEAD_DIM, 2 * PAGE_SIZE, SUMM_PAGE_OUT), lambda i, g, d: (0, 0, 0))],
        out_specs=pl.BlockSpec((p, SUMM_PAGE_OUT), lambda i, g, d: (i, g)),
        out_shape=jax.ShapeDtypeStruct((pool, NSA_KV_HEADS * SUMM_PAGE_OUT), F32),
        compiler_params=pltpu.CompilerParams(dimension_semantics=("arbitrary",) * 3,
                                             vmem_limit_bytes=VMEM_LIMIT_BYTES),
        name="nsa_summarize_pool",
    )(cm, cm, w_paged)
    return out.reshape(pool, NSA_KV_HEADS, 2, SUMM_PAGE_BLOCKS, NSA_HEAD_DIM)


NSA_TQ = 128
NSA_TK = 512
NSA_GROUP = NSA_HEADS // NSA_KV_HEADS
NSA_BAND = WINDOW + NSA_TQ


def _softmax_rows(s, valid):
    m = jnp.max(s, axis=1, keepdims=True)
    e = jnp.where(valid, jnp.exp(s - m), 0.0)
    l = jnp.sum(e, axis=1, keepdims=True)
    return e / jnp.where(l > 0.0, l, 1.0)


LOG2E = 1.4426950408889634
MASK_BIG = 1e30
M_INIT = -0.5e30


def _nsa_prompt_body(q_ref, ksum_ref, vsum_ref, ks_ref, vs_ref, kw_ref, vw_ref, gate_ref, o_ref):
    g = pl.program_id(1)
    q0 = pl.program_id(2) * NSA_TQ
    rows = NSA_GROUP * NSA_TQ
    nblk = ksum_ref.shape[2]
    dh = NSA_HEAD_DIM
    q = q_ref[0].reshape(rows, dh)

    row = lax.broadcasted_iota(jnp.int32, (rows, 1), 0)
    tok_in_tile = row & (NSA_TQ - 1)
    qpos = q0 + tok_in_tile
    head = g * NSA_GROUP + (row >> (NSA_TQ.bit_length() - 1))
    slope = jnp.exp((head + 1).astype(F32) * (-8.0 / NSA_HEADS * 0.6931471805599453)) * LOG2E

    s = _dot_nt(q, ksum_ref[0, 0])
    blk_end = (lax.broadcasted_iota(jnp.int32, (1, nblk), 1) + 1) * CMP_BLOCK - 1
    dist = qpos - blk_end
    vis = dist >= 0
    s = jnp.where(vis, s - slope * dist.astype(F32), NEG)
    e = jnp.where(vis, jnp.exp2(s - jnp.max(s, axis=1, keepdims=True)), 0.0)
    l = jnp.sum(e, axis=1, keepdims=True)
    p = e / jnp.where(l > 0.0, l, 1.0)
    o_cmp = jnp.dot(p.astype(BF16), vsum_ref[0, 0], preferred_element_type=F32)
    imp = p[0:NSA_TQ]
    for h in range(1, NSA_GROUP):
        imp = imp + p[h * NSA_TQ:(h + 1) * NSA_TQ]

    imp_t = jnp.concatenate([imp, jnp.zeros((NSA_TQ, NSA_TQ - nblk), F32)], axis=1).T[:nblk]
    cur = (q0 + lax.broadcasted_iota(jnp.int32, (1, NSA_TQ), 1)) >> 6
    jj = lax.broadcasted_iota(jnp.int32, (nblk, NSA_TQ), 0)
    forced = jnp.where(jj == 0, 1, jnp.where(jj == cur, 1, jnp.where(jj == cur - 1, 1, 0)))
    score = jnp.where(forced > 0, FORCE, jnp.where(jj <= cur, imp_t, -FORCE))
    rank = jnp.zeros((nblk, NSA_TQ), jnp.int32)
    for i in range(nblk):
        ci = score[i:i + 1, :]
        rank = rank + jnp.where(ci > score, 1, jnp.where(ci == score, jnp.where(jj > i, 1, 0), 0))
    unpicked_t = jnp.where(rank < min(TOP_K_BLOCKS, nblk), 0.0, -1.0)
    unpicked = jnp.concatenate([unpicked_t, jnp.zeros((NSA_TQ - nblk, NSA_TQ), F32)], axis=0).T[:, :dh]
    qs = jnp.concatenate([q, jnp.concatenate([unpicked.astype(BF16)] * NSA_GROUP, axis=0)], axis=1)

    def sel_tile(kt, carry, diagonal):
        m, acc = carry
        k0 = pl.multiple_of(kt * NSA_TK, NSA_TK)
        rel = k0 - q0 + lax.broadcasted_iota(jnp.int32, (1, NSA_TK), 1)
        s = _dot_nt(qs, ks_ref[0, 0, pl.ds(k0, NSA_TK), :]) + slope * rel.astype(F32)
        if diagonal:
            s = jnp.where(rel <= tok_in_tile, s, NEG)
        m_new = jnp.maximum(m, jnp.max(s, axis=1, keepdims=True))
        e = jnp.exp2(s - m_new).astype(BF16)
        acc = jnp.exp2(m - m_new) * acc + jnp.dot(e, vs_ref[0, 0, pl.ds(k0, NSA_TK), :],
                                                  preferred_element_type=F32)
        return m_new, acc

    n_kt = (q0 + NSA_TQ + NSA_TK - 1) // NSA_TK
    init = (jnp.full((rows, 1), M_INIT, F32), jnp.zeros((rows, 2 * dh), F32))
    carry = lax.fori_loop(0, n_kt - 1, functools.partial(sel_tile, diagonal=False), init)
    _, acc = sel_tile(n_kt - 1, carry, True)
    l = acc[:, dh:dh + 1]
    o_sel = acc[:, :dh] / jnp.where(l > 0.0, l, 1.0)

    w0 = pl.multiple_of(jnp.maximum(q0 - WINDOW, 0), NSA_TQ)
    dist = qpos - (w0 + lax.broadcasted_iota(jnp.int32, (1, NSA_BAND), 1))
    valid = jnp.where(dist >= 0, jnp.where(dist < WINDOW, 1, 0), 0) > 0
    s = jnp.where(valid, _dot_nt(q, kw_ref[0, 0, pl.ds(w0, NSA_BAND), :]) - slope * dist.astype(F32), NEG)
    e = jnp.exp2(s - jnp.max(s, axis=1, keepdims=True))
    o_win = (jnp.dot(e.astype(BF16), vw_ref[0, 0, pl.ds(w0, NSA_BAND), :], preferred_element_type=F32)
             / jnp.sum(e, axis=1, keepdims=True))

    gate = jax.nn.sigmoid(gate_ref[0, 0])
    outs = []
    for h in range(NSA_GROUP):
        r = slice(h * NSA_TQ, (h + 1) * NSA_TQ)
        outs.append(gate[:, 3 * h:3 * h + 1] * o_cmp[r] + gate[:, 3 * h + 1:3 * h + 2] * o_sel[r]
                    + gate[:, 3 * h + 2:3 * h + 3] * o_win[r])
    o_ref[0] = jnp.concatenate(outs, axis=-1)


def _nsa_prompt_attend(q, summ, kv_s, kv_w, gate):
    b, h, s, dh = q.shape
    g = NSA_KV_HEADS
    nblk = summ.shape[3]
    assert s % NSA_TK == 0 and s >= NSA_BAND and nblk <= dh
    blk = jnp.arange(s, dtype=jnp.int32)[:, None] // SEL_BLOCK
    onehot = jnp.where(blk == jnp.arange(dh, dtype=jnp.int32)[None, :], MASK_BIG, 0.0).astype(BF16)
    ks = jnp.concatenate([kv_s[0], jnp.broadcast_to(onehot, (b, g, s, dh))], axis=-1)
    ones = jnp.zeros((s, dh), BF16).at[:, 0].set(1.0)
    vs = jnp.concatenate([kv_s[1], jnp.broadcast_to(ones, (b, g, s, dh))], axis=-1)
    seq_spec = lambda w: pl.BlockSpec((1, 1, s, w), lambda bi, gi, qi: (bi, gi, 0, 0))
    sum_spec = pl.BlockSpec((1, 1, nblk, dh), lambda bi, gi, qi: (bi, gi, 0, 0))
    return pl.pallas_call(
        _nsa_prompt_body,
        grid=(b, g, s // NSA_TQ),
        in_specs=[pl.BlockSpec((1, NSA_GROUP, NSA_TQ, dh), lambda bi, gi, qi: (bi, gi, qi, 0)),
                  sum_spec, sum_spec, seq_spec(2 * dh), seq_spec(2 * dh), seq_spec(dh), seq_spec(dh),
                  pl.BlockSpec((1, 1, NSA_TQ, 3 * NSA_GROUP), lambda bi, gi, qi: (bi, gi, qi, 0))],
        out_specs=pl.BlockSpec((1, NSA_TQ, NSA_GROUP * dh), lambda bi, gi, qi: (bi, qi, gi)),
        out_shape=jax.ShapeDtypeStruct((b, s, h * dh), F32),
        compiler_params=pltpu.CompilerParams(dimension_semantics=("arbitrary",) * 3,
                                             vmem_limit_bytes=VMEM_LIMIT_BYTES),
        name="nsa_prompt_attend",
    )(q, summ[0], summ[1], ks, vs, kv_w[0], kv_w[1], gate)


def _heads_matmul_body(x_ref, w_ref, o_ref):
    o_ref[0] = jnp.dot(x_ref[0].astype(BF16), w_ref[0], preferred_element_type=F32)


def _heads_matmul(x, w):
    h, m, k = x.shape
    n = w.shape[2]
    return pl.pallas_call(
        _heads_matmul_body,
        grid=(h,),
        in_specs=[pl.BlockSpec((1, m, k), lambda i: (i, 0, 0)), pl.BlockSpec((1, k, n), lambda i: (i, 0, 0))],
        out_specs=pl.BlockSpec((1, m, n), lambda i: (i, 0, 0)),
        out_shape=jax.ShapeDtypeStruct((h, m, n), F32),
        compiler_params=pltpu.CompilerParams(dimension_semantics=("arbitrary",)),
        name="heads_matmul",
    )(x, w.astype(BF16))


MLA_TILE = 512
MLA_QK_PAD = 128
MLA_HEAD_PAIR = 2


def _mla_flash_body(q_ref, k_ref, kpe_ref, v_ref, o_ref):
    qi = pl.program_id(2)
    t = MLA_TILE
    w = MLA_QK_PAD
    causal = lax.broadcasted_iota(jnp.int32, (t, t), 0) >= lax.broadcasted_iota(jnp.int32, (t, t), 1)
    qs = [q_ref[0, :, hh * w:(hh + 1) * w] for hh in range(MLA_HEAD_PAIR)]

    def step(kt, carry, diagonal):
        k0 = pl.multiple_of(kt * t, t)
        kpe = kpe_ref[0, pl.ds(k0, t), :]
        out = []
        for hh in range(MLA_HEAD_PAIR):
            m, acc = carry[hh]
            s = _dot_nt(qs[hh], k_ref[0, pl.ds(k0, t), hh * w:(hh + 1) * w] + kpe)
            if diagonal:
                s = jnp.where(causal, s, NEG)
            m_new = jnp.maximum(m, jnp.max(s, axis=1, keepdims=True))
            e = jnp.exp2(s - m_new).astype(BF16)
            acc = jnp.exp2(m - m_new) * acc + jnp.dot(e, v_ref[0, pl.ds(k0, t), hh * w:(hh + 1) * w],
                                                      preferred_element_type=F32)
            out.append((m_new, acc))
        return tuple(out)

    init = tuple((jnp.full((t, 1), M_INIT, F32), jnp.zeros((t, MLA_QK_PAD), F32)) for _ in range(MLA_HEAD_PAIR))
    carry = lax.fori_loop(0, qi, functools.partial(step, diagonal=False), init)
    carry = step(qi, carry, True)
    o_ref[0] = jnp.concatenate([acc[:, :MLA_V_DIM] / acc[:, MLA_V_DIM:MLA_V_DIM + 1] for _, acc in carry],
                               axis=-1)


def _mla_flash(q, k, kpe, v):
    b, s, hw = q.shape
    t = MLA_TILE
    pair = MLA_HEAD_PAIR * MLA_QK_PAD
    assert s % t == 0 and hw % pair == 0
    seq = lambda bi, hi, qi: (bi, 0, hi)
    return pl.pallas_call(
        _mla_flash_body,
        grid=(b, hw // pair, s // t),
        in_specs=[pl.BlockSpec((1, t, pair), lambda bi, hi, qi: (bi, qi, hi)),
                  pl.BlockSpec((1, s, pair), seq),
                  pl.BlockSpec((1, s, MLA_QK_PAD), lambda bi, hi, qi: (bi, 0, 0)),
                  pl.BlockSpec((1, s, pair), seq)],
        out_specs=pl.BlockSpec((1, t, MLA_HEAD_PAIR * MLA_V_DIM), lambda bi, hi, qi: (bi, qi, hi)),
        out_shape=jax.ShapeDtypeStruct((b, s, (hw // MLA_QK_PAD) * MLA_V_DIM), F32),
        compiler_params=pltpu.CompilerParams(dimension_semantics=("arbitrary",) * 3,
                                             vmem_limit_bytes=VMEM_LIMIT_BYTES),
        name="mla_flash",
    )(q, k, kpe, v)


def _rot_cols(w):
    half = w.shape[-1] // 2
    return jnp.concatenate([-w[..., half:], w[..., :half]], axis=-1)


def _mla_prompt_weights(w_a, w_uq, w_uk, w_uv):
    H, N, R, W = MLA_HEADS, MLA_NOPE_DIM, MLA_ROPE_DIM, MLA_QK_PAD
    d = w_a.shape[0]
    w_kpe = w_a[:, MLA_Q_LORA + MLA_KV_LORA:]
    lanes = lambda w: jnp.concatenate([jnp.zeros((d, N), F32), w, jnp.zeros((d, W - N - R), F32)], axis=1)
    w_a_ext = jnp.concatenate([w_a[:, :MLA_Q_LORA], jnp.zeros((d, W), F32),
                               w_a[:, MLA_Q_LORA:MLA_Q_LORA + MLA_KV_LORA], lanes(w_kpe), lanes(_rot_cols(w_kpe))],
                              axis=1)
    wq = w_uq.reshape(MLA_Q_LORA, H, N + R)
    zq = jnp.zeros((MLA_Q_LORA, H, W - N - R), F32)
    plain = jnp.concatenate([wq, zq], axis=-1)
    rot = jnp.concatenate([jnp.zeros((MLA_Q_LORA, H, N), F32), _rot_cols(wq[..., N:]), zq], axis=-1)
    w_q = jnp.concatenate([plain.reshape(MLA_Q_LORA, H * W), rot.reshape(MLA_Q_LORA, H * W)], axis=1)
    zk = jnp.zeros((MLA_KV_LORA, H, W - N), F32)
    zv = jnp.zeros((MLA_KV_LORA, H, W - MLA_V_DIM), F32)
    w_kv = jnp.concatenate([jnp.concatenate([w_uk, zk], axis=-1).reshape(MLA_KV_LORA, H * W),
                            jnp.concatenate([w_uv, zv], axis=-1).reshape(MLA_KV_LORA, H * W)], axis=1)
    return w_a_ext, w_q, w_kv


def _rope_tables(s):
    half = MLA_ROPE_DIM // 2
    freqs = ROPE_THETA ** (-jnp.arange(half, dtype=F32) / half)
    ang = jnp.arange(s, dtype=F32)[:, None] * freqs[None, :]
    pad = jnp.zeros((s, MLA_QK_PAD - MLA_NOPE_DIM - MLA_ROPE_DIM), F32)
    cos = jnp.concatenate([jnp.ones((s, MLA_NOPE_DIM), F32), jnp.cos(ang), jnp.cos(ang), pad], axis=1)
    sin = jnp.concatenate([jnp.zeros((s, MLA_NOPE_DIM), F32), jnp.sin(ang), jnp.sin(ang), pad], axis=1)
    return cos, sin


def _mla_q_body(x_ref, g_ref, w_ref, cos_ref, sin_ref, o_ref):
    h = _rms_rows(x_ref[...], g_ref[...]).astype(BF16)
    z = jnp.dot(h, w_ref[...], preferred_element_type=F32)
    cos = cos_ref[...] * (MLA_SCALE * LOG2E)
    sin = sin_ref[...] * (MLA_SCALE * LOG2E)
    hw = o_ref.shape[1]
    for c in range(0, hw, MLA_QK_PAD):
        o_ref[:, c:c + MLA_QK_PAD] = (z[:, c:c + MLA_QK_PAD] * cos
                                      + z[:, hw + c:hw + c + MLA_QK_PAD] * sin).astype(BF16)


def _mla_kv_body(x_ref, g_ref, w_ref, ka_ref, kb_ref, cos_ref, sin_ref, ckv_ref, k_ref, v_ref, kpe_ref, kpeb_ref):
    ckv = _rms_rows(x_ref[...], g_ref[...])
    ckv_ref[...] = ckv
    z = jnp.dot(ckv.astype(BF16), w_ref[...], preferred_element_type=F32)
    hw = k_ref.shape[1]
    k_ref[...] = z[:, :hw].astype(BF16)
    ones_col = jnp.where(lax.broadcasted_iota(jnp.int32, (1, MLA_QK_PAD), 1) == MLA_V_DIM, 1.0, 0.0)
    for c in range(0, hw, MLA_QK_PAD):
        v_ref[:, c:c + MLA_QK_PAD] = (z[:, hw + c:hw + c + MLA_QK_PAD] + ones_col).astype(BF16)
    kpe = ka_ref[...] * cos_ref[...] + kb_ref[...] * sin_ref[...]
    kpe_ref[...] = kpe
    kpeb_ref[...] = kpe.astype(BF16)


def _mla_prompt_proj(a_ext, g_q, g_kv, w_q, w_kv, s):
    m = a_ext.shape[0]
    tm = _row_tile(m)
    assert s % tm == 0
    W = MLA_QK_PAD
    hw = MLA_HEADS * W
    cos, sin = _rope_tables(s)
    n_pos = s // tm
    row = lambda width, blk: pl.BlockSpec((tm, width), lambda i: (i, blk))
    table = pl.BlockSpec((tm, W), lambda i: (i % n_pos, 0))
    full = lambda r, c: pl.BlockSpec((r, c), lambda i: (0, 0))
    params = pltpu.CompilerParams(dimension_semantics=("arbitrary",), vmem_limit_bytes=VMEM_LIMIT_BYTES)
    q = pl.pallas_call(
        _mla_q_body,
        grid=(m // tm,),
        in_specs=[row(MLA_Q_LORA, 0), full(1, MLA_Q_LORA), full(MLA_Q_LORA, 2 * hw), table, table],
        out_specs=row(hw, 0),
        out_shape=jax.ShapeDtypeStruct((m, hw), BF16),
        compiler_params=params,
        name="mla_q_proj",
    )(a_ext, g_q.reshape(1, -1), w_q.astype(BF16), cos, sin)
    ckv_blk = (MLA_Q_LORA + W) // MLA_KV_LORA
    ka_blk = (MLA_Q_LORA + W + MLA_KV_LORA) // W
    ckv, k, v, kpe, kpe_b = pl.pallas_call(
        _mla_kv_body,
        grid=(m // tm,),
        in_specs=[row(MLA_KV_LORA, ckv_blk), full(1, MLA_KV_LORA), full(MLA_KV_LORA, 2 * hw),
                  row(W, ka_blk), row(W, ka_blk + 1), table, table],
        out_specs=[row(MLA_KV_LORA, 0), row(hw, 0), row(hw, 0), row(W, 0), row(W, 0)],
        out_shape=[jax.ShapeDtypeStruct((m, MLA_KV_LORA), F32), jax.ShapeDtypeStruct((m, hw), BF16),
                   jax.ShapeDtypeStruct((m, hw), BF16), jax.ShapeDtypeStruct((m, W), F32),
                   jax.ShapeDtypeStruct((m, W), BF16)],
        compiler_params=params,
        name="mla_kv_proj",
    )(a_ext, g_kv.reshape(1, -1), w_kv.astype(BF16), a_ext, a_ext, cos, sin)
    return q, ckv, k, v, kpe, kpe_b


MLA_PAGES_PER_STEP = 64


def _mla_decode_body(pt_ref, q_ref, new_ref, *refs):
    del pt_ref
    npg = MLA_PAGES_PER_STEP
    pages, o_ref = refs[:npg], refs[npg]
    m_sc, l_sc, acc_sc = refs[npg + 1:]
    step = pl.program_id(1)

    @pl.when(step == 0)
    def _():
        m_sc[...] = jnp.full_like(m_sc, NEG)
        l_sc[...] = jnp.zeros_like(l_sc)
        acc_sc[...] = jnp.zeros_like(acc_sc)

    qf = q_ref[0]
    q = qf.astype(BF16)
    kt = jnp.concatenate([pages[p][0, 0].astype(BF16) for p in range(npg)], axis=1)
    s = jnp.dot(q, kt, preferred_element_type=F32)
    m = m_sc[...]
    m_new = jnp.maximum(m, jnp.max(s, axis=1, keepdims=True))
    alpha = jnp.exp(m - m_new)
    e = jnp.exp(s - m_new)
    l_new = alpha * l_sc[...] + jnp.sum(e, axis=1, keepdims=True)
    acc_new = alpha * acc_sc[...] + _dot_nt(e.astype(BF16), kt[:MLA_KV_LORA])
    m_sc[...] = m_new
    l_sc[...] = l_new
    acc_sc[...] = acc_new

    @pl.when(step == pl.num_programs(1) - 1)
    def _():
        new = new_ref[0]
        s_new = jnp.sum(qf * new, axis=1, keepdims=True)
        m_fin = jnp.maximum(m_new, s_new)
        a = jnp.exp(m_new - m_fin)
        e_new = jnp.exp(s_new - m_fin)
        o_ref[0] = (a * acc_new + e_new * new[:, :MLA_KV_LORA]) / (a * l_new + e_new)


def _mla_decode(q, new_rows, cache, layer, page_table):
    b, h, w = q.shape
    n_pages = page_table.shape[1]
    npg = MLA_PAGES_PER_STEP
    assert n_pages % npg == 0 and cache.shape[2] == PAGE_SIZE
    cache = cache.transpose(0, 1, 3, 2)

    def page_spec(p):
        return pl.BlockSpec((1, 1, w, PAGE_SIZE), lambda bi, si, pt: (layer, pt[bi, si * npg + p], 0, 0))

    grid_spec = pltpu.PrefetchScalarGridSpec(
        num_scalar_prefetch=1,
        grid=(b, n_pages // npg),
        in_specs=[pl.BlockSpec((1, h, w), lambda bi, si, pt: (bi, 0, 0)),
                  pl.BlockSpec((1, 1, w), lambda bi, si, pt: (bi, 0, 0))] + [page_spec(p) for p in range(npg)],
        out_specs=pl.BlockSpec((1, h, MLA_KV_LORA), lambda bi, si, pt: (bi, 0, 0)),
        scratch_shapes=[pltpu.VMEM((h, 1), F32), pltpu.VMEM((h, 1), F32), pltpu.VMEM((h, MLA_KV_LORA), F32)],
    )
    return pl.pallas_call(
        _mla_decode_body,
        grid_spec=grid_spec,
        out_shape=jax.ShapeDtypeStruct((b, h, MLA_KV_LORA), F32),
        compiler_params=pltpu.CompilerParams(dimension_semantics=("arbitrary", "arbitrary"),
                                             vmem_limit_bytes=VMEM_LIMIT_BYTES),
        name="mla_decode",
    )(page_table, q, new_rows, *([cache] * npg))


NSA_KV_PAIR = 2 * NSA_HEAD_DIM


def _div_pow2(x, d):
    assert d & (d - 1) == 0
    return x >> (d.bit_length() - 1)


def _alibi_col(first_head, n):
    head = first_head + lax.broadcasted_iota(jnp.int32, (n, 1), 0)
    return jnp.exp((head + 1).astype(F32) * (-8.0 / NSA_HEADS * 0.6931471805599453))


def _nsa_cmp_sample_body(q_ref, k_ref, v_ref, o_ref, idx_ref):
    nblk = k_ref.shape[2]
    pos = nblk * CMP_BLOCK
    blk_end = (lax.broadcasted_iota(jnp.int32, (1, nblk), 1) + 1) * CMP_BLOCK - 1
    dist = (pos - blk_end).astype(F32)
    imps = []
    for g in range(NSA_KV_HEADS):
        q = q_ref[0, g].astype(BF16)
        s = _dot_nt(q, k_ref[0, g]) - _alibi_col(g * NSA_GROUP, NSA_GROUP) * dist
        p = _softmax_rows(s, jnp.full(s.shape, True))
        o_ref[0, g] = jnp.dot(p.astype(BF16), v_ref[0, g], preferred_element_type=F32)
        imps.append(jnp.sum(p, axis=0, keepdims=True))
    imp = jnp.concatenate(imps, axis=0)
    jj = lax.broadcasted_iota(jnp.int32, (NSA_KV_HEADS, nblk), 1)
    score = jnp.where(jj == 0, FORCE, jnp.where(jj == nblk - 1, FORCE, imp))
    rank = jnp.zeros((NSA_KV_HEADS, nblk), jnp.int32)
    for i in range(nblk):
        ci = score[:, i:i + 1]
        rank = rank + jnp.where(ci > score, 1, jnp.where(ci == score, jnp.where(jj > i, 1, 0), 0))
    n_pick = min(TOP_K_BLOCKS, nblk + 1) - 1
    cols = [jnp.sum(jnp.where(rank == r, jj, 0), axis=1, keepdims=True) for r in range(n_pick)]
    cols.append(jnp.full((NSA_KV_HEADS, 1), nblk, jnp.int32))
    idx_ref[0] = jnp.concatenate(cols, axis=1)


def _nsa_cmp_sample(q, summ):
    b = q.shape[0]
    nblk = summ.shape[3]
    n_sel = min(TOP_K_BLOCKS, nblk + 1)
    sum_spec = pl.BlockSpec((1, NSA_KV_HEADS, nblk, NSA_HEAD_DIM), lambda i: (i, 0, 0, 0))
    return pl.pallas_call(
        _nsa_cmp_sample_body,
        grid=(b,),
        in_specs=[pl.BlockSpec((1, NSA_KV_HEADS, NSA_GROUP, NSA_HEAD_DIM), lambda i: (i, 0, 0, 0)),
                  sum_spec, sum_spec],
        out_specs=[pl.BlockSpec((1, NSA_KV_HEADS, NSA_GROUP, NSA_HEAD_DIM), lambda i: (i, 0, 0, 0)),
                   pl.BlockSpec((1, NSA_KV_HEADS, n_sel), lambda i: (i, 0, 0))],
        out_shape=[jax.ShapeDtypeStruct((b, NSA_KV_HEADS, NSA_GROUP, NSA_HEAD_DIM), F32),
                   jax.ShapeDtypeStruct((b, NSA_KV_HEADS, n_sel), jnp.int32)],
        compiler_params=pltpu.CompilerParams(dimension_semantics=("arbitrary",)),
        name="nsa_cmp_sample",
    )(q, summ[0], summ[1])


def _nsa_sel_sample_body(n_past, idx_ref, pt_ref, q_ref, new_ref, *refs):
    del pt_ref
    n_sel = (len(refs) - 1) // NSA_KV_HEADS
    pages, o_ref = refs[:-1], refs[-1]
    bi = pl.program_id(0)
    pos = n_past * SEL_BLOCK
    per_page = PAGE_SIZE // SEL_BLOCK
    tok = lax.broadcasted_iota(jnp.int32, (1, PAGE_SIZE), 1)
    for g in range(NSA_KV_HEADS):
        qf = q_ref[0, g]
        q = qf.astype(BF16)
        slope = _alibi_col(g * NSA_GROUP, NSA_GROUP)
        vts, scores = [], []
        for k in range(n_sel):
            j = idx_ref[bi, g * n_sel + k]
            kv = pages[g * n_sel + k][0].astype(BF16)
            dist = (pos - (_div_pow2(j, per_page) * PAGE_SIZE + tok)).astype(F32)
            s = jnp.dot(q, kv[:NSA_HEAD_DIM], preferred_element_type=F32) - slope * dist
            in_block = _div_pow2(tok, SEL_BLOCK) == (j & (per_page - 1))
            keep = jnp.where(j < n_past, jnp.where(in_block, 1, 0), 0) > 0
            scores.append(jnp.where(keep, s, NEG))
            vts.append(kv[NSA_HEAD_DIM:])
        new = new_ref[0, g]
        s_new = jnp.sum(qf * new[0:1], axis=1, keepdims=True)
        m = s_new
        for s in scores:
            m = jnp.maximum(m, jnp.max(s, axis=1, keepdims=True))
        e_new = jnp.exp(s_new - m)
        l = e_new
        acc = e_new * new[1:2]
        for s, vt in zip(scores, vts):
            e = jnp.exp(s - m)
            l = l + jnp.sum(e, axis=1, keepdims=True)
            acc = acc + _dot_nt(e.astype(BF16), vt)
        o_ref[0, g] = acc / l


def _nsa_channel_major(cache):
    lead = cache.ndim - 4
    perm = tuple(range(lead)) + (lead + 1, lead + 2, lead + 3, lead)
    t = cache.transpose(perm)
    return t.reshape(t.shape[:lead] + (NSA_KV_COLS, t.shape[-1]))


def _nsa_sel_sample(q, new_kv, cache_sel, layer, page_table, idx):
    b, g, n_sel = idx.shape
    n_pages = page_table.shape[1]
    per_page = PAGE_SIZE // SEL_BLOCK
    n_past = n_pages * per_page
    cache = _nsa_channel_major(cache_sel)

    def page_spec(gi, k):
        def index(bi, idx_ref, pt_ref):
            j = jnp.minimum(idx_ref[bi, gi * n_sel + k], n_past - 1)
            return (layer, pt_ref[bi, _div_pow2(j, per_page)], gi, 0)
        return pl.BlockSpec((None, 1, NSA_KV_PAIR, PAGE_SIZE), index)

    seq_spec = lambda rows: pl.BlockSpec((1, g, rows, NSA_HEAD_DIM), lambda bi, i_, p_: (bi, 0, 0, 0))
    grid_spec = pltpu.PrefetchScalarGridSpec(
        num_scalar_prefetch=2,
        grid=(b,),
        in_specs=[seq_spec(NSA_GROUP), seq_spec(2)] + [page_spec(gi, k) for gi in range(g) for k in range(n_sel)],
        out_specs=seq_spec(NSA_GROUP),
    )
    return pl.pallas_call(
        functools.partial(_nsa_sel_sample_body, n_past),
        grid_spec=grid_spec,
        out_shape=jax.ShapeDtypeStruct((b, g, NSA_GROUP, NSA_HEAD_DIM), F32),
        compiler_params=pltpu.CompilerParams(dimension_semantics=("arbitrary",)),
        name="nsa_sel_sample",
    )(idx.reshape(b, g * n_sel), page_table, q, new_kv, *([cache] * (g * n_sel)))


def _nsa_win_sample_body(q_ref, new_ref, win_ref, o_ref, nwin_ref):
    wbuf = win_ref.shape[2]
    win = win_ref[0]
    tok = lax.broadcasted_iota(jnp.int32, (1, wbuf), 1)
    dist = wbuf - tok
    valid = dist < WINDOW
    for g in range(NSA_KV_HEADS):
        r0 = g * NSA_KV_PAIR
        qf = q_ref[0, g]
        new = new_ref[0, g]
        kt = win[r0:r0 + NSA_HEAD_DIM].astype(BF16)
        vt = win[r0 + NSA_HEAD_DIM:r0 + NSA_KV_PAIR].astype(BF16)
        slope = _alibi_col(g * NSA_GROUP, NSA_GROUP)
        s = jnp.dot(qf.astype(BF16), kt, preferred_element_type=F32) - slope * dist.astype(F32)
        s = jnp.where(valid, s, NEG)
        s_new = jnp.sum(qf * new[0:1], axis=1, keepdims=True)
        m = jnp.maximum(s_new, jnp.max(s, axis=1, keepdims=True))
        e = jnp.exp(s - m)
        e_new = jnp.exp(s_new - m)
        acc = e_new * new[1:2] + _dot_nt(e.astype(BF16), vt)
        o_ref[0, g] = acc / (e_new + jnp.sum(e, axis=1, keepdims=True))
    cols = win.shape[0]
    eye = (lax.broadcasted_iota(jnp.int32, (cols, cols), 0) == lax.broadcasted_iota(jnp.int32, (cols, cols), 1))
    new_row = jnp.concatenate([new_ref[0, g][c:c + 1] for g in range(NSA_KV_HEADS) for c in range(2)], axis=1)
    new_col = jnp.sum(jnp.where(eye, new_row, 0.0), axis=1, keepdims=True)
    nwin_ref[0] = jnp.where(tok == wbuf - 1, new_col, pltpu.roll(win, wbuf - 1, 1))


def _nsa_win_sample(q, new_kv, win_state):
    b, wbuf = win_state.shape[:2]
    assert wbuf == WINDOW
    win = _nsa_channel_major(win_state)
    q_spec = pl.BlockSpec((1, NSA_KV_HEADS, NSA_GROUP, NSA_HEAD_DIM), lambda i: (i, 0, 0, 0))
    win_spec = pl.BlockSpec((1, NSA_KV_COLS, wbuf), lambda i: (i, 0, 0))
    return pl.pallas_call(
        _nsa_win_sample_body,
        grid=(b,),
        in_specs=[q_spec, pl.BlockSpec((1, NSA_KV_HEADS, 2, NSA_HEAD_DIM), lambda i: (i, 0, 0, 0)), win_spec],
        out_specs=[q_spec, win_spec],
        out_shape=[jax.ShapeDtypeStruct((b, NSA_KV_HEADS, NSA_GROUP, NSA_HEAD_DIM), F32),
                   jax.ShapeDtypeStruct((b, NSA_KV_COLS, wbuf), F32)],
        compiler_params=pltpu.CompilerParams(dimension_semantics=("arbitrary",)),
        name="nsa_win_sample",
    )(q, new_kv, win)


MLSTM_KERNEL_CHUNK = 256
MLSTM_QK_COLS = MLSTM_HEADS * MLSTM_DQK
MLSTM_V_COLS = MLSTM_HEADS * MLSTM_DV


def _log_sigmoid(x):
    return jnp.minimum(x, 0.0) - jnp.log(1.0 + jnp.exp(-jnp.abs(x)))


def _mlstm_chunk_body(q_ref, k_ref, v_ref, og_ref, vt_ref, ig_ref, fg_ref,
                      h_ref, c_ref, n_ref, m_ref, c_sc, n_sc, m_sc):
    chunk = pl.program_id(2)

    @pl.when(chunk == 0)
    def _():
        c_sc[...] = jnp.zeros_like(c_sc)
        n_sc[...] = jnp.zeros_like(n_sc)
        m_sc[...] = jnp.zeros_like(m_sc)

    L = q_ref.shape[1]
    q = q_ref[0]
    qb = q.astype(BF16)
    kb = (k_ref[0] * (MLSTM_DQK ** -0.5)).astype(BF16)
    i_row = ig_ref[0, 0, 0]
    f_row = _log_sigmoid(fg_ref[0, 0, 0])
    tt = lax.broadcasted_iota(jnp.int32, (L, L), 0)
    ss = lax.broadcasted_iota(jnp.int32, (L, L), 1)
    tri = ss <= tt
    b_col = jnp.sum(jnp.where(tri, f_row, 0.0), axis=1, keepdims=True)
    b_row = jnp.sum(jnp.where(tt == ss, b_col, 0.0), axis=0, keepdims=True)
    m = m_sc[...]
    c = c_sc[...]
    n = n_sc[...]
    d = jnp.where(tri, b_col - b_row + i_row, NEG)
    inter = b_col + m
    mt = jnp.maximum(inter, jnp.max(d, axis=1, keepdims=True))
    w = jnp.exp(d - mt)
    gq = jnp.exp(inter - mt)
    a = w * _dot_nt(qb, kb)
    num = (jnp.dot(a.astype(BF16), v_ref[0].astype(BF16), preferred_element_type=F32)
           + gq * _dot_nt(qb, c.astype(BF16)))
    den = jnp.sum(a, axis=1, keepdims=True) + gq * jnp.sum(q * n, axis=1, keepdims=True)
    hc = num / jnp.maximum(jnp.abs(den), jnp.exp(-mt))
    h_ref[0] = hc * jax.nn.sigmoid(og_ref[0])

    b_last = b_col[L - 1:L, :]
    m_new = mt[L - 1:L, :]
    wl = jnp.exp(b_last - b_row + i_row - m_new)
    gl = jnp.exp(b_last + m - m_new)
    c_new = gl * c + jnp.dot((vt_ref[0, 0] * wl).astype(BF16), kb, preferred_element_type=F32)
    wl8 = jnp.broadcast_to(wl, (8, L)).astype(BF16)
    n_new = gl * n + jnp.dot(wl8, kb, preferred_element_type=F32)[0:1]
    c_sc[...] = c_new
    n_sc[...] = n_new
    m_sc[...] = m_new

    @pl.when(chunk == pl.num_programs(2) - 1)
    def _():
        c_ref[0, 0] = c_new
        n_ref[0, 0] = n_new
        m_ref[0, 0] = m_new


def _mlstm_prompt(z, b_gate):
    B, S, _ = z.shape
    H, L = MLSTM_HEADS, MLSTM_KERNEL_CHUNK
    assert S % L == 0
    nc = S // L
    kblk, vblk = MLSTM_QK_COLS // MLSTM_DQK, (2 * MLSTM_QK_COLS) // MLSTM_DV
    gates = z[..., 2 * MLSTM_QK_COLS + 2 * MLSTM_V_COLS:] + b_gate
    gates = gates.reshape(B, nc, L, 2, H).transpose(3, 0, 4, 1, 2)[:, :, :, :, None, :]
    vt = z[..., 2 * MLSTM_QK_COLS:2 * MLSTM_QK_COLS + MLSTM_V_COLS].reshape(B, S, H, MLSTM_DV)
    vt = vt.transpose(0, 2, 3, 1)
    gate_spec = pl.BlockSpec((1, 1, 1, 1, L), lambda b, h, c: (b, h, c, 0, 0))
    state = lambda r, w: pl.BlockSpec((1, 1, r, w), lambda b, h, c: (b, h, 0, 0))
    hs, c, n, m = pl.pallas_call(
        _mlstm_chunk_body,
        grid=(B, H, nc),
        in_specs=[pl.BlockSpec((1, L, MLSTM_DQK), lambda b, h, c: (b, c, h)),
                  pl.BlockSpec((1, L, MLSTM_DQK), lambda b, h, c: (b, c, kblk + h)),
                  pl.BlockSpec((1, L, MLSTM_DV), lambda b, h, c: (b, c, vblk + h)),
                  pl.BlockSpec((1, L, MLSTM_DV), lambda b, h, c: (b, c, vblk + H + h)),
                  pl.BlockSpec((1, 1, MLSTM_DV, L), lambda b, h, c: (b, h, 0, c)),
                  gate_spec, gate_spec],
        out_specs=[pl.BlockSpec((1, L, MLSTM_DV), lambda b, h, c: (b, c, h)),
                   state(MLSTM_DV, MLSTM_DQK), state(1, MLSTM_DQK), state(1, 1)],
        out_shape=[jax.ShapeDtypeStruct((B, S, MLSTM_V_COLS), F32),
                   jax.ShapeDtypeStruct((B, H, MLSTM_DV, MLSTM_DQK), F32),
                   jax.ShapeDtypeStruct((B, H, 1, MLSTM_DQK), F32),
                   jax.ShapeDtypeStruct((B, H, 1, 1), F32)],
        scratch_shapes=[pltpu.VMEM((MLSTM_DV, MLSTM_DQK), F32), pltpu.VMEM((1, MLSTM_DQK), F32),
                        pltpu.VMEM((1, 1), F32)],
        compiler_params=pltpu.CompilerParams(dimension_semantics=("arbitrary",) * 3),
        name="mlstm_chunks",
    )(z, z, z, z, vt, gates[0], gates[1])
    return hs, c, n.reshape(B, H, MLSTM_DQK), m.reshape(B, H)


def _mlstm_step_body(q_ref, k_ref, v_ref, og_ref, ig_ref, fg_ref, c_ref, n_ref, m_ref,
                     h_ref, c_out, n_out, m_out):
    eye = (lax.broadcasted_iota(jnp.int32, (MLSTM_DV, MLSTM_DV), 0)
           == lax.broadcasted_iota(jnp.int32, (MLSTM_DV, MLSTM_DV), 1))
    for h in range(MLSTM_HEADS):
        c, n, m = c_ref[0, h], n_ref[0, h], m_ref[0, h]
        q = q_ref[0, h]
        k = k_ref[0, h] * (MLSTM_DQK ** -0.5)
        v = v_ref[0, h]
        i_g = ig_ref[0, h]
        inter = _log_sigmoid(fg_ref[0, h]) + m
        mt = jnp.maximum(inter, i_g)
        w = jnp.exp(i_g - mt)
        gq = jnp.exp(inter - mt)
        a = w * jnp.sum(q * k, axis=1, keepdims=True)
        cq_col = jnp.sum(c * q, axis=1, keepdims=True)
        cq_row = jnp.sum(jnp.where(eye, cq_col, 0.0), axis=0, keepdims=True)
        den = a + gq * jnp.sum(n * q, axis=1, keepdims=True)
        hc = (a * v + gq * cq_row) / jnp.maximum(jnp.abs(den), jnp.exp(-mt))
        h_ref[0, h] = hc * jax.nn.sigmoid(og_ref[0, h])
        v_col = jnp.sum(jnp.where(eye, v, 0.0), axis=1, keepdims=True)
        c_out[0, h] = gq * c + (w * v_col) * k
        n_out[0, h] = gq * n + w * k
        m_out[0, h] = mt


def _mlstm_sample(z, b_gate, c0, n0, m0):
    B, T, _ = z.shape
    assert T == 1
    H = MLSTM_HEADS
    z = z.reshape(B, -1)
    qk, hv = MLSTM_QK_COLS, MLSTM_V_COLS
    q = z[:, :qk].reshape(B, H, 1, MLSTM_DQK)
    k = z[:, qk:2 * qk].reshape(B, H, 1, MLSTM_DQK)
    v = z[:, 2 * qk:2 * qk + hv].reshape(B, H, 1, MLSTM_DV)
    og = z[:, 2 * qk + hv:2 * qk + 2 * hv].reshape(B, H, 1, MLSTM_DV)
    gates = z[:, 2 * qk + 2 * hv:] + b_gate
    ig = gates[:, :H].reshape(B, H, 1, 1)
    fg = gates[:, H:].reshape(B, H, 1, 1)
    spec = lambda r, w: pl.BlockSpec((1, H, r, w), lambda b: (b, 0, 0, 0))
    shapes = [(1, MLSTM_DV), (MLSTM_DV, MLSTM_DQK), (1, MLSTM_DQK), (1, 1)]
    hs, c, n, m = pl.pallas_call(
        _mlstm_step_body,
        grid=(B,),
        in_specs=[spec(1, MLSTM_DQK), spec(1, MLSTM_DQK), spec(1, MLSTM_DV), spec(1, MLSTM_DV),
                  spec(1, 1), spec(1, 1), spec(MLSTM_DV, MLSTM_DQK), spec(1, MLSTM_DQK), spec(1, 1)],
        out_specs=[spec(*s) for s in shapes],
        out_shape=[jax.ShapeDtypeStruct((B, H) + s, F32) for s in shapes],
        compiler_params=pltpu.CompilerParams(dimension_semantics=("arbitrary",)),
        name="mlstm_step",
    )(q, k, v, og, ig, fg, c0, n0.reshape(B, H, 1, MLSTM_DQK), m0.reshape(B, H, 1, 1))
    return hs.reshape(B, T, hv), c, n.reshape(B, H, MLSTM_DQK), m.reshape(B, H)


def _rope(x, pos):
    half = x.shape[-1] // 2
    freqs = ROPE_THETA ** (-jnp.arange(half, dtype=F32) / half)
    ang = pos.astype(F32)[:, None] * freqs[None, :]
    cos = jnp.cos(ang)[None, :, None, :]
    sin = jnp.sin(ang)[None, :, None, :]
    x1, x2 = x[..., :half], x[..., half:]
    return jnp.concatenate([x1 * cos - x2 * sin, x1 * sin + x2 * cos], axis=-1)


def _mla_prompt(x, g0, B, S, w_a, g_q, g_kv, w_uq, w_uk, w_uv):
    w_a_ext, w_q, w_kv = _mla_prompt_weights(w_a, w_uq, w_uk, w_uv)
    a_ext = _norm_proj(x, g0, w_a_ext)
    q, ckv, k, v, kpe, kpe_b = _mla_prompt_proj(a_ext, g_q, g_kv, w_q, w_kv, S)
    seq = lambda t: t.reshape(B, S, -1)
    o = _mla_flash(seq(q), seq(k), seq(kpe_b), seq(v))
    new_rows = jnp.concatenate([ckv, kpe[:, MLA_NOPE_DIM:MLA_NOPE_DIM + MLA_ROPE_DIM]], axis=-1)
    return o, new_rows.reshape(B, S, -1)


def _mla_sample(a, cache, j, page_table, g_q, g_kv, w_uq, w_uk, w_uv):
    B, T, _ = a.shape
    assert T == 1
    H = MLA_HEADS
    past_len = page_table.shape[1] * cache.shape[2]
    pos = past_len + jnp.arange(T, dtype=jnp.int32)
    a2 = a.reshape(B * T, -1)
    q = _norm_proj(a2[:, :MLA_Q_LORA], g_q, w_uq).reshape(B, T, H, MLA_NOPE_DIM + MLA_ROPE_DIM)
    w_kv = jnp.concatenate([w_uk.reshape(MLA_KV_LORA, -1), w_uv.reshape(MLA_KV_LORA, -1)], axis=1)
    _, ckv = _norm_proj(a2[:, MLA_Q_LORA:MLA_Q_LORA + MLA_KV_LORA], g_kv, w_kv, with_normed=True)
    kpe = _rope(a[..., MLA_Q_LORA + MLA_KV_LORA:][:, :, None, :], pos)
    q_pe = _rope(q[..., MLA_NOPE_DIM:], pos)
    new_rows = jnp.concatenate([ckv.reshape(B, T, -1), kpe[:, :, 0]], axis=-1)
    q_nope = q[:, 0, :, :MLA_NOPE_DIM].transpose(1, 0, 2)
    q_lat = _heads_matmul(q_nope, w_uk.transpose(1, 2, 0)).transpose(1, 0, 2)
    q_abs = jnp.concatenate([q_lat, q_pe[:, 0]], axis=-1) * MLA_SCALE
    o_lat = _mla_decode(q_abs, new_rows, cache, j, page_table)
    o = _heads_matmul(o_lat.transpose(1, 0, 2), w_uv.transpose(1, 0, 2))
    return o.transpose(1, 0, 2).reshape(B, T, -1), new_rows


def _nsa_split(z):
    B, T, _ = z.shape
    q = z[..., :NSA_Q_COLS].reshape(B, T, NSA_HEADS, NSA_HEAD_DIM)
    kv = z[..., NSA_Q_COLS:NSA_Q_COLS + 3 * NSA_KV_COLS].reshape(B, T, 3, NSA_KV_HEADS, 2, NSA_HEAD_DIM)
    g = jax.nn.sigmoid(z[..., NSA_Q_COLS + 3 * NSA_KV_COLS:]).reshape(B, T, NSA_HEADS, 3)
    return q, kv[:, :, 0], kv[:, :, 1], kv[:, :, 2], g


def _nsa_merge(g, o_cmp, o_sel, o_win):
    o = g[..., 0:1] * o_cmp + g[..., 1:2] * o_sel + g[..., 2:3] * o_win
    B, T = o.shape[:2]
    return o.reshape(B, T, -1)


def _nsa_seq_layout(kv):
    return kv.transpose(3, 0, 2, 1, 4).astype(BF16)


def _nsa_prompt(z, w_bd):
    B, S, _ = z.shape
    nb = S // CMP_BLOCK
    q = (z[..., :NSA_Q_COLS] * (NSA_SCALE * LOG2E)).astype(BF16)
    q = q.reshape(B, S, NSA_HEADS, NSA_HEAD_DIM).transpose(0, 2, 1, 3)
    kv = z[..., NSA_Q_COLS:NSA_Q_COLS + 3 * NSA_KV_COLS].reshape(B, S, 3, NSA_KV_HEADS, 2, NSA_HEAD_DIM)
    kv_c, kv_s, kv_w = kv[:, :, 0], kv[:, :, 1], kv[:, :, 2]
    summ = _summarize_blocks(kv_c.reshape(B * nb, CMP_BLOCK * NSA_KV_COLS), w_bd)
    summ = summ.reshape(B, nb, NSA_KV_HEADS, 2, NSA_HEAD_DIM).transpose(3, 0, 2, 1, 4).astype(BF16)
    gate = z[..., NSA_Q_COLS + 3 * NSA_KV_COLS:].reshape(B, S, NSA_KV_HEADS, 3 * NSA_GROUP).transpose(0, 2, 1, 3)
    y = _nsa_prompt_attend(q, summ, _nsa_seq_layout(kv_s), _nsa_seq_layout(kv_w), gate)
    return y, kv_c, kv_s, kv_w[:, -min(WINDOW, S):]


def _nsa_sample(z, cache_cmp, cache_sel, win_state, j, page_table, w_paged):
    B, T, _ = z.shape
    assert T == 1 and T < CMP_BLOCK
    G, Dh = NSA_KV_HEADS, NSA_HEAD_DIM
    assert cache_cmp.shape[2] == PAGE_SIZE
    n_pages = page_table.shape[1]
    q, kv_c, kv_s, kv_w, gate = _nsa_split(z)
    summ = _summarize_pool(cache_cmp[j], w_paged)[page_table]
    summ = summ.transpose(3, 0, 2, 1, 4, 5).reshape(2, B, G, n_pages * SUMM_PAGE_BLOCKS, Dh).astype(BF16)
    qg = q.reshape(B, G, NSA_GROUP, Dh) * NSA_SCALE
    o_cmp, idx = _nsa_cmp_sample(qg, summ)
    o_sel = _nsa_sel_sample(qg, kv_s.reshape(B, G, 2, Dh), cache_sel, j, page_table, idx)
    o_win, new_win = _nsa_win_sample(qg, kv_w.reshape(B, G, 2, Dh), win_state)
    heads = lambda o: o.reshape(B, T, NSA_HEADS, Dh)
    y = _nsa_merge(gate, heads(o_cmp), heads(o_sel), heads(o_win))
    new_win = new_win.reshape(B, G, 2, Dh, new_win.shape[-1]).transpose(0, 4, 1, 2, 3)
    return y, kv_c, kv_s, new_win


def kernel(x_prompt, x_sample, cache_mla_kv, state_mlstm_c, state_mlstm_n, state_mlstm_m, cache_nsa_cmp,
           cache_nsa_sel, state_nsa_win, page_table, norm_g, final_norm_g, mla_w_a, mla_g_q, mla_g_kv,
           mla_w_uq, mla_w_uk, mla_w_uv, mla_w_o, mlstm_w_in, mlstm_b_gate, mlstm_w_out, nsa_w_in,
           nsa_w_cmp, nsa_w_out, mlp_w1, mlp_w2):
    B, S, D = x_prompt.shape
    Bs, Ts, _ = x_sample.shape
    xp = x_prompt.reshape(B * S, D)
    xs = x_sample.reshape(Bs * Ts, D)
    mla_p, mla_s = [], []
    mc_p, mn_p, mm_p, mc_s, mn_s, mm_s = [], [], [], [], [], []
    cmp_p, cmp_s, sel_p, sel_s, win_p, win_s = [], [], [], [], [], []
    for i in range(DEPTH):
        j = i // N_MIXERS
        g0 = norm_g[i, 0]
        if i % N_MIXERS == 0:
            as_ = _norm_proj(xs, g0, mla_w_a[j]).reshape(Bs, Ts, -1)
            w = (mla_g_q[j], mla_g_kv[j], mla_w_uq[j], mla_w_uk[j], mla_w_uv[j])
            op, rp = _mla_prompt(xp, g0, B, S, mla_w_a[j], *w)
            os_, rs = _mla_sample(as_, cache_mla_kv, j, page_table, *w)
            mla_p.append(rp)
            mla_s.append(rs)
            w_out = mla_w_o[j]
        elif i % N_MIXERS == 1:
            zp = _norm_proj(xp, g0, mlstm_w_in[j]).reshape(B, S, -1)
            zs = _norm_proj(xs, g0, mlstm_w_in[j]).reshape(Bs, Ts, -1)
            op, cp, nst_p, mp = _mlstm_prompt(zp, mlstm_b_gate[j])
            os_, cs, nst_s, ms = _mlstm_sample(zs, mlstm_b_gate[j], state_mlstm_c[j], state_mlstm_n[j],
                                               state_mlstm_m[j])
            mc_p.append(cp)
            mn_p.append(nst_p)
            mm_p.append(mp)
            mc_s.append(cs)
            mn_s.append(nst_s)
            mm_s.append(ms)
            w_out = mlstm_w_out[j]
        else:
            zp = _norm_proj(xp, g0, nsa_w_in[j]).reshape(B, S, -1)
            zs = _norm_proj(xs, g0, nsa_w_in[j]).reshape(Bs, Ts, -1)
            op, kcp, ksp, kwp = _nsa_prompt(zp, _summ_weights(nsa_w_cmp[j]))
            os_, kcs, kss, kws = _nsa_sample(zs, cache_nsa_cmp, cache_nsa_sel, state_nsa_win[j], j,
                                             page_table, _summ_weights_paged(nsa_w_cmp[j]))
            cmp_p.append(kcp)
            cmp_s.append(kcs)
            sel_p.append(ksp)
            sel_s.append(kss)
            win_p.append(kwp)
            win_s.append(kws)
            w_out = nsa_w_out[j]
        xp = _proj_res(op.reshape(B * S, -1), w_out, xp)
        xs = _proj_res(os_.reshape(Bs * Ts, -1), w_out, xs)
        xp = _mlp_res(xp, norm_g[i, 1], mlp_w1[i], mlp_w2[i])
        xs = _mlp_res(xs, norm_g[i, 1], mlp_w1[i], mlp_w2[i])
    y_prompt = _final_norm(xp, final_norm_g).reshape(B, S, D)
    y_sample = _final_norm(xs, final_norm_g).reshape(Bs, Ts, D)
    return (y_prompt, y_sample,
            jnp.stack(mla_p), jnp.stack(mla_s),
            jnp.stack(mc_p), jnp.stack(mn_p), jnp.stack(mm_p),
            jnp.stack(mc_s), jnp.stack(mn_s), jnp.stack(mm_s),
            jnp.stack(cmp_p), jnp.stack(cmp_s),
            jnp.stack(sel_p), jnp.stack(sel_s),
            jnp.stack(win_p), jnp.stack(win_s))
```

```python
import functools

import jax
import jax.numpy as jnp
from jax import lax
from jax.experimental import pallas as pl
from jax.experimental.pallas import tpu as pltpu

F32 = jnp.float32
BF16 = jnp.bfloat16

D_MODEL = 1024
DEPTH = 4
N_MIXERS = 3
PAGE_SIZE = 128

MLA_HEADS = 16
MLA_NOPE_DIM = 64
MLA_ROPE_DIM = 32
MLA_V_DIM = 64
MLA_Q_LORA = 384
MLA_KV_LORA = 256
MLA_SCALE = (MLA_NOPE_DIM + MLA_ROPE_DIM) ** -0.5
ROPE_THETA = 10000.0

MLSTM_HEADS = 4
MLSTM_DQK = 128
MLSTM_DV = 256

NSA_HEADS = 16
NSA_KV_HEADS = 4
NSA_HEAD_DIM = 64
CMP_BLOCK = 64
SEL_BLOCK = 64
TOP_K_BLOCKS = 16
WINDOW = 512
NSA_Q_COLS = NSA_HEADS * NSA_HEAD_DIM
NSA_KV_COLS = NSA_KV_HEADS * 2 * NSA_HEAD_DIM
NSA_SCALE = NSA_HEAD_DIM ** -0.5

D_FF = 4 * D_MODEL
RMS_EPS = 1e-6
NEG = -1e30
FORCE = 1e4

VMEM_LIMIT_BYTES = 56 * 1024 * 1024
FF_CHUNK = 512


def _row_tile(m):
    for t in (512, 256, 128):
        if m % t == 0:
            return t
    return m


def _rms_rows(x, g):
    return x * lax.rsqrt(jnp.mean(x * x, axis=-1, keepdims=True) + RMS_EPS) * g


def _norm_proj_body(x_ref, g_ref, w_ref, o_ref, *h_ref):
    h = _rms_rows(x_ref[...], g_ref[...])
    o_ref[...] = jnp.dot(h.astype(BF16), w_ref[...], preferred_element_type=F32)
    if h_ref:
        h_ref[0][...] = h


def _norm_proj(x, g, w, with_normed=False):
    m, d = x.shape
    n = w.shape[1]
    tm = _row_tile(m)
    out_specs = [pl.BlockSpec((tm, n), lambda i: (i, 0))]
    out_shape = [jax.ShapeDtypeStruct((m, n), F32)]
    if with_normed:
        out_specs.append(pl.BlockSpec((tm, d), lambda i: (i, 0)))
        out_shape.append(jax.ShapeDtypeStruct((m, d), F32))
    out = pl.pallas_call(
        _norm_proj_body,
        grid=(m // tm,),
        in_specs=[pl.BlockSpec((tm, d), lambda i: (i, 0)),
                  pl.BlockSpec((1, d), lambda i: (0, 0)),
                  pl.BlockSpec((d, n), lambda i: (0, 0))],
        out_specs=out_specs,
        out_shape=out_shape,
        compiler_params=pltpu.CompilerParams(dimension_semantics=("arbitrary",),
                                             vmem_limit_bytes=VMEM_LIMIT_BYTES),
        name="norm_proj",
    )(x, g.reshape(1, d), w.astype(BF16))
    return out if with_normed else out[0]


def _proj_res_body(a_ref, w_ref, r_ref, o_ref):
    o_ref[...] = r_ref[...] + jnp.dot(a_ref[...].astype(BF16), w_ref[...], preferred_element_type=F32)


def _proj_res(a, w, res):
    m, k = a.shape
    d = w.shape[1]
    tm = _row_tile(m)
    return pl.pallas_call(
        _proj_res_body,
        grid=(m // tm,),
        in_specs=[pl.BlockSpec((tm, k), lambda i: (i, 0)),
                  pl.BlockSpec((k, d), lambda i: (0, 0)),
                  pl.BlockSpec((tm, d), lambda i: (i, 0))],
        out_specs=pl.BlockSpec((tm, d), lambda i: (i, 0)),
        out_shape=jax.ShapeDtypeStruct((m, d), F32),
        compiler_params=pltpu.CompilerParams(dimension_semantics=("arbitrary",),
                                             vmem_limit_bytes=VMEM_LIMIT_BYTES),
        name="proj_res",
    )(a, w.astype(BF16), res)


def _mlp_body(x_ref, g_ref, w1_ref, w2_ref, o_ref):
    x = x_ref[...]
    h = _rms_rows(x, g_ref[...]).astype(BF16)
    acc = x
    for c in range(D_FF // FF_CHUNK):
        a = jnp.dot(h, w1_ref[:, c * FF_CHUNK:(c + 1) * FF_CHUNK], preferred_element_type=F32)
        a = jnp.maximum(a, 0.0)
        acc = acc + jnp.dot((a * a).astype(BF16), w2_ref[c * FF_CHUNK:(c + 1) * FF_CHUNK, :],
                            preferred_element_type=F32)
    o_ref[...] = acc


def _mlp_res(x, g, w1, w2):
    m, d = x.shape
    tm = _row_tile(m)
    return pl.pallas_call(
        _mlp_body,
        grid=(m // tm,),
        in_specs=[pl.BlockSpec((tm, d), lambda i: (i, 0)),
                  pl.BlockSpec((1, d), lambda i: (0, 0)),
                  pl.BlockSpec((d, D_FF), lambda i: (0, 0)),
                  pl.BlockSpec((D_FF, d), lambda i: (0, 0))],
        out_specs=pl.BlockSpec((tm, d), lambda i: (i, 0)),
        out_shape=jax.ShapeDtypeStruct((m, d), F32),
        compiler_params=pltpu.CompilerParams(dimension_semantics=("arbitrary",),
                                             vmem_limit_bytes=VMEM_LIMIT_BYTES),
        name="mlp_res",
    )(x, g.reshape(1, d), w1.astype(BF16), w2.astype(BF16))


def _final_norm_body(x_ref, g_ref, o_ref):
    o_ref[...] = _rms_rows(x_ref[...], g_ref[...])


def _final_norm(x, g):
    m, d = x.shape
    tm = _row_tile(m)
    return pl.pallas_call(
        _final_norm_body,
        grid=(m // tm,),
        in_specs=[pl.BlockSpec((tm, d), lambda i: (i, 0)), pl.BlockSpec((1, d), lambda i: (0, 0))],
        out_specs=pl.BlockSpec((tm, d), lambda i: (i, 0)),
        out_shape=jax.ShapeDtypeStruct((m, d), F32),
        compiler_params=pltpu.CompilerParams(dimension_semantics=("arbitrary",)),
        name="final_norm",
    )(x, g.reshape(1, d))


def _dot_nt(a, b):
    return lax.dot_general(a, b, (((1,), (1,)), ((), ())), preferred_element_type=F32)


SUMM_L_PER_STEP = 8
SUMM_COLS = NSA_KV_COLS
SUMM_HALF = SUMM_COLS // 2


def _summ_weights(w_cmp):
    wk = jnp.stack([w_cmp[0], w_cmp[1], w_cmp[0], w_cmp[1]], axis=0)
    bd = jnp.einsum('kj,klde->lkdje', jnp.eye(4, dtype=F32), wk)
    return bd.reshape(CMP_BLOCK, SUMM_HALF, SUMM_HALF).astype(BF16)


def _summarize_body(x_ref, w_ref, o_ref):
    @pl.when(pl.program_id(1) == 0)
    def _():
        o_ref[...] = jnp.zeros_like(o_ref)

    lo = o_ref[:, :SUMM_HALF]
    hi = o_ref[:, SUMM_HALF:]
    for li in range(SUMM_L_PER_STEP):
        x = x_ref[:, li * SUMM_COLS:(li + 1) * SUMM_COLS].astype(BF16)
        w = w_ref[li]
        lo = lo + jnp.dot(x[:, :SUMM_HALF], w, preferred_element_type=F32)
        hi = hi + jnp.dot(x[:, SUMM_HALF:], w, preferred_element_type=F32)
    o_ref[:, :SUMM_HALF] = lo
    o_ref[:, SUMM_HALF:] = hi


def _summarize_blocks(x2d, w_bd):
    nb = x2d.shape[0]
    p = 512 if nb % 512 == 0 else nb
    step_cols = SUMM_L_PER_STEP * SUMM_COLS
    return pl.pallas_call(
        _summarize_body,
        grid=(nb // p, CMP_BLOCK // SUMM_L_PER_STEP),
        in_specs=[pl.BlockSpec((p, step_cols), lambda i, l: (i, l)),
                  pl.BlockSpec((SUMM_L_PER_STEP, SUMM_HALF, SUMM_HALF), lambda i, l: (l, 0, 0))],
        out_specs=pl.BlockSpec((p, SUMM_COLS), lambda i, l: (i, 0)),
        out_shape=jax.ShapeDtypeStruct((nb, SUMM_COLS), F32),
        compiler_params=pltpu.CompilerParams(dimension_semantics=("arbitrary", "arbitrary"),
                                             vmem_limit_bytes=VMEM_LIMIT_BYTES),
        name="nsa_summarize",
    )(x2d, w_bd)


SUMM_D_PER_STEP = 8
SUMM_PAGE_BLOCKS = PAGE_SIZE // CMP_BLOCK
SUMM_PAGE_OUT = 2 * SUMM_PAGE_BLOCKS * NSA_HEAD_DIM


def _summ_weights_paged(w_cmp):
    n = 2 * SUMM_PAGE_BLOCKS
    wk = jnp.stack([w_cmp[c] for c in range(2) for _ in range(SUMM_PAGE_BLOCKS)], axis=0)
    bd = jnp.einsum('kj,klde->dklje', jnp.eye(n, dtype=F32), wk)
    return bd.reshape(NSA_HEAD_DIM, 2 * PAGE_SIZE, SUMM_PAGE_OUT).astype(BF16)


def _rows_by_channel(x):
    p, n, lanes = x.shape
    assert n == 8 and p % 8 == 0
    x4 = x.reshape(p // 8, 8, 8, lanes)
    parts = [x4[:, i] for i in range(8)]
    sub = lax.broadcasted_iota(jnp.int32, (1, 8, lanes), 1)
    for s in (4, 2, 1):
        low = (sub & s) == 0
        nxt = list(parts)
        for i in range(8):
            if i & s == 0:
                a, b = parts[i], parts[i + s]
                nxt[i] = jnp.where(low, a, pltpu.roll(b, s, 1))
                nxt[i + s] = jnp.where(low, pltpu.roll(a, 8 - s, 1), b)
        parts = nxt
    return [t.reshape(p, lanes) for t in parts]


def _summarize_pool_body(k_ref, v_ref, w_ref, o_ref):
    dg = pl.program_id(2)

    @pl.when(dg == 0)
    def _():
        o_ref[...] = jnp.zeros_like(o_ref)

    acc = o_ref[...]
    ks = _rows_by_channel(k_ref[...])
    vs = _rows_by_channel(v_ref[...])
    for dd in range(SUMM_D_PER_STEP):
        lhs = jnp.concatenate([ks[dd], vs[dd]], axis=1).astype(BF16)
        acc = acc + jnp.dot(lhs, w_ref[dg * SUMM_D_PER_STEP + dd], preferred_element_type=F32)
    o_ref[...] = acc


def _summarize_pool(cache, w_paged):
    pool = cache.shape[0]
    cm = _nsa_channel_major(cache)
    p = 512 if pool % 512 == 0 else pool
    d_steps = NSA_HEAD_DIM // SUMM_D_PER_STEP
    rows_per_group = NSA_KV_PAIR // SUMM_D_PER_STEP
    out = pl.pallas_call(
        _summarize_pool_body,
        grid=(pool // p, NSA_KV_HEADS, d_steps),
        in_specs=[pl.BlockSpec((p, SUMM_D_PER_STEP, PAGE_SIZE), lambda i, g, d: (i, g * rows_per_group + d, 0)),
                  pl.BlockSpec((p, SUMM_D_PER_STEP, PAGE_SIZE),
                               lambda i, g, d: (i, g * rows_per_group + d_steps + d, 0)),
                  pl.BlockSpec((NSA_HEAD_DIM, 2 * PAGE_SIZE, SUMM_PAGE_OUT), lambda i, g, d: (0, 0, 0))],
        out_specs=pl.BlockSpec((p, SUMM_PAGE_OUT), lambda i, g, d: (i, g)),
        out_shape=jax.ShapeDtypeStruct((pool, NSA_KV_HEADS * SUMM_PAGE_OUT), F32),
        compiler_params=pltpu.CompilerParams(dimension_semantics=("arbitrary",) * 3,
                                             vmem_limit_bytes=VMEM_LIMIT_BYTES),
        name="nsa_summarize_pool",
    )(cm, cm, w_paged)
    return out.reshape(pool, NSA_KV_HEADS, 2, SUMM_PAGE_BLOCKS, NSA_HEAD_DIM)


LANES = 128
NSA_TQ = 256
NSA_TK = 512
NSA_GROUP = NSA_HEADS // NSA_KV_HEADS
NSA_BAND = WINDOW + NSA_TQ


def _softmax_rows(s, valid):
    m = jnp.max(s, axis=1, keepdims=True)
    e = jnp.where(valid, jnp.exp(s - m), 0.0)
    l = jnp.sum(e, axis=1, keepdims=True)
    return e / jnp.where(l > 0.0, l, 1.0)


LOG2E = 1.4426950408889634
MASK_BIG = 1e30
M_INIT = -0.5e30


def _nsa_prompt_body(q_ref, ksum_ref, vsum_ref, ks_ref, vs_ref, kw_ref, vw_ref, gate_ref, o_ref):
    g = pl.program_id(1)
    q0 = pl.program_id(2) * NSA_TQ
    rows = NSA_GROUP * NSA_TQ
    nblk = ksum_ref.shape[2]
    dh = NSA_HEAD_DIM
    q = q_ref[0].reshape(rows, dh)

    row = lax.broadcasted_iota(jnp.int32, (rows, 1), 0)
    tok_in_tile = row & (NSA_TQ - 1)
    qpos = q0 + tok_in_tile
    head = g * NSA_GROUP + (row >> (NSA_TQ.bit_length() - 1))
    slope = jnp.exp((head + 1).astype(F32) * (-8.0 / NSA_HEADS * 0.6931471805599453)) * LOG2E

    s = _dot_nt(q, ksum_ref[0, 0])
    blk_end = (lax.broadcasted_iota(jnp.int32, (1, nblk), 1) + 1) * CMP_BLOCK - 1
    dist = qpos - blk_end
    vis = dist >= 0
    s = jnp.where(vis, s - slope * dist.astype(F32), NEG)
    e = jnp.where(vis, jnp.exp2(s - jnp.max(s, axis=1, keepdims=True)), 0.0)
    l = jnp.sum(e, axis=1, keepdims=True)
    p = e / jnp.where(l > 0.0, l, 1.0)
    o_cmp = jnp.dot(p.astype(BF16), vsum_ref[0, 0], preferred_element_type=F32)
    imp = p[0:NSA_TQ]
    for h in range(1, NSA_GROUP):
        imp = imp + p[h * NSA_TQ:(h + 1) * NSA_TQ]

    imp_t = jnp.concatenate([imp, jnp.zeros((NSA_TQ, LANES - nblk), F32)], axis=1).T[:nblk]
    cur = (q0 + lax.broadcasted_iota(jnp.int32, (1, NSA_TQ), 1)) >> 6
    jj = lax.broadcasted_iota(jnp.int32, (nblk, NSA_TQ), 0)
    forced = jnp.where(jj == 0, 1, jnp.where(jj == cur, 1, jnp.where(jj == cur - 1, 1, 0)))
    score = jnp.where(forced > 0, FORCE, jnp.where(jj <= cur, imp_t, -FORCE))
    rank = jnp.zeros((nblk, NSA_TQ), jnp.int32)
    for i in range(nblk):
        ci = score[i:i + 1, :]
        rank = rank + jnp.where(ci > score, 1, jnp.where(ci == score, jnp.where(jj > i, 1, 0), 0))
    unpicked_t = jnp.where(rank < min(TOP_K_BLOCKS, nblk), 0.0, -1.0)
    unpicked = jnp.concatenate([unpicked_t, jnp.zeros((LANES - nblk, NSA_TQ), F32)], axis=0).T[:, :dh]
    qs = jnp.concatenate([q, jnp.concatenate([unpicked.astype(BF16)] * NSA_GROUP, axis=0)], axis=1)

    def sel_tile(kt, carry, diagonal):
        m, acc = carry
        k0 = pl.multiple_of(kt * NSA_TK, NSA_TK)
        rel = k0 - q0 + lax.broadcasted_iota(jnp.int32, (1, NSA_TK), 1)
        s = _dot_nt(qs, ks_ref[0, 0, pl.ds(k0, NSA_TK), :]) + slope * rel.astype(F32)
        if diagonal:
            s = jnp.where(rel <= tok_in_tile, s, NEG)
        m_new = jnp.maximum(m, jnp.max(s, axis=1, keepdims=True))
        e = jnp.exp2(s - m_new).astype(BF16)
        acc = jnp.exp2(m - m_new) * acc + jnp.dot(e, vs_ref[0, 0, pl.ds(k0, NSA_TK), :],
                                                  preferred_element_type=F32)
        return m_new, acc

    n_kt = (q0 + NSA_TQ + NSA_TK - 1) // NSA_TK
    init = (jnp.full((rows, 1), M_INIT, F32), jnp.zeros((rows, 2 * dh), F32))
    carry = lax.fori_loop(0, n_kt - 1, functools.partial(sel_tile, diagonal=False), init)
    _, acc = sel_tile(n_kt - 1, carry, True)
    l = acc[:, dh:dh + 1]
    o_sel = acc[:, :dh] / jnp.where(l > 0.0, l, 1.0)

    w0 = pl.multiple_of(jnp.maximum(q0 - WINDOW, 0), NSA_TQ)
    dist = qpos - (w0 + lax.broadcasted_iota(jnp.int32, (1, NSA_BAND), 1))
    valid = jnp.where(dist >= 0, jnp.where(dist < WINDOW, 1, 0), 0) > 0
    s = jnp.where(valid, _dot_nt(q, kw_ref[0, 0, pl.ds(w0, NSA_BAND), :]) - slope * dist.astype(F32), NEG)
    e = jnp.exp2(s - jnp.max(s, axis=1, keepdims=True))
    o_win = (jnp.dot(e.astype(BF16), vw_ref[0, 0, pl.ds(w0, NSA_BAND), :], preferred_element_type=F32)
             / jnp.sum(e, axis=1, keepdims=True))

    gate = jax.nn.sigmoid(gate_ref[0, 0])
    outs = []
    for h in range(NSA_GROUP):
        r = slice(h * NSA_TQ, (h + 1) * NSA_TQ)
        outs.append(gate[:, 3 * h:3 * h + 1] * o_cmp[r] + gate[:, 3 * h + 1:3 * h + 2] * o_sel[r]
                    + gate[:, 3 * h + 2:3 * h + 3] * o_win[r])
    o_ref[0] = jnp.concatenate(outs, axis=-1)


def _nsa_prompt_attend(q, summ, kv_s, kv_w, gate):
    b, h, s, dh = q.shape
    g = NSA_KV_HEADS
    nblk = summ.shape[3]
    assert s % NSA_TK == 0 and s >= NSA_BAND and nblk <= dh
    blk = jnp.arange(s, dtype=jnp.int32)[:, None] // SEL_BLOCK
    onehot = jnp.where(blk == jnp.arange(dh, dtype=jnp.int32)[None, :], MASK_BIG, 0.0).astype(BF16)
    ks = jnp.concatenate([kv_s[0], jnp.broadcast_to(onehot, (b, g, s, dh))], axis=-1)
    ones = jnp.zeros((s, dh), BF16).at[:, 0].set(1.0)
    vs = jnp.concatenate([kv_s[1], jnp.broadcast_to(ones, (b, g, s, dh))], axis=-1)
    seq_spec = lambda w: pl.BlockSpec((1, 1, s, w), lambda bi, gi, qi: (bi, gi, 0, 0))
    sum_spec = pl.BlockSpec((1, 1, nblk, dh), lambda bi, gi, qi: (bi, gi, 0, 0))
    return pl.pallas_call(
        _nsa_prompt_body,
        grid=(b, g, s // NSA_TQ),
        in_specs=[pl.BlockSpec((1, NSA_GROUP, NSA_TQ, dh), lambda bi, gi, qi: (bi, gi, qi, 0)),
                  sum_spec, sum_spec, seq_spec(2 * dh), seq_spec(2 * dh), seq_spec(dh), seq_spec(dh),
                  pl.BlockSpec((1, 1, NSA_TQ, 3 * NSA_GROUP), lambda bi, gi, qi: (bi, gi, qi, 0))],
        out_specs=pl.BlockSpec((1, NSA_TQ, NSA_GROUP * dh), lambda bi, gi, qi: (bi, qi, gi)),
        out_shape=jax.ShapeDtypeStruct((b, s, h * dh), F32),
        compiler_params=pltpu.CompilerParams(dimension_semantics=("arbitrary",) * 3,
                                             vmem_limit_bytes=VMEM_LIMIT_BYTES),
        name="nsa_prompt_attend",
    )(q, summ[0], summ[1], ks, vs, kv_w[0], kv_w[1], gate)


def _heads_matmul_body(x_ref, w_ref, o_ref):
    o_ref[0] = jnp.dot(x_ref[0].astype(BF16), w_ref[0], preferred_element_type=F32)


def _heads_matmul(x, w):
    h, m, k = x.shape
    n = w.shape[2]
    return pl.pallas_call(
        _heads_matmul_body,
        grid=(h,),
        in_specs=[pl.BlockSpec((1, m, k), lambda i: (i, 0, 0)), pl.BlockSpec((1, k, n), lambda i: (i, 0, 0))],
        out_specs=pl.BlockSpec((1, m, n), lambda i: (i, 0, 0)),
        out_shape=jax.ShapeDtypeStruct((h, m, n), F32),
        compiler_params=pltpu.CompilerParams(dimension_semantics=("arbitrary",)),
        name="heads_matmul",
    )(x, w.astype(BF16))


MLA_TILE = 512
MLA_QK_PAD = 128
MLA_HEAD_PAIR = 2


def _mla_flash_body(q_ref, k_ref, kpe_ref, v_ref, o_ref):
    qi = pl.program_id(2)
    t = MLA_TILE
    w = MLA_QK_PAD
    causal = lax.broadcasted_iota(jnp.int32, (t, t), 0) >= lax.broadcasted_iota(jnp.int32, (t, t), 1)
    qs = [q_ref[0, :, hh * w:(hh + 1) * w] for hh in range(MLA_HEAD_PAIR)]

    def step(kt, carry, diagonal):
        k0 = pl.multiple_of(kt * t, t)
        kpe = kpe_ref[0, pl.ds(k0, t), :]
        out = []
        for hh in range(MLA_HEAD_PAIR):
            m, acc = carry[hh]
            s = _dot_nt(qs[hh], k_ref[0, pl.ds(k0, t), hh * w:(hh + 1) * w] + kpe)
            if diagonal:
                s = jnp.where(causal, s, NEG)
            m_new = jnp.maximum(m, jnp.max(s, axis=1, keepdims=True))
            e = jnp.exp2(s - m_new).astype(BF16)
            acc = jnp.exp2(m - m_new) * acc + jnp.dot(e, v_ref[0, pl.ds(k0, t), hh * w:(hh + 1) * w],
                                                      preferred_element_type=F32)
            out.append((m_new, acc))
        return tuple(out)

    init = tuple((jnp.full((t, 1), M_INIT, F32), jnp.zeros((t, MLA_QK_PAD), F32)) for _ in range(MLA_HEAD_PAIR))
    carry = lax.fori_loop(0, qi, functools.partial(step, diagonal=False), init)
    carry = step(qi, carry, True)
    o_ref[0] = jnp.concatenate([acc[:, :MLA_V_DIM] / acc[:, MLA_V_DIM:MLA_V_DIM + 1] for _, acc in carry],
                               axis=-1)


def _mla_flash(q, k, kpe, v):
    b, s, hw = q.shape
    t = MLA_TILE
    pair = MLA_HEAD_PAIR * MLA_QK_PAD
    assert s % t == 0 and hw % pair == 0
    seq = lambda bi, hi, qi: (bi, 0, hi)
    return pl.pallas_call(
        _mla_flash_body,
        grid=(b, hw // pair, s // t),
        in_specs=[pl.BlockSpec((1, t, pair), lambda bi, hi, qi: (bi, qi, hi)),
                  pl.BlockSpec((1, s, pair), seq),
                  pl.BlockSpec((1, s, MLA_QK_PAD), lambda bi, hi, qi: (bi, 0, 0)),
                  pl.BlockSpec((1, s, pair), seq)],
        out_specs=pl.BlockSpec((1, t, MLA_HEAD_PAIR * MLA_V_DIM), lambda bi, hi, qi: (bi, qi, hi)),
        out_shape=jax.ShapeDtypeStruct((b, s, (hw // MLA_QK_PAD) * MLA_V_DIM), F32),
        compiler_params=pltpu.CompilerParams(dimension_semantics=("arbitrary",) * 3,
                                             vmem_limit_bytes=VMEM_LIMIT_BYTES),
        name="mla_flash",
    )(q, k, kpe, v)


def _rot_cols(w):
    half = w.shape[-1] // 2
    return jnp.concatenate([-w[..., half:], w[..., :half]], axis=-1)


def _mla_prompt_weights(w_a, w_uq, w_uk, w_uv):
    H, N, R, W = MLA_HEADS, MLA_NOPE_DIM, MLA_ROPE_DIM, MLA_QK_PAD
    d = w_a.shape[0]
    w_kpe = w_a[:, MLA_Q_LORA + MLA_KV_LORA:]
    lanes = lambda w: jnp.concatenate([jnp.zeros((d, N), F32), w, jnp.zeros((d, W - N - R), F32)], axis=1)
    w_a_ext = jnp.concatenate([w_a[:, :MLA_Q_LORA], jnp.zeros((d, W), F32),
                               w_a[:, MLA_Q_LORA:MLA_Q_LORA + MLA_KV_LORA], lanes(w_kpe), lanes(_rot_cols(w_kpe))],
                              axis=1)
    wq = w_uq.reshape(MLA_Q_LORA, H, N + R)
    zq = jnp.zeros((MLA_Q_LORA, H, W - N - R), F32)
    plain = jnp.concatenate([wq, zq], axis=-1)
    rot = jnp.concatenate([jnp.zeros((MLA_Q_LORA, H, N), F32), _rot_cols(wq[..., N:]), zq], axis=-1)
    w_q = jnp.concatenate([plain.reshape(MLA_Q_LORA, H * W), rot.reshape(MLA_Q_LORA, H * W)], axis=1)
    zk = jnp.zeros((MLA_KV_LORA, H, W - N), F32)
    zv = jnp.zeros((MLA_KV_LORA, H, W - MLA_V_DIM), F32)
    w_kv = jnp.concatenate([jnp.concatenate([w_uk, zk], axis=-1).reshape(MLA_KV_LORA, H * W),
                            jnp.concatenate([w_uv, zv], axis=-1).reshape(MLA_KV_LORA, H * W)], axis=1)
    return w_a_ext, w_q, w_kv


def _rope_tables(s):
    half = MLA_ROPE_DIM // 2
    freqs = ROPE_THETA ** (-jnp.arange(half, dtype=F32) / half)
    ang = jnp.arange(s, dtype=F32)[:, None] * freqs[None, :]
    pad = jnp.zeros((s, MLA_QK_PAD - MLA_NOPE_DIM - MLA_ROPE_DIM), F32)
    cos = jnp.concatenate([jnp.ones((s, MLA_NOPE_DIM), F32), jnp.cos(ang), jnp.cos(ang), pad], axis=1)
    sin = jnp.concatenate([jnp.zeros((s, MLA_NOPE_DIM), F32), jnp.sin(ang), jnp.sin(ang), pad], axis=1)
    return cos, sin


def _mla_q_body(x_ref, g_ref, w_ref, cos_ref, sin_ref, o_ref):
    h = _rms_rows(x_ref[...], g_ref[...]).astype(BF16)
    z = jnp.dot(h, w_ref[...], preferred_element_type=F32)
    cos = cos_ref[...] * (MLA_SCALE * LOG2E)
    sin = sin_ref[...] * (MLA_SCALE * LOG2E)
    hw = o_ref.shape[1]
    for c in range(0, hw, MLA_QK_PAD):
        o_ref[:, c:c + MLA_QK_PAD] = (z[:, c:c + MLA_QK_PAD] * cos
                                      + z[:, hw + c:hw + c + MLA_QK_PAD] * sin).astype(BF16)


def _mla_kv_body(x_ref, g_ref, w_ref, ka_ref, kb_ref, cos_ref, sin_ref, ckv_ref, k_ref, v_ref, kpe_ref, kpeb_ref):
    ckv = _rms_rows(x_ref[...], g_ref[...])
    ckv_ref[...] = ckv
    z = jnp.dot(ckv.astype(BF16), w_ref[...], preferred_element_type=F32)
    hw = k_ref.shape[1]
    k_ref[...] = z[:, :hw].astype(BF16)
    ones_col = jnp.where(lax.broadcasted_iota(jnp.int32, (1, MLA_QK_PAD), 1) == MLA_V_DIM, 1.0, 0.0)
    for c in range(0, hw, MLA_QK_PAD):
        v_ref[:, c:c + MLA_QK_PAD] = (z[:, hw + c:hw + c + MLA_QK_PAD] + ones_col).astype(BF16)
    kpe = ka_ref[...] * cos_ref[...] + kb_ref[...] * sin_ref[...]
    kpe_ref[...] = kpe
    kpeb_ref[...] = kpe.astype(BF16)


def _mla_prompt_proj(a_ext, g_q, g_kv, w_q, w_kv, s):
    m = a_ext.shape[0]
    tm = _row_tile(m)
    assert s % tm == 0
    W = MLA_QK_PAD
    hw = MLA_HEADS * W
    cos, sin = _rope_tables(s)
    n_pos = s // tm
    row = lambda width, blk: pl.BlockSpec((tm, width), lambda i: (i, blk))
    table = pl.BlockSpec((tm, W), lambda i: (i % n_pos, 0))
    full = lambda r, c: pl.BlockSpec((r, c), lambda i: (0, 0))
    params = pltpu.CompilerParams(dimension_semantics=("arbitrary",), vmem_limit_bytes=VMEM_LIMIT_BYTES)
    q = pl.pallas_call(
        _mla_q_body,
        grid=(m // tm,),
        in_specs=[row(MLA_Q_LORA, 0), full(1, MLA_Q_LORA), full(MLA_Q_LORA, 2 * hw), table, table],
        out_specs=row(hw, 0),
        out_shape=jax.ShapeDtypeStruct((m, hw), BF16),
        compiler_params=params,
        name="mla_q_proj",
    )(a_ext, g_q.reshape(1, -1), w_q.astype(BF16), cos, sin)
    ckv_blk = (MLA_Q_LORA + W) // MLA_KV_LORA
    ka_blk = (MLA_Q_LORA + W + MLA_KV_LORA) // W
    ckv, k, v, kpe, kpe_b = pl.pallas_call(
        _mla_kv_body,
        grid=(m // tm,),
        in_specs=[row(MLA_KV_LORA, ckv_blk), full(1, MLA_KV_LORA), full(MLA_KV_LORA, 2 * hw),
                  row(W, ka_blk), row(W, ka_blk + 1), table, table],
        out_specs=[row(MLA_KV_LORA, 0), row(hw, 0), row(hw, 0), row(W, 0), row(W, 0)],
        out_shape=[jax.ShapeDtypeStruct((m, MLA_KV_LORA), F32), jax.ShapeDtypeStruct((m, hw), BF16),
                   jax.ShapeDtypeStruct((m, hw), BF16), jax.ShapeDtypeStruct((m, W), F32),
                   jax.ShapeDtypeStruct((m, W), BF16)],
        compiler_params=params,
        name="mla_kv_proj",
    )(a_ext, g_kv.reshape(1, -1), w_kv.astype(BF16), a_ext, a_ext, cos, sin)
    return q, ckv, k, v, kpe, kpe_b


MLA_PAGES_PER_STEP = 64


def _mla_decode_body(pt_ref, q_ref, new_ref, *refs):
    del pt_ref
    npg = MLA_PAGES_PER_STEP
    pages, o_ref = refs[:npg], refs[npg]
    m_sc, l_sc, acc_sc = refs[npg + 1:]
    step = pl.program_id(1)

    @pl.when(step == 0)
    def _():
        m_sc[...] = jnp.full_like(m_sc, NEG)
        l_sc[...] = jnp.zeros_like(l_sc)
        acc_sc[...] = jnp.zeros_like(acc_sc)

    qf = q_ref[0]
    q = qf.astype(BF16)
    kt = jnp.concatenate([pages[p][0, 0].astype(BF16) for p in range(npg)], axis=1)
    s = jnp.dot(q, kt, preferred_element_type=F32)
    m = m_sc[...]
    m_new = jnp.maximum(m, jnp.max(s, axis=1, keepdims=True))
    alpha = jnp.exp(m - m_new)
    e = jnp.exp(s - m_new)
    l_new = alpha * l_sc[...] + jnp.sum(e, axis=1, keepdims=True)
    acc_new = alpha * acc_sc[...] + _dot_nt(e.astype(BF16), kt[:MLA_KV_LORA])
    m_sc[...] = m_new
    l_sc[...] = l_new
    acc_sc[...] = acc_new

    @pl.when(step == pl.num_programs(1) - 1)
    def _():
        new = new_ref[0]
        s_new = jnp.sum(qf * new, axis=1, keepdims=True)
        m_fin = jnp.maximum(m_new, s_new)
        a = jnp.exp(m_new - m_fin)
        e_new = jnp.exp(s_new - m_fin)
        o_ref[0] = (a * acc_new + e_new * new[:, :MLA_KV_LORA]) / (a * l_new + e_new)


def _mla_decode(q, new_rows, cache, layer, page_table):
    b, h, w = q.shape
    n_pages = page_table.shape[1]
    npg = MLA_PAGES_PER_STEP
    assert n_pages % npg == 0 and cache.shape[2] == PAGE_SIZE
    cache = cache.transpose(0, 1, 3, 2)

    def page_spec(p):
        return pl.BlockSpec((1, 1, w, PAGE_SIZE), lambda bi, si, pt: (layer, pt[bi, si * npg + p], 0, 0))

    grid_spec = pltpu.PrefetchScalarGridSpec(
        num_scalar_prefetch=1,
        grid=(b, n_pages // npg),
        in_specs=[pl.BlockSpec((1, h, w), lambda bi, si, pt: (bi, 0, 0)),
                  pl.BlockSpec((1, 1, w), lambda bi, si, pt: (bi, 0, 0))] + [page_spec(p) for p in range(npg)],
        out_specs=pl.BlockSpec((1, h, MLA_KV_LORA), lambda bi, si, pt: (bi, 0, 0)),
        scratch_shapes=[pltpu.VMEM((h, 1), F32), pltpu.VMEM((h, 1), F32), pltpu.VMEM((h, MLA_KV_LORA), F32)],
    )
    return pl.pallas_call(
        _mla_decode_body,
        grid_spec=grid_spec,
        out_shape=jax.ShapeDtypeStruct((b, h, MLA_KV_LORA), F32),
        compiler_params=pltpu.CompilerParams(dimension_semantics=("arbitrary", "arbitrary"),
                                             vmem_limit_bytes=VMEM_LIMIT_BYTES),
        name="mla_decode",
    )(page_table, q, new_rows, *([cache] * npg))


NSA_KV_PAIR = 2 * NSA_HEAD_DIM


def _div_pow2(x, d):
    assert d & (d - 1) == 0
    return x >> (d.bit_length() - 1)


def _alibi_col(first_head, n):
    head = first_head + lax.broadcasted_iota(jnp.int32, (n, 1), 0)
    return jnp.exp((head + 1).astype(F32) * (-8.0 / NSA_HEADS * 0.6931471805599453))


def _nsa_cmp_sample_body(q_ref, k_ref, v_ref, o_ref, idx_ref):
    nblk = k_ref.shape[2]
    pos = nblk * CMP_BLOCK
    blk_end = (lax.broadcasted_iota(jnp.int32, (1, nblk), 1) + 1) * CMP_BLOCK - 1
    dist = (pos - blk_end).astype(F32)
    imps = []
    for g in range(NSA_KV_HEADS):
        q = q_ref[0, g].astype(BF16)
        s = _dot_nt(q, k_ref[0, g]) - _alibi_col(g * NSA_GROUP, NSA_GROUP) * dist
        p = _softmax_rows(s, jnp.full(s.shape, True))
        o_ref[0, g] = jnp.dot(p.astype(BF16), v_ref[0, g], preferred_element_type=F32)
        imps.append(jnp.sum(p, axis=0, keepdims=True))
    imp = jnp.concatenate(imps, axis=0)
    jj = lax.broadcasted_iota(jnp.int32, (NSA_KV_HEADS, nblk), 1)
    score = jnp.where(jj == 0, FORCE, jnp.where(jj == nblk - 1, FORCE, imp))
    rank = jnp.zeros((NSA_KV_HEADS, nblk), jnp.int32)
    for i in range(nblk):
        ci = score[:, i:i + 1]
        rank = rank + jnp.where(ci > score, 1, jnp.where(ci == score, jnp.where(jj > i, 1, 0), 0))
    n_pick = min(TOP_K_BLOCKS, nblk + 1) - 1
    cols = [jnp.sum(jnp.where(rank == r, jj, 0), axis=1, keepdims=True) for r in range(n_pick)]
    cols.append(jnp.full((NSA_KV_HEADS, 1), nblk, jnp.int32))
    idx_ref[0] = jnp.concatenate(cols, axis=1)


def _nsa_cmp_sample(q, summ):
    b = q.shape[0]
    nblk = summ.shape[3]
    n_sel = min(TOP_K_BLOCKS, nblk + 1)
    sum_spec = pl.BlockSpec((1, NSA_KV_HEADS, nblk, NSA_HEAD_DIM), lambda i: (i, 0, 0, 0))
    return pl.pallas_call(
        _nsa_cmp_sample_body,
        grid=(b,),
        in_specs=[pl.BlockSpec((1, NSA_KV_HEADS, NSA_GROUP, NSA_HEAD_DIM), lambda i: (i, 0, 0, 0)),
                  sum_spec, sum_spec],
        out_specs=[pl.BlockSpec((1, NSA_KV_HEADS, NSA_GROUP, NSA_HEAD_DIM), lambda i: (i, 0, 0, 0)),
                   pl.BlockSpec((1, NSA_KV_HEADS, n_sel), lambda i: (i, 0, 0))],
        out_shape=[jax.ShapeDtypeStruct((b, NSA_KV_HEADS, NSA_GROUP, NSA_HEAD_DIM), F32),
                   jax.ShapeDtypeStruct((b, NSA_KV_HEADS, n_sel), jnp.int32)],
        compiler_params=pltpu.CompilerParams(dimension_semantics=("arbitrary",)),
        name="nsa_cmp_sample",
    )(q, summ[0], summ[1])


def _nsa_sel_sample_body(n_past, idx_ref, pt_ref, q_ref, new_ref, *refs):
    del pt_ref
    n_sel = (len(refs) - 1) // NSA_KV_HEADS
    pages, o_ref = refs[:-1], refs[-1]
    bi = pl.program_id(0)
    pos = n_past * SEL_BLOCK
    per_page = PAGE_SIZE // SEL_BLOCK
    tok = lax.broadcasted_iota(jnp.int32, (1, PAGE_SIZE), 1)
    for g in range(NSA_KV_HEADS):
        qf = q_ref[0, g]
        q = qf.astype(BF16)
        slope = _alibi_col(g * NSA_GROUP, NSA_GROUP)
        vts, scores = [], []
        for k in range(n_sel):
            j = idx_ref[bi, g * n_sel + k]
            kv = pages[g * n_sel + k][0].astype(BF16)
            dist = (pos - (_div_pow2(j, per_page) * PAGE_SIZE + tok)).astype(F32)
            s = jnp.dot(q, kv[:NSA_HEAD_DIM], preferred_element_type=F32) - slope * dist
            in_block = _div_pow2(tok, SEL_BLOCK) == (j & (per_page - 1))
            keep = jnp.where(j < n_past, jnp.where(in_block, 1, 0), 0) > 0
            scores.append(jnp.where(keep, s, NEG))
            vts.append(kv[NSA_HEAD_DIM:])
        new = new_ref[0, g]
        s_new = jnp.sum(qf * new[0:1], axis=1, keepdims=True)
        m = s_new
        for s in scores:
            m = jnp.maximum(m, jnp.max(s, axis=1, keepdims=True))
        e_new = jnp.exp(s_new - m)
        l = e_new
        acc = e_new * new[1:2]
        for s, vt in zip(scores, vts):
            e = jnp.exp(s - m)
            l = l + jnp.sum(e, axis=1, keepdims=True)
            acc = acc + _dot_nt(e.astype(BF16), vt)
        o_ref[0, g] = acc / l


def _nsa_channel_major(cache):
    lead = cache.ndim - 4
    perm = tuple(range(lead)) + (lead + 1, lead + 2, lead + 3, lead)
    t = cache.transpose(perm)
    return t.reshape(t.shape[:lead] + (NSA_KV_COLS, t.shape[-1]))


def _nsa_sel_sample(q, new_kv, cache_sel, layer, page_table, idx):
    b, g, n_sel = idx.shape
    n_pages = page_table.shape[1]
    per_page = PAGE_SIZE // SEL_BLOCK
    n_past = n_pages * per_page
    cache = _nsa_channel_major(cache_sel)

    def page_spec(gi, k):
        def index(bi, idx_ref, pt_ref):
            j = jnp.minimum(idx_ref[bi, gi * n_sel + k], n_past - 1)
            return (layer, pt_ref[bi, _div_pow2(j, per_page)], gi, 0)
        return pl.BlockSpec((None, 1, NSA_KV_PAIR, PAGE_SIZE), index)

    seq_spec = lambda rows: pl.BlockSpec((1, g, rows, NSA_HEAD_DIM), lambda bi, i_, p_: (bi, 0, 0, 0))
    grid_spec = pltpu.PrefetchScalarGridSpec(
        num_scalar_prefetch=2,
        grid=(b,),
        in_specs=[seq_spec(NSA_GROUP), seq_spec(2)] + [page_spec(gi, k) for gi in range(g) for k in range(n_sel)],
        out_specs=seq_spec(NSA_GROUP),
    )
    return pl.pallas_call(
        functools.partial(_nsa_sel_sample_body, n_past),
        grid_spec=grid_spec,
        out_shape=jax.ShapeDtypeStruct((b, g, NSA_GROUP, NSA_HEAD_DIM), F32),
        compiler_params=pltpu.CompilerParams(dimension_semantics=("arbitrary",)),
        name="nsa_sel_sample",
    )(idx.reshape(b, g * n_sel), page_table, q, new_kv, *([cache] * (g * n_sel)))


def _nsa_win_sample_body(q_ref, new_ref, win_ref, o_ref, nwin_ref):
    wbuf = win_ref.shape[2]
    win = win_ref[0]
    tok = lax.broadcasted_iota(jnp.int32, (1, wbuf), 1)
    dist = wbuf - tok
    valid = dist < WINDOW
    for g in range(NSA_KV_HEADS):
        r0 = g * NSA_KV_PAIR
        qf = q_ref[0, g]
        new = new_ref[0, g]
        kt = win[r0:r0 + NSA_HEAD_DIM].astype(BF16)
        vt = win[r0 + NSA_HEAD_DIM:r0 + NSA_KV_PAIR].astype(BF16)
        slope = _alibi_col(g * NSA_GROUP, NSA_GROUP)
        s = jnp.dot(qf.astype(BF16), kt, preferred_element_type=F32) - slope * dist.astype(F32)
        s = jnp.where(valid, s, NEG)
        s_new = jnp.sum(qf * new[0:1], axis=1, keepdims=True)
        m = jnp.maximum(s_new, jnp.max(s, axis=1, keepdims=True))
        e = jnp.exp(s - m)
        e_new = jnp.exp(s_new - m)
        acc = e_new * new[1:2] + _dot_nt(e.astype(BF16), vt)
        o_ref[0, g] = acc / (e_new + jnp.sum(e, axis=1, keepdims=True))
    cols = win.shape[0]
    eye = (lax.broadcasted_iota(jnp.int32, (cols, cols), 0) == lax.broadcasted_iota(jnp.int32, (cols, cols), 1))
    new_row = jnp.concatenate([new_ref[0, g][c:c + 1] for g in range(NSA_KV_HEADS) for c in range(2)], axis=1)
    new_col = jnp.sum(jnp.where(eye, new_row, 0.0), axis=1, keepdims=True)
    nwin_ref[0] = jnp.where(tok == wbuf - 1, new_col, pltpu.roll(win, wbuf - 1, 1))


def _nsa_win_sample(q, new_kv, win_state):
    b, wbuf = win_state.shape[:2]
    assert wbuf == WINDOW
    win = _nsa_channel_major(win_state)
    q_spec = pl.BlockSpec((1, NSA_KV_HEADS, NSA_GROUP, NSA_HEAD_DIM), lambda i: (i, 0, 0, 0))
    win_spec = pl.BlockSpec((1, NSA_KV_COLS, wbuf), lambda i: (i, 0, 0))
    return pl.pallas_call(
        _nsa_win_sample_body,
        grid=(b,),
        in_specs=[q_spec, pl.BlockSpec((1, NSA_KV_HEADS, 2, NSA_HEAD_DIM), lambda i: (i, 0, 0, 0)), win_spec],
        out_specs=[q_spec, win_spec],
        out_shape=[jax.ShapeDtypeStruct((b, NSA_KV_HEADS, NSA_GROUP, NSA_HEAD_DIM), F32),
                   jax.ShapeDtypeStruct((b, NSA_KV_COLS, wbuf), F32)],
        compiler_params=pltpu.CompilerParams(dimension_semantics=("arbitrary",)),
        name="nsa_win_sample",
    )(q, new_kv, win)


MLSTM_KERNEL_CHUNK = 256
MLSTM_QK_COLS = MLSTM_HEADS * MLSTM_DQK
MLSTM_V_COLS = MLSTM_HEADS * MLSTM_DV


def _log_sigmoid(x):
    return jnp.minimum(x, 0.0) - jnp.log(1.0 + jnp.exp(-jnp.abs(x)))


def _mlstm_chunk_body(q_ref, k_ref, v_ref, og_ref, vt_ref, ig_ref, fg_ref,
                      h_ref, c_ref, n_ref, m_ref, c_sc, n_sc, m_sc):
    chunk = pl.program_id(2)

    @pl.when(chunk == 0)
    def _():
        c_sc[...] = jnp.zeros_like(c_sc)
        n_sc[...] = jnp.zeros_like(n_sc)
        m_sc[...] = jnp.zeros_like(m_sc)

    L = q_ref.shape[1]
    q = q_ref[0]
    qb = q.astype(BF16)
    kb = (k_ref[0] * (MLSTM_DQK ** -0.5)).astype(BF16)
    i_row = ig_ref[0, 0, 0]
    f_row = _log_sigmoid(fg_ref[0, 0, 0])
    tt = lax.broadcasted_iota(jnp.int32, (L, L), 0)
    ss = lax.broadcasted_iota(jnp.int32, (L, L), 1)
    tri = ss <= tt
    b_col = jnp.sum(jnp.where(tri, f_row, 0.0), axis=1, keepdims=True)
    b_row = jnp.sum(jnp.where(tt == ss, b_col, 0.0), axis=0, keepdims=True)
    m = m_sc[...]
    c = c_sc[...]
    n = n_sc[...]
    d = jnp.where(tri, b_col - b_row + i_row, NEG)
    inter = b_col + m
    mt = jnp.maximum(inter, jnp.max(d, axis=1, keepdims=True))
    w = jnp.exp(d - mt)
    gq = jnp.exp(inter - mt)
    a = w * _dot_nt(qb, kb)
    num = (jnp.dot(a.astype(BF16), v_ref[0].astype(BF16), preferred_element_type=F32)
           + gq * _dot_nt(qb, c.astype(BF16)))
    den = jnp.sum(a, axis=1, keepdims=True) + gq * jnp.sum(q * n, axis=1, keepdims=True)
    hc = num / jnp.maximum(jnp.abs(den), jnp.exp(-mt))
    h_ref[0] = hc * jax.nn.sigmoid(og_ref[0])

    b_last = b_col[L - 1:L, :]
    m_new = mt[L - 1:L, :]
    wl = jnp.exp(b_last - b_row + i_row - m_new)
    gl = jnp.exp(b_last + m - m_new)
    c_new = gl * c + jnp.dot((vt_ref[0, 0] * wl).astype(BF16), kb, preferred_element_type=F32)
    wl8 = jnp.broadcast_to(wl, (8, L)).astype(BF16)
    n_new = gl * n + jnp.dot(wl8, kb, preferred_element_type=F32)[0:1]
    c_sc[...] = c_new
    n_sc[...] = n_new
    m_sc[...] = m_new

    @pl.when(chunk == pl.num_programs(2) - 1)
    def _():
        c_ref[0, 0] = c_new
        n_ref[0, 0] = n_new
        m_ref[0, 0] = m_new


def _mlstm_prompt(z, b_gate):
    B, S, _ = z.shape
    H, L = MLSTM_HEADS, MLSTM_KERNEL_CHUNK
    assert S % L == 0
    nc = S // L
    kblk, vblk = MLSTM_QK_COLS // MLSTM_DQK, (2 * MLSTM_QK_COLS) // MLSTM_DV
    gates = z[..., 2 * MLSTM_QK_COLS + 2 * MLSTM_V_COLS:] + b_gate
    gates = gates.reshape(B, nc, L, 2, H).transpose(3, 0, 4, 1, 2)[:, :, :, :, None, :]
    vt = z[..., 2 * MLSTM_QK_COLS:2 * MLSTM_QK_COLS + MLSTM_V_COLS].reshape(B, S, H, MLSTM_DV)
    vt = vt.transpose(0, 2, 3, 1)
    gate_spec = pl.BlockSpec((1, 1, 1, 1, L), lambda b, h, c: (b, h, c, 0, 0))
    state = lambda r, w: pl.BlockSpec((1, 1, r, w), lambda b, h, c: (b, h, 0, 0))
    hs, c, n, m = pl.pallas_call(
        _mlstm_chunk_body,
        grid=(B, H, nc),
        in_specs=[pl.BlockSpec((1, L, MLSTM_DQK), lambda b, h, c: (b, c, h)),
                  pl.BlockSpec((1, L, MLSTM_DQK), lambda b, h, c: (b, c, kblk + h)),
                  pl.BlockSpec((1, L, MLSTM_DV), lambda b, h, c: (b, c, vblk + h)),
                  pl.BlockSpec((1, L, MLSTM_DV), lambda b, h, c: (b, c, vblk + H + h)),
                  pl.BlockSpec((1, 1, MLSTM_DV, L), lambda b, h, c: (b, h, 0, c)),
                  gate_spec, gate_spec],
        out_specs=[pl.BlockSpec((1, L, MLSTM_DV), lambda b, h, c: (b, c, h)),
                   state(MLSTM_DV, MLSTM_DQK), state(1, MLSTM_DQK), state(1, 1)],
        out_shape=[jax.ShapeDtypeStruct((B, S, MLSTM_V_COLS), F32),
                   jax.ShapeDtypeStruct((B, H, MLSTM_DV, MLSTM_DQK), F32),
                   jax.ShapeDtypeStruct((B, H, 1, MLSTM_DQK), F32),
                   jax.ShapeDtypeStruct((B, H, 1, 1), F32)],
        scratch_shapes=[pltpu.VMEM((MLSTM_DV, MLSTM_DQK), F32), pltpu.VMEM((1, MLSTM_DQK), F32),
                        pltpu.VMEM((1, 1), F32)],
        compiler_params=pltpu.CompilerParams(dimension_semantics=("arbitrary",) * 3),
        name="mlstm_chunks",
    )(z, z, z, z, vt, gates[0], gates[1])
    return hs, c, n.reshape(B, H, MLSTM_DQK), m.reshape(B, H)


def _mlstm_step_body(q_ref, k_ref, v_ref, og_ref, ig_ref, fg_ref, c_ref, n_ref, m_ref,
                     h_ref, c_out, n_out, m_out):
    eye = (lax.broadcasted_iota(jnp.int32, (MLSTM_DV, MLSTM_DV), 0)
           == lax.broadcasted_iota(jnp.int32, (MLSTM_DV, MLSTM_DV), 1))
    for h in range(MLSTM_HEADS):
        c, n, m = c_ref[0, h], n_ref[0, h], m_ref[0, h]
        q = q_ref[0, h]
        k = k_ref[0, h] * (MLSTM_DQK ** -0.5)
        v = v_ref[0, h]
        i_g = ig_ref[0, h]
        inter = _log_sigmoid(fg_ref[0, h]) + m
        mt = jnp.maximum(inter, i_g)
        w = jnp.exp(i_g - mt)
        gq = jnp.exp(inter - mt)
        a = w * jnp.sum(q * k, axis=1, keepdims=True)
        cq_col = jnp.sum(c * q, axis=1, keepdims=True)
        cq_row = jnp.sum(jnp.where(eye, cq_col, 0.0), axis=0, keepdims=True)
        den = a + gq * jnp.sum(n * q, axis=1, keepdims=True)
        hc = (a * v + gq * cq_row) / jnp.maximum(jnp.abs(den), jnp.exp(-mt))
        h_ref[0, h] = hc * jax.nn.sigmoid(og_ref[0, h])
        v_col = jnp.sum(jnp.where(eye, v, 0.0), axis=1, keepdims=True)
        c_out[0, h] = gq * c + (w * v_col) * k
        n_out[0, h] = gq * n + w * k
        m_out[0, h] = mt


def _mlstm_sample(z, b_gate, c0, n0, m0):
    B, T, _ = z.shape
    assert T == 1
    H = MLSTM_HEADS
    z = z.reshape(B, -1)
    qk, hv = MLSTM_QK_COLS, MLSTM_V_COLS
    q = z[:, :qk].reshape(B, H, 1, MLSTM_DQK)
    k = z[:, qk:2 * qk].reshape(B, H, 1, MLSTM_DQK)
    v = z[:, 2 * qk:2 * qk + hv].reshape(B, H, 1, MLSTM_DV)
    og = z[:, 2 * qk + hv:2 * qk + 2 * hv].reshape(B, H, 1, MLSTM_DV)
    gates = z[:, 2 * qk + 2 * hv:] + b_gate
    ig = gates[:, :H].reshape(B, H, 1, 1)
    fg = gates[:, H:].reshape(B, H, 1, 1)
    spec = lambda r, w: pl.BlockSpec((1, H, r, w), lambda b: (b, 0, 0, 0))
    shapes = [(1, MLSTM_DV), (MLSTM_DV, MLSTM_DQK), (1, MLSTM_DQK), (1, 1)]
    hs, c, n, m = pl.pallas_call(
        _mlstm_step_body,
        grid=(B,),
        in_specs=[spec(1, MLSTM_DQK), spec(1, MLSTM_DQK), spec(1, MLSTM_DV), spec(1, MLSTM_DV),
                  spec(1, 1), spec(1, 1), spec(MLSTM_DV, MLSTM_DQK), spec(1, MLSTM_DQK), spec(1, 1)],
        out_specs=[spec(*s) for s in shapes],
        out_shape=[jax.ShapeDtypeStruct((B, H) + s, F32) for s in shapes],
        compiler_params=pltpu.CompilerParams(dimension_semantics=("arbitrary",)),
        name="mlstm_step",
    )(q, k, v, og, ig, fg, c0, n0.reshape(B, H, 1, MLSTM_DQK), m0.reshape(B, H, 1, 1))
    return hs.reshape(B, T, hv), c, n.reshape(B, H, MLSTM_DQK), m.reshape(B, H)


def _rope(x, pos):
    half = x.shape[-1] // 2
    freqs = ROPE_THETA ** (-jnp.arange(half, dtype=F32) / half)
    ang = pos.astype(F32)[:, None] * freqs[None, :]
    cos = jnp.cos(ang)[None, :, None, :]
    sin = jnp.sin(ang)[None, :, None, :]
    x1, x2 = x[..., :half], x[..., half:]
    return jnp.concatenate([x1 * cos - x2 * sin, x1 * sin + x2 * cos], axis=-1)


def _mla_prompt(x, g0, B, S, w_a, g_q, g_kv, w_uq, w_uk, w_uv):
    w_a_ext, w_q, w_kv = _mla_prompt_weights(w_a, w_uq, w_uk, w_uv)
    a_ext = _norm_proj(x, g0, w_a_ext)
    q, ckv, k, v, kpe, kpe_b = _mla_prompt_proj(a_ext, g_q, g_kv, w_q, w_kv, S)
    seq = lambda t: t.reshape(B, S, -1)
    o = _mla_flash(seq(q), seq(k), seq(kpe_b), seq(v))
    new_rows = jnp.concatenate([ckv, kpe[:, MLA_NOPE_DIM:MLA_NOPE_DIM + MLA_ROPE_DIM]], axis=-1)
    return o, new_rows.reshape(B, S, -1)


def _mla_sample(a, cache, j, page_table, g_q, g_kv, w_uq, w_uk, w_uv):
    B, T, _ = a.shape
    assert T == 1
    H = MLA_HEADS
    past_len = page_table.shape[1] * cache.shape[2]
    pos = past_len + jnp.arange(T, dtype=jnp.int32)
    a2 = a.reshape(B * T, -1)
    q = _norm_proj(a2[:, :MLA_Q_LORA], g_q, w_uq).reshape(B, T, H, MLA_NOPE_DIM + MLA_ROPE_DIM)
    w_kv = jnp.concatenate([w_uk.reshape(MLA_KV_LORA, -1), w_uv.reshape(MLA_KV_LORA, -1)], axis=1)
    _, ckv = _norm_proj(a2[:, MLA_Q_LORA:MLA_Q_LORA + MLA_KV_LORA], g_kv, w_kv, with_normed=True)
    kpe = _rope(a[..., MLA_Q_LORA + MLA_KV_LORA:][:, :, None, :], pos)
    q_pe = _rope(q[..., MLA_NOPE_DIM:], pos)
    new_rows = jnp.concatenate([ckv.reshape(B, T, -1), kpe[:, :, 0]], axis=-1)
    q_nope = q[:, 0, :, :MLA_NOPE_DIM].transpose(1, 0, 2)
    q_lat = _heads_matmul(q_nope, w_uk.transpose(1, 2, 0)).transpose(1, 0, 2)
    q_abs = jnp.concatenate([q_lat, q_pe[:, 0]], axis=-1) * MLA_SCALE
    o_lat = _mla_decode(q_abs, new_rows, cache, j, page_table)
    o = _heads_matmul(o_lat.transpose(1, 0, 2), w_uv.transpose(1, 0, 2))
    return o.transpose(1, 0, 2).reshape(B, T, -1), new_rows


def _nsa_split(z):
    B, T, _ = z.shape
    q = z[..., :NSA_Q_COLS].reshape(B, T, NSA_HEADS, NSA_HEAD_DIM)
    kv = z[..., NSA_Q_COLS:NSA_Q_COLS + 3 * NSA_KV_COLS].reshape(B, T, 3, NSA_KV_HEADS, 2, NSA_HEAD_DIM)
    g = jax.nn.sigmoid(z[..., NSA_Q_COLS + 3 * NSA_KV_COLS:]).reshape(B, T, NSA_HEADS, 3)
    return q, kv[:, :, 0], kv[:, :, 1], kv[:, :, 2], g


def _nsa_merge(g, o_cmp, o_sel, o_win):
    o = g[..., 0:1] * o_cmp + g[..., 1:2] * o_sel + g[..., 2:3] * o_win
    B, T = o.shape[:2]
    return o.reshape(B, T, -1)


def _nsa_seq_layout(kv):
    return kv.transpose(3, 0, 2, 1, 4).astype(BF16)


def _nsa_prompt(z, w_bd):
    B, S, _ = z.shape
    nb = S // CMP_BLOCK
    q = (z[..., :NSA_Q_COLS] * (NSA_SCALE * LOG2E)).astype(BF16)
    q = q.reshape(B, S, NSA_HEADS, NSA_HEAD_DIM).transpose(0, 2, 1, 3)
    kv = z[..., NSA_Q_COLS:NSA_Q_COLS + 3 * NSA_KV_COLS].reshape(B, S, 3, NSA_KV_HEADS, 2, NSA_HEAD_DIM)
    kv_c, kv_s, kv_w = kv[:, :, 0], kv[:, :, 1], kv[:, :, 2]
    summ = _summarize_blocks(kv_c.reshape(B * nb, CMP_BLOCK * NSA_KV_COLS), w_bd)
    summ = summ.reshape(B, nb, NSA_KV_HEADS, 2, NSA_HEAD_DIM).transpose(3, 0, 2, 1, 4).astype(BF16)
    gate = z[..., NSA_Q_COLS + 3 * NSA_KV_COLS:].reshape(B, S, NSA_KV_HEADS, 3 * NSA_GROUP).transpose(0, 2, 1, 3)
    y = _nsa_prompt_attend(q, summ, _nsa_seq_layout(kv_s), _nsa_seq_layout(kv_w), gate)
    return y, kv_c, kv_s, kv_w[:, -min(WINDOW, S):]


def _nsa_sample(z, cache_cmp, cache_sel, win_state, j, page_table, w_paged):
    B, T, _ = z.shape
    assert T == 1 and T < CMP_BLOCK
    G, Dh = NSA_KV_HEADS, NSA_HEAD_DIM
    assert cache_cmp.shape[2] == PAGE_SIZE
    n_pages = page_table.shape[1]
    q, kv_c, kv_s, kv_w, gate = _nsa_split(z)
    summ = _summarize_pool(cache_cmp[j], w_paged)[page_table]
    summ = summ.transpose(3, 0, 2, 1, 4, 5).reshape(2, B, G, n_pages * SUMM_PAGE_BLOCKS, Dh).astype(BF16)
    qg = q.reshape(B, G, NSA_GROUP, Dh) * NSA_SCALE
    o_cmp, idx = _nsa_cmp_sample(qg, summ)
    o_sel = _nsa_sel_sample(qg, kv_s.reshape(B, G, 2, Dh), cache_sel, j, page_table, idx)
    o_win, new_win = _nsa_win_sample(qg, kv_w.reshape(B, G, 2, Dh), win_state)
    heads = lambda o: o.reshape(B, T, NSA_HEADS, Dh)
    y = _nsa_merge(gate, heads(o_cmp), heads(o_sel), heads(o_win))
    new_win = new_win.reshape(B, G, 2, Dh, new_win.shape[-1]).transpose(0, 4, 1, 2, 3)
    return y, kv_c, kv_s, new_win


def kernel(x_prompt, x_sample, cache_mla_kv, state_mlstm_c, state_mlstm_n, state_mlstm_m, cache_nsa_cmp,
           cache_nsa_sel, state_nsa_win, page_table, norm_g, final_norm_g, mla_w_a, mla_g_q, mla_g_kv,
           mla_w_uq, mla_w_uk, mla_w_uv, mla_w_o, mlstm_w_in, mlstm_b_gate, mlstm_w_out, nsa_w_in,
           nsa_w_cmp, nsa_w_out, mlp_w1, mlp_w2):
    B, S, D = x_prompt.shape
    Bs, Ts, _ = x_sample.shape
    xp = x_prompt.reshape(B * S, D)
    xs = x_sample.reshape(Bs * Ts, D)
    mla_p, mla_s = [], []
    mc_p, mn_p, mm_p, mc_s, mn_s, mm_s = [], [], [], [], [], []
    cmp_p, cmp_s, sel_p, sel_s, win_p, win_s = [], [], [], [], [], []
    for i in range(DEPTH):
        j = i // N_MIXERS
        g0 = norm_g[i, 0]
        if i % N_MIXERS == 0:
            as_ = _norm_proj(xs, g0, mla_w_a[j]).reshape(Bs, Ts, -1)
            w = (mla_g_q[j], mla_g_kv[j], mla_w_uq[j], mla_w_uk[j], mla_w_uv[j])
            op, rp = _mla_prompt(xp, g0, B, S, mla_w_a[j], *w)
            os_, rs = _mla_sample(as_, cache_mla_kv, j, page_table, *w)
            mla_p.append(rp)
            mla_s.append(rs)
            w_out = mla_w_o[j]
        elif i % N_MIXERS == 1:
            zp = _norm_proj(xp, g0, mlstm_w_in[j]).reshape(B, S, -1)
            zs = _norm_proj(xs, g0, mlstm_w_in[j]).reshape(Bs, Ts, -1)
            op, cp, nst_p, mp = _mlstm_prompt(zp, mlstm_b_gate[j])
            os_, cs, nst_s, ms = _mlstm_sample(zs, mlstm_b_gate[j], state_mlstm_c[j], state_mlstm_n[j],
                                               state_mlstm_m[j])
            mc_p.append(cp)
            mn_p.append(nst_p)
            mm_p.append(mp)
            mc_s.append(cs)
            mn_s.append(nst_s)
            mm_s.append(ms)
            w_out = mlstm_w_out[j]
        else:
            zp = _norm_proj(xp, g0, nsa_w_in[j]).reshape(B, S, -1)
            zs = _norm_proj(xs, g0, nsa_w_in[j]).reshape(Bs, Ts, -1)
            op, kcp, ksp, kwp = _nsa_prompt(zp, _summ_weights(nsa_w_cmp[j]))
            os_, kcs, kss, kws = _nsa_sample(zs, cache_nsa_cmp, cache_nsa_sel, state_nsa_win[j], j,
                                             page_table, _summ_weights_paged(nsa_w_cmp[j]))
            cmp_p.append(kcp)
            cmp_s.append(kcs)
            sel_p.append(ksp)
            sel_s.append(kss)
            win_p.append(kwp)
            win_s.append(kws)
            w_out = nsa_w_out[j]
        xp = _proj_res(op.reshape(B * S, -1), w_out, xp)
        xs = _proj_res(os_.reshape(Bs * Ts, -1), w_out, xs)
        xp = _mlp_res(xp, norm_g[i, 1], mlp_w1[i], mlp_w2[i])
        xs = _mlp_res(xs, norm_g[i, 1], mlp_w1[i], mlp_w2[i])
    y_prompt = _final_norm(xp, final_norm_g).reshape(B, S, D)
    y_sample = _final_norm(xs, final_norm_g).reshape(Bs, Ts, D)
    return (y_prompt, y_sample,
            jnp.stack(mla_p), jnp.stack(mla_s),
            jnp.stack(mc_p), jnp.stack(mn_p), jnp.stack(mm_p),
            jnp.stack(mc_s), jnp.stack(mn_s), jnp.stack(mm_s),
            jnp.stack(cmp_p), jnp.stack(cmp_s),
            jnp.stack(sel_p), jnp.stack(sel_s),
            jnp.stack(win_p), jnp.stack(win_s))
```

```python
import functools

import jax
import jax.numpy as jnp
from jax import lax
from jax.experimental import pallas as pl
from jax.experimental.pallas import tpu as pltpu

F32 = jnp.float32
BF16 = jnp.bfloat16

D_MODEL = 1024
DEPTH = 4
N_MIXERS = 3
PAGE_SIZE = 128

MLA_HEADS = 16
MLA_NOPE_DIM = 64
MLA_ROPE_DIM = 32
MLA_V_DIM = 64
MLA_Q_LORA = 384
MLA_KV_LORA = 256
MLA_SCALE = (MLA_NOPE_DIM + MLA_ROPE_DIM) ** -0.5
ROPE_THETA = 10000.0

MLSTM_HEADS = 4
MLSTM_DQK = 128
MLSTM_DV = 256

NSA_HEADS = 16
NSA_KV_HEADS = 4
NSA_HEAD_DIM = 64
CMP_BLOCK = 64
SEL_BLOCK = 64
TOP_K_BLOCKS = 16
WINDOW = 512
NSA_Q_COLS = NSA_HEADS * NSA_HEAD_DIM
NSA_KV_COLS = NSA_KV_HEADS * 2 * NSA_HEAD_DIM
NSA_SCALE = NSA_HEAD_DIM ** -0.5

D_FF = 4 * D_MODEL
RMS_EPS = 1e-6
NEG = -1e30
FORCE = 1e4

VMEM_LIMIT_BYTES = 56 * 1024 * 1024
FF_CHUNK = 512


def _row_tile(m):
    for t in (512, 256, 128):
        if m % t == 0:
            return t
    return m


def _rms_rows(x, g):
    return x * lax.rsqrt(jnp.mean(x * x, axis=-1, keepdims=True) + RMS_EPS) * g


def _norm_proj_body(x_ref, g_ref, w_ref, o_ref, *h_ref):
    h = _rms_rows(x_ref[...], g_ref[...])
    o_ref[...] = jnp.dot(h.astype(BF16), w_ref[...], preferred_element_type=F32)
    if h_ref:
        h_ref[0][...] = h


def _norm_proj(x, g, w, with_normed=False):
    m, d = x.shape
    n = w.shape[1]
    tm = _row_tile(m)
    out_specs = [pl.BlockSpec((tm, n), lambda i: (i, 0))]
    out_shape = [jax.ShapeDtypeStruct((m, n), F32)]
    if with_normed:
        out_specs.append(pl.BlockSpec((tm, d), lambda i: (i, 0)))
        out_shape.append(jax.ShapeDtypeStruct((m, d), F32))
    out = pl.pallas_call(
        _norm_proj_body,
        grid=(m // tm,),
        in_specs=[pl.BlockSpec((tm, d), lambda i: (i, 0)),
                  pl.BlockSpec((1, d), lambda i: (0, 0)),
                  pl.BlockSpec((d, n), lambda i: (0, 0))],
        out_specs=out_specs,
        out_shape=out_shape,
        compiler_params=pltpu.CompilerParams(dimension_semantics=("arbitrary",),
                                             vmem_limit_bytes=VMEM_LIMIT_BYTES),
        name="norm_proj",
    )(x, g.reshape(1, d), w.astype(BF16))
    return out if with_normed else out[0]


def _proj_res_body(a_ref, w_ref, r_ref, o_ref):
    o_ref[...] = r_ref[...] + jnp.dot(a_ref[...].astype(BF16), w_ref[...], preferred_element_type=F32)


def _proj_res(a, w, res):
    m, k = a.shape
    d = w.shape[1]
    tm = _row_tile(m)
    return pl.pallas_call(
        _proj_res_body,
        grid=(m // tm,),
        in_specs=[pl.BlockSpec((tm, k), lambda i: (i, 0)),
                  pl.BlockSpec((k, d), lambda i: (0, 0)),
                  pl.BlockSpec((tm, d), lambda i: (i, 0))],
        out_specs=pl.BlockSpec((tm, d), lambda i: (i, 0)),
        out_shape=jax.ShapeDtypeStruct((m, d), F32),
        compiler_params=pltpu.CompilerParams(dimension_semantics=("arbitrary",),
                                             vmem_limit_bytes=VMEM_LIMIT_BYTES),
        name="proj_res",
    )(a, w.astype(BF16), res)


def _mlp_body(x_ref, g_ref, w1_ref, w2_ref, o_ref):
    x = x_ref[...]
    h = _rms_rows(x, g_ref[...]).astype(BF16)
    acc = x
    for c in range(D_FF // FF_CHUNK):
        a = jnp.dot(h, w1_ref[:, c * FF_CHUNK:(c + 1) * FF_CHUNK], preferred_element_type=F32)
        a = jnp.maximum(a, 0.0)
        acc = acc + jnp.dot((a * a).astype(BF16), w2_ref[c * FF_CHUNK:(c + 1) * FF_CHUNK, :],
                            preferred_element_type=F32)
    o_ref[...] = acc


def _mlp_res(x, g, w1, w2):
    m, d = x.shape
    tm = _row_tile(m)
    return pl.pallas_call(
        _mlp_body,
        grid=(m // tm,),
        in_specs=[pl.BlockSpec((tm, d), lambda i: (i, 0)),
                  pl.BlockSpec((1, d), lambda i: (0, 0)),
                  pl.BlockSpec((d, D_FF), lambda i: (0, 0)),
                  pl.BlockSpec((D_FF, d), lambda i: (0, 0))],
        out_specs=pl.BlockSpec((tm, d), lambda i: (i, 0)),
        out_shape=jax.ShapeDtypeStruct((m, d), F32),
        compiler_params=pltpu.CompilerParams(dimension_semantics=("arbitrary",),
                                             vmem_limit_bytes=VMEM_LIMIT_BYTES),
        name="mlp_res",
    )(x, g.reshape(1, d), w1.astype(BF16), w2.astype(BF16))


def _final_norm_body(x_ref, g_ref, o_ref):
    o_ref[...] = _rms_rows(x_ref[...], g_ref[...])


def _final_norm(x, g):
    m, d = x.shape
    tm = _row_tile(m)
    return pl.pallas_call(
        _final_norm_body,
        grid=(m // tm,),
        in_specs=[pl.BlockSpec((tm, d), lambda i: (i, 0)), pl.BlockSpec((1, d), lambda i: (0, 0))],
        out_specs=pl.BlockSpec((tm, d), lambda i: (i, 0)),
        out_shape=jax.ShapeDtypeStruct((m, d), F32),
        compiler_params=pltpu.CompilerParams(dimension_semantics=("arbitrary",)),
        name="final_norm",
    )(x, g.reshape(1, d))


def _dot_nt(a, b):
    return lax.dot_general(a, b, (((1,), (1,)), ((), ())), preferred_element_type=F32)


SUMM_L_PER_STEP = 8
SUMM_COLS = NSA_KV_COLS
SUMM_HALF = SUMM_COLS // 2


def _summ_weights(w_cmp):
    wk = jnp.stack([w_cmp[0], w_cmp[1], w_cmp[0], w_cmp[1]], axis=0)
    bd = jnp.einsum('kj,klde->lkdje', jnp.eye(4, dtype=F32), wk)
    return bd.reshape(CMP_BLOCK, SUMM_HALF, SUMM_HALF).astype(BF16)


def _summarize_body(x_ref, w_ref, o_ref):
    @pl.when(pl.program_id(1) == 0)
    def _():
        o_ref[...] = jnp.zeros_like(o_ref)

    lo = o_ref[:, :SUMM_HALF]
    hi = o_ref[:, SUMM_HALF:]
    for li in range(SUMM_L_PER_STEP):
        x = x_ref[:, li * SUMM_COLS:(li + 1) * SUMM_COLS].astype(BF16)
        w = w_ref[li]
        lo = lo + jnp.dot(x[:, :SUMM_HALF], w, preferred_element_type=F32)
        hi = hi + jnp.dot(x[:, SUMM_HALF:], w, preferred_element_type=F32)
    o_ref[:, :SUMM_HALF] = lo
    o_ref[:, SUMM_HALF:] = hi


def _summarize_blocks(x2d, w_bd):
    nb = x2d.shape[0]
    p = 512 if nb % 512 == 0 else nb
    step_cols = SUMM_L_PER_STEP * SUMM_COLS
    return pl.pallas_call(
        _summarize_body,
        grid=(nb // p, CMP_BLOCK // SUMM_L_PER_STEP),
        in_specs=[pl.BlockSpec((p, step_cols), lambda i, l: (i, l)),
                  pl.BlockSpec((SUMM_L_PER_STEP, SUMM_HALF, SUMM_HALF), lambda i, l: (l, 0, 0))],
        out_specs=pl.BlockSpec((p, SUMM_COLS), lambda i, l: (i, 0)),
        out_shape=jax.ShapeDtypeStruct((nb, SUMM_COLS), F32),
        compiler_params=pltpu.CompilerParams(dimension_semantics=("arbitrary", "arbitrary"),
                                             vmem_limit_bytes=VMEM_LIMIT_BYTES),
        name="nsa_summarize",
    )(x2d, w_bd)


SUMM_D_PER_STEP = 8
SUMM_PAGE_BLOCKS = PAGE_SIZE // CMP_BLOCK
SUMM_PAGE_OUT = 2 * SUMM_PAGE_BLOCKS * NSA_HEAD_DIM


def _summ_weights_paged(w_cmp):
    n = 2 * SUMM_PAGE_BLOCKS
    wk = jnp.stack([w_cmp[c] for c in range(2) for _ in range(SUMM_PAGE_BLOCKS)], axis=0)
    bd = jnp.einsum('kj,klde->dklje', jnp.eye(n, dtype=F32), wk)
    return bd.reshape(NSA_HEAD_DIM, 2 * PAGE_SIZE, SUMM_PAGE_OUT).astype(BF16)


def _rows_by_channel(x):
    p, n, lanes = x.shape
    assert n == 8 and p % 8 == 0
    x4 = x.reshape(p // 8, 8, 8, lanes)
    parts = [x4[:, i] for i in range(8)]
    sub = lax.broadcasted_iota(jnp.int32, (1, 8, lanes), 1)
    for s in (4, 2, 1):
        low = (sub & s) == 0
        nxt = list(parts)
        for i in range(8):
            if i & s == 0:
                a, b = parts[i], parts[i + s]
                nxt[i] = jnp.where(low, a, pltpu.roll(b, s, 1))
                nxt[i + s] = jnp.where(low, pltpu.roll(a, 8 - s, 1), b)
        parts = nxt
    return [t.reshape(p, lanes) for t in parts]


def _summarize_pool_body(k_ref, v_ref, w_ref, o_ref):
    dg = pl.program_id(2)

    @pl.when(dg == 0)
    def _():
        o_ref[...] = jnp.zeros_like(o_ref)

    acc = o_ref[...]
    ks = _rows_by_channel(k_ref[...])
    vs = _rows_by_channel(v_ref[...])
    for dd in range(SUMM_D_PER_STEP):
        lhs = jnp.concatenate([ks[dd], vs[dd]], axis=1).astype(BF16)
        acc = acc + jnp.dot(lhs, w_ref[dg * SUMM_D_PER_STEP + dd], preferred_element_type=F32)
    o_ref[...] = acc


def _summarize_pool(cache, w_paged):
    pool = cache.shape[0]
    cm = _nsa_channel_major(cache)
    p = 1024 if pool % 1024 == 0 else pool
    d_steps = NSA_HEAD_DIM // SUMM_D_PER_STEP
    rows_per_group = NSA_KV_PAIR // SUMM_D_PER_STEP
    out = pl.pallas_call(
        _summarize_pool_body,
        grid=(pool // p, NSA_KV_HEADS, d_steps),
        in_specs=[pl.BlockSpec((p, SUMM_D_PER_STEP, PAGE_SIZE), lambda i, g, d: (i, g * rows_per_group + d, 0)),
                  pl.BlockSpec((p, SUMM_D_PER_STEP, PAGE_SIZE),
                               lambda i, g, d: (i, g * rows_per_group + d_steps + d, 0)),
                  pl.BlockSpec((NSA_HEAD_DIM, 2 * PAGE_SIZE, SUMM_PAGE_OUT), lambda i, g, d: (0, 0, 0))],
        out_specs=pl.BlockSpec((p, SUMM_PAGE_OUT), lambda i, g, d: (i, g)),
        out_shape=jax.ShapeDtypeStruct((pool, NSA_KV_HEADS * SUMM_PAGE_OUT), F32),
        compiler_params=pltpu.CompilerParams(dimension_semantics=("arbitrary",) * 3,
                                             vmem_limit_bytes=VMEM_LIMIT_BYTES),
        name="nsa_summarize_pool",
    )(cm, cm, w_paged)
    return out.reshape(pool, NSA_KV_HEADS, 2, SUMM_PAGE_BLOCKS, NSA_HEAD_DIM)


LANES = 128
NSA_TQ = 256
NSA_TK = 512
NSA_GROUP = NSA_HEADS // NSA_KV_HEADS
NSA_BAND = WINDOW + NSA_TQ


def _softmax_rows(s, valid):
    m = jnp.max(s, axis=1, keepdims=True)
    e = jnp.where(valid, jnp.exp(s - m), 0.0)
    l = jnp.sum(e, axis=1, keepdims=True)
    return e / jnp.where(l > 0.0, l, 1.0)


LOG2E = 1.4426950408889634
MASK_BIG = 1e30
M_INIT = -0.5e30


def _nsa_prompt_body(q_ref, ksum_ref, vsum_ref, ks_ref, vs_ref, kw_ref, vw_ref, gate_ref, o_ref):
    g = pl.program_id(1)
    q0 = pl.program_id(2) * NSA_TQ
    rows = NSA_GROUP * NSA_TQ
    nblk = ksum_ref.shape[2]
    dh = NSA_HEAD_DIM
    q = q_ref[0].reshape(rows, dh)

    row = lax.broadcasted_iota(jnp.int32, (rows, 1), 0)
    tok_in_tile = row & (NSA_TQ - 1)
    qpos = q0 + tok_in_tile
    head = g * NSA_GROUP + (row >> (NSA_TQ.bit_length() - 1))
    slope = jnp.exp((head + 1).astype(F32) * (-8.0 / NSA_HEADS * 0.6931471805599453)) * LOG2E

    s = _dot_nt(q, ksum_ref[0, 0])
    blk_end = (lax.broadcasted_iota(jnp.int32, (1, nblk), 1) + 1) * CMP_BLOCK - 1
    dist = qpos - blk_end
    vis = dist >= 0
    s = jnp.where(vis, s - slope * dist.astype(F32), NEG)
    e = jnp.where(vis, jnp.exp2(s - jnp.max(s, axis=1, keepdims=True)), 0.0)
    l = jnp.sum(e, axis=1, keepdims=True)
    p = e / jnp.where(l > 0.0, l, 1.0)
    o_cmp = jnp.dot(p.astype(BF16), vsum_ref[0, 0], preferred_element_type=F32)
    imp = p[0:NSA_TQ]
    for h in range(1, NSA_GROUP):
        imp = imp + p[h * NSA_TQ:(h + 1) * NSA_TQ]

    imp_t = jnp.concatenate([imp, jnp.zeros((NSA_TQ, LANES - nblk), F32)], axis=1).T[:nblk]
    cur = (q0 + lax.broadcasted_iota(jnp.int32, (1, NSA_TQ), 1)) >> 6
    jj = lax.broadcasted_iota(jnp.int32, (nblk, NSA_TQ), 0)
    forced = jnp.where(jj == 0, 1, jnp.where(jj == cur, 1, jnp.where(jj == cur - 1, 1, 0)))
    score = jnp.where(forced > 0, FORCE, jnp.where(jj <= cur, imp_t, -FORCE))
    rank = jnp.zeros((nblk, NSA_TQ), jnp.int32)
    for i in range(nblk):
        ci = score[i:i + 1, :]
        rank = rank + jnp.where(ci > score, 1, jnp.where(ci == score, jnp.where(jj > i, 1, 0), 0))
    unpicked_t = jnp.where(rank < min(TOP_K_BLOCKS, nblk), 0.0, -1.0)
    unpicked = jnp.concatenate([unpicked_t, jnp.zeros((LANES - nblk, NSA_TQ), F32)], axis=0).T[:, :dh]
    qs = jnp.concatenate([q, jnp.concatenate([unpicked.astype(BF16)] * NSA_GROUP, axis=0)], axis=1)

    def sel_tile(kt, carry, diagonal):
        m, acc = carry
        k0 = pl.multiple_of(kt * NSA_TK, NSA_TK)
        rel = k0 - q0 + lax.broadcasted_iota(jnp.int32, (1, NSA_TK), 1)
        s = _dot_nt(qs, ks_ref[0, 0, pl.ds(k0, NSA_TK), :]) + slope * rel.astype(F32)
        if diagonal:
            s = jnp.where(rel <= tok_in_tile, s, NEG)
        m_new = jnp.maximum(m, jnp.max(s, axis=1, keepdims=True))
        e = jnp.exp2(s - m_new).astype(BF16)
        acc = jnp.exp2(m - m_new) * acc + jnp.dot(e, vs_ref[0, 0, pl.ds(k0, NSA_TK), :],
                                                  preferred_element_type=F32)
        return m_new, acc

    n_kt = (q0 + NSA_TQ + NSA_TK - 1) // NSA_TK
    init = (jnp.full((rows, 1), M_INIT, F32), jnp.zeros((rows, 2 * dh), F32))
    carry = lax.fori_loop(0, n_kt - 1, functools.partial(sel_tile, diagonal=False), init)
    _, acc = sel_tile(n_kt - 1, carry, True)
    l = acc[:, dh:dh + 1]
    o_sel = acc[:, :dh] / jnp.where(l > 0.0, l, 1.0)

    w0 = pl.multiple_of(jnp.maximum(q0 - WINDOW, 0), NSA_TQ)
    dist = qpos - (w0 + lax.broadcasted_iota(jnp.int32, (1, NSA_BAND), 1))
    valid = jnp.where(dist >= 0, jnp.where(dist < WINDOW, 1, 0), 0) > 0
    s = jnp.where(valid, _dot_nt(q, kw_ref[0, 0, pl.ds(w0, NSA_BAND), :]) - slope * dist.astype(F32), NEG)
    e = jnp.exp2(s - jnp.max(s, axis=1, keepdims=True))
    o_win = (jnp.dot(e.astype(BF16), vw_ref[0, 0, pl.ds(w0, NSA_BAND), :], preferred_element_type=F32)
             / jnp.sum(e, axis=1, keepdims=True))

    gate = jax.nn.sigmoid(gate_ref[0, 0])
    outs = []
    for h in range(NSA_GROUP):
        r = slice(h * NSA_TQ, (h + 1) * NSA_TQ)
        outs.append(gate[:, 3 * h:3 * h + 1] * o_cmp[r] + gate[:, 3 * h + 1:3 * h + 2] * o_sel[r]
                    + gate[:, 3 * h + 2:3 * h + 3] * o_win[r])
    o_ref[0] = jnp.concatenate(outs, axis=-1)


def _nsa_prompt_attend(q, summ, kv_s, kv_w, gate):
    b, h, s, dh = q.shape
    g = NSA_KV_HEADS
    nblk = summ.shape[3]
    assert s % NSA_TK == 0 and s >= NSA_BAND and nblk <= dh
    blk = jnp.arange(s, dtype=jnp.int32)[:, None] // SEL_BLOCK
    onehot = jnp.where(blk == jnp.arange(dh, dtype=jnp.int32)[None, :], MASK_BIG, 0.0).astype(BF16)
    ks = jnp.concatenate([kv_s[0], jnp.broadcast_to(onehot, (b, g, s, dh))], axis=-1)
    ones = jnp.zeros((s, dh), BF16).at[:, 0].set(1.0)
    vs = jnp.concatenate([kv_s[1], jnp.broadcast_to(ones, (b, g, s, dh))], axis=-1)
    seq_spec = lambda w: pl.BlockSpec((1, 1, s, w), lambda bi, gi, qi: (bi, gi, 0, 0))
    sum_spec = pl.BlockSpec((1, 1, nblk, dh), lambda bi, gi, qi: (bi, gi, 0, 0))
    return pl.pallas_call(
        _nsa_prompt_body,
        grid=(b, g, s // NSA_TQ),
        in_specs=[pl.BlockSpec((1, NSA_GROUP, NSA_TQ, dh), lambda bi, gi, qi: (bi, gi, qi, 0)),
                  sum_spec, sum_spec, seq_spec(2 * dh), seq_spec(2 * dh), seq_spec(dh), seq_spec(dh),
                  pl.BlockSpec((1, 1, NSA_TQ, 3 * NSA_GROUP), lambda bi, gi, qi: (bi, gi, qi, 0))],
        out_specs=pl.BlockSpec((1, NSA_TQ, NSA_GROUP * dh), lambda bi, gi, qi: (bi, qi, gi)),
        out_shape=jax.ShapeDtypeStruct((b, s, h * dh), F32),
        compiler_params=pltpu.CompilerParams(dimension_semantics=("arbitrary",) * 3,
                                             vmem_limit_bytes=VMEM_LIMIT_BYTES),
        name="nsa_prompt_attend",
    )(q, summ[0], summ[1], ks, vs, kv_w[0], kv_w[1], gate)


def _heads_matmul_body(x_ref, w_ref, o_ref):
    o_ref[0] = jnp.dot(x_ref[0].astype(BF16), w_ref[0], preferred_element_type=F32)


def _heads_matmul(x, w):
    h, m, k = x.shape
    n = w.shape[2]
    return pl.pallas_call(
        _heads_matmul_body,
        grid=(h,),
        in_specs=[pl.BlockSpec((1, m, k), lambda i: (i, 0, 0)), pl.BlockSpec((1, k, n), lambda i: (i, 0, 0))],
        out_specs=pl.BlockSpec((1, m, n), lambda i: (i, 0, 0)),
        out_shape=jax.ShapeDtypeStruct((h, m, n), F32),
        compiler_params=pltpu.CompilerParams(dimension_semantics=("arbitrary",)),
        name="heads_matmul",
    )(x, w.astype(BF16))


MLA_TILE = 512
MLA_QK_PAD = 128
MLA_HEAD_PAIR = 2


def _mla_flash_body(q_ref, k_ref, kpe_ref, v_ref, o_ref):
    qi = pl.program_id(2)
    t = MLA_TILE
    w = MLA_QK_PAD
    causal = lax.broadcasted_iota(jnp.int32, (t, t), 0) >= lax.broadcasted_iota(jnp.int32, (t, t), 1)
    qs = [q_ref[0, :, hh * w:(hh + 1) * w] for hh in range(MLA_HEAD_PAIR)]

    def step(kt, carry, diagonal):
        k0 = pl.multiple_of(kt * t, t)
        kpe = kpe_ref[0, pl.ds(k0, t), :]
        out = []
        for hh in range(MLA_HEAD_PAIR):
            m, acc = carry[hh]
            s = _dot_nt(qs[hh], k_ref[0, pl.ds(k0, t), hh * w:(hh + 1) * w] + kpe)
            if diagonal:
                s = jnp.where(causal, s, NEG)
            m_new = jnp.maximum(m, jnp.max(s, axis=1, keepdims=True))
            e = jnp.exp2(s - m_new).astype(BF16)
            acc = jnp.exp2(m - m_new) * acc + jnp.dot(e, v_ref[0, pl.ds(k0, t), hh * w:(hh + 1) * w],
                                                      preferred_element_type=F32)
            out.append((m_new, acc))
        return tuple(out)

    init = tuple((jnp.full((t, 1), M_INIT, F32), jnp.zeros((t, MLA_QK_PAD), F32)) for _ in range(MLA_HEAD_PAIR))
    carry = lax.fori_loop(0, qi, functools.partial(step, diagonal=False), init)
    carry = step(qi, carry, True)
    o_ref[0] = jnp.concatenate([acc[:, :MLA_V_DIM] / acc[:, MLA_V_DIM:MLA_V_DIM + 1] for _, acc in carry],
                               axis=-1)


def _mla_flash(q, k, kpe, v):
    b, s, hw = q.shape
    t = MLA_TILE
    pair = MLA_HEAD_PAIR * MLA_QK_PAD
    assert s % t == 0 and hw % pair == 0
    seq = lambda bi, hi, qi: (bi, 0, hi)
    return pl.pallas_call(
        _mla_flash_body,
        grid=(b, hw // pair, s // t),
        in_specs=[pl.BlockSpec((1, t, pair), lambda bi, hi, qi: (bi, qi, hi)),
                  pl.BlockSpec((1, s, pair), seq),
                  pl.BlockSpec((1, s, MLA_QK_PAD), lambda bi, hi, qi: (bi, 0, 0)),
                  pl.BlockSpec((1, s, pair), seq)],
        out_specs=pl.BlockSpec((1, t, MLA_HEAD_PAIR * MLA_V_DIM), lambda bi, hi, qi: (bi, qi, hi)),
        out_shape=jax.ShapeDtypeStruct((b, s, (hw // MLA_QK_PAD) * MLA_V_DIM), F32),
        compiler_params=pltpu.CompilerParams(dimension_semantics=("arbitrary",) * 3,
                                             vmem_limit_bytes=VMEM_LIMIT_BYTES),
        name="mla_flash",
    )(q, k, kpe, v)


def _rot_cols(w):
    half = w.shape[-1] // 2
    return jnp.concatenate([-w[..., half:], w[..., :half]], axis=-1)


def _mla_prompt_weights(w_a, w_uq, w_uk, w_uv):
    H, N, R, W = MLA_HEADS, MLA_NOPE_DIM, MLA_ROPE_DIM, MLA_QK_PAD
    d = w_a.shape[0]
    w_kpe = w_a[:, MLA_Q_LORA + MLA_KV_LORA:]
    lanes = lambda w: jnp.concatenate([jnp.zeros((d, N), F32), w, jnp.zeros((d, W - N - R), F32)], axis=1)
    w_a_ext = jnp.concatenate([w_a[:, :MLA_Q_LORA], jnp.zeros((d, W), F32),
                               w_a[:, MLA_Q_LORA:MLA_Q_LORA + MLA_KV_LORA], lanes(w_kpe), lanes(_rot_cols(w_kpe))],
                              axis=1)
    wq = w_uq.reshape(MLA_Q_LORA, H, N + R)
    zq = jnp.zeros((MLA_Q_LORA, H, W - N - R), F32)
    plain = jnp.concatenate([wq, zq], axis=-1)
    rot = jnp.concatenate([jnp.zeros((MLA_Q_LORA, H, N), F32), _rot_cols(wq[..., N:]), zq], axis=-1)
    w_q = jnp.concatenate([plain.reshape(MLA_Q_LORA, H * W), rot.reshape(MLA_Q_LORA, H * W)], axis=1)
    zk = jnp.zeros((MLA_KV_LORA, H, W - N), F32)
    zv = jnp.zeros((MLA_KV_LORA, H, W - MLA_V_DIM), F32)
    w_kv = jnp.concatenate([jnp.concatenate([w_uk, zk], axis=-1).reshape(MLA_KV_LORA, H * W),
                            jnp.concatenate([w_uv, zv], axis=-1).reshape(MLA_KV_LORA, H * W)], axis=1)
    return w_a_ext, w_q, w_kv


def _rope_tables(s):
    half = MLA_ROPE_DIM // 2
    freqs = ROPE_THETA ** (-jnp.arange(half, dtype=F32) / half)
    ang = jnp.arange(s, dtype=F32)[:, None] * freqs[None, :]
    pad = jnp.zeros((s, MLA_QK_PAD - MLA_NOPE_DIM - MLA_ROPE_DIM), F32)
    cos = jnp.concatenate([jnp.ones((s, MLA_NOPE_DIM), F32), jnp.cos(ang), jnp.cos(ang), pad], axis=1)
    sin = jnp.concatenate([jnp.zeros((s, MLA_NOPE_DIM), F32), jnp.sin(ang), jnp.sin(ang), pad], axis=1)
    return cos, sin


def _mla_q_body(x_ref, g_ref, w_ref, cos_ref, sin_ref, o_ref):
    h = _rms_rows(x_ref[...], g_ref[...]).astype(BF16)
    z = jnp.dot(h, w_ref[...], preferred_element_type=F32)
    cos = cos_ref[...] * (MLA_SCALE * LOG2E)
    sin = sin_ref[...] * (MLA_SCALE * LOG2E)
    hw = o_ref.shape[1]
    for c in range(0, hw, MLA_QK_PAD):
        o_ref[:, c:c + MLA_QK_PAD] = (z[:, c:c + MLA_QK_PAD] * cos
                                      + z[:, hw + c:hw + c + MLA_QK_PAD] * sin).astype(BF16)


def _mla_kv_body(x_ref, g_ref, w_ref, ka_ref, kb_ref, cos_ref, sin_ref, ckv_ref, k_ref, v_ref, kpe_ref, kpeb_ref):
    ckv = _rms_rows(x_ref[...], g_ref[...])
    ckv_ref[...] = ckv
    z = jnp.dot(ckv.astype(BF16), w_ref[...], preferred_element_type=F32)
    hw = k_ref.shape[1]
    k_ref[...] = z[:, :hw].astype(BF16)
    ones_col = jnp.where(lax.broadcasted_iota(jnp.int32, (1, MLA_QK_PAD), 1) == MLA_V_DIM, 1.0, 0.0)
    for c in range(0, hw, MLA_QK_PAD):
        v_ref[:, c:c + MLA_QK_PAD] = (z[:, hw + c:hw + c + MLA_QK_PAD] + ones_col).astype(BF16)
    kpe = ka_ref[...] * cos_ref[...] + kb_ref[...] * sin_ref[...]
    kpe_ref[...] = kpe
    kpeb_ref[...] = kpe.astype(BF16)


def _mla_prompt_proj(a_ext, g_q, g_kv, w_q, w_kv, s):
    m = a_ext.shape[0]
    tm = _row_tile(m)
    assert s % tm == 0
    W = MLA_QK_PAD
    hw = MLA_HEADS * W
    cos, sin = _rope_tables(s)
    n_pos = s // tm
    row = lambda width, blk: pl.BlockSpec((tm, width), lambda i: (i, blk))
    table = pl.BlockSpec((tm, W), lambda i: (i % n_pos, 0))
    full = lambda r, c: pl.BlockSpec((r, c), lambda i: (0, 0))
    params = pltpu.CompilerParams(dimension_semantics=("arbitrary",), vmem_limit_bytes=VMEM_LIMIT_BYTES)
    q = pl.pallas_call(
        _mla_q_body,
        grid=(m // tm,),
        in_specs=[row(MLA_Q_LORA, 0), full(1, MLA_Q_LORA), full(MLA_Q_LORA, 2 * hw), table, table],
        out_specs=row(hw, 0),
        out_shape=jax.ShapeDtypeStruct((m, hw), BF16),
        compiler_params=params,
        name="mla_q_proj",
    )(a_ext, g_q.reshape(1, -1), w_q.astype(BF16), cos, sin)
    ckv_blk = (MLA_Q_LORA + W) // MLA_KV_LORA
    ka_blk = (MLA_Q_LORA + W + MLA_KV_LORA) // W
    ckv, k, v, kpe, kpe_b = pl.pallas_call(
        _mla_kv_body,
        grid=(m // tm,),
        in_specs=[row(MLA_KV_LORA, ckv_blk), full(1, MLA_KV_LORA), full(MLA_KV_LORA, 2 * hw),
                  row(W, ka_blk), row(W, ka_blk + 1), table, table],
        out_specs=[row(MLA_KV_LORA, 0), row(hw, 0), row(hw, 0), row(W, 0), row(W, 0)],
        out_shape=[jax.ShapeDtypeStruct((m, MLA_KV_LORA), F32), jax.ShapeDtypeStruct((m, hw), BF16),
                   jax.ShapeDtypeStruct((m, hw), BF16), jax.ShapeDtypeStruct((m, W), F32),
                   jax.ShapeDtypeStruct((m, W), BF16)],
        compiler_params=params,
        name="mla_kv_proj",
    )(a_ext, g_kv.reshape(1, -1), w_kv.astype(BF16), a_ext, a_ext, cos, sin)
    return q, ckv, k, v, kpe, kpe_b


MLA_PAGES_PER_STEP = 64


def _mla_decode_body(pt_ref, q_ref, new_ref, *refs):
    del pt_ref
    npg = MLA_PAGES_PER_STEP
    pages, o_ref = refs[:npg], refs[npg]
    m_sc, l_sc, acc_sc = refs[npg + 1:]
    step = pl.program_id(1)

    @pl.when(step == 0)
    def _():
        m_sc[...] = jnp.full_like(m_sc, NEG)
        l_sc[...] = jnp.zeros_like(l_sc)
        acc_sc[...] = jnp.zeros_like(acc_sc)

    qf = q_ref[0]
    q = qf.astype(BF16)
    kt = jnp.concatenate([pages[p][0, 0].astype(BF16) for p in range(npg)], axis=1)
    s = jnp.dot(q, kt, preferred_element_type=F32)
    m = m_sc[...]
    m_new = jnp.maximum(m, jnp.max(s, axis=1, keepdims=True))
    alpha = jnp.exp(m - m_new)
    e = jnp.exp(s - m_new)
    l_new = alpha * l_sc[...] + jnp.sum(e, axis=1, keepdims=True)
    acc_new = alpha * acc_sc[...] + _dot_nt(e.astype(BF16), kt[:MLA_KV_LORA])
    m_sc[...] = m_new
    l_sc[...] = l_new
    acc_sc[...] = acc_new

    @pl.when(step == pl.num_programs(1) - 1)
    def _():
        new = new_ref[0]
        s_new = jnp.sum(qf * new, axis=1, keepdims=True)
        m_fin = jnp.maximum(m_new, s_new)
        a = jnp.exp(m_new - m_fin)
        e_new = jnp.exp(s_new - m_fin)
        o_ref[0] = (a * acc_new + e_new * new[:, :MLA_KV_LORA]) / (a * l_new + e_new)


def _mla_decode(q, new_rows, cache, layer, page_table):
    b, h, w = q.shape
    n_pages = page_table.shape[1]
    npg = MLA_PAGES_PER_STEP
    assert n_pages % npg == 0 and cache.shape[2] == PAGE_SIZE
    cache = cache.transpose(0, 1, 3, 2)

    def page_spec(p):
        return pl.BlockSpec((1, 1, w, PAGE_SIZE), lambda bi, si, pt: (layer, pt[bi, si * npg + p], 0, 0))

    grid_spec = pltpu.PrefetchScalarGridSpec(
        num_scalar_prefetch=1,
        grid=(b, n_pages // npg),
        in_specs=[pl.BlockSpec((1, h, w), lambda bi, si, pt: (bi, 0, 0)),
                  pl.BlockSpec((1, 1, w), lambda bi, si, pt: (bi, 0, 0))] + [page_spec(p) for p in range(npg)],
        out_specs=pl.BlockSpec((1, h, MLA_KV_LORA), lambda bi, si, pt: (bi, 0, 0)),
        scratch_shapes=[pltpu.VMEM((h, 1), F32), pltpu.VMEM((h, 1), F32), pltpu.VMEM((h, MLA_KV_LORA), F32)],
    )
    return pl.pallas_call(
        _mla_decode_body,
        grid_spec=grid_spec,
        out_shape=jax.ShapeDtypeStruct((b, h, MLA_KV_LORA), F32),
        compiler_params=pltpu.CompilerParams(dimension_semantics=("arbitrary", "arbitrary"),
                                             vmem_limit_bytes=VMEM_LIMIT_BYTES),
        name="mla_decode",
    )(page_table, q, new_rows, *([cache] * npg))


NSA_KV_PAIR = 2 * NSA_HEAD_DIM


def _div_pow2(x, d):
    assert d & (d - 1) == 0
    return x >> (d.bit_length() - 1)


def _alibi_col(first_head, n):
    head = first_head + lax.broadcasted_iota(jnp.int32, (n, 1), 0)
    return jnp.exp((head + 1).astype(F32) * (-8.0 / NSA_HEADS * 0.6931471805599453))


def _nsa_cmp_sample_body(q_ref, k_ref, v_ref, o_ref, idx_ref):
    nblk = k_ref.shape[2]
    pos = nblk * CMP_BLOCK
    blk_end = (lax.broadcasted_iota(jnp.int32, (1, nblk), 1) + 1) * CMP_BLOCK - 1
    dist = (pos - blk_end).astype(F32)
    imps = []
    for g in range(NSA_KV_HEADS):
        q = q_ref[0, g].astype(BF16)
        s = _dot_nt(q, k_ref[0, g]) - _alibi_col(g * NSA_GROUP, NSA_GROUP) * dist
        p = _softmax_rows(s, jnp.full(s.shape, True))
        o_ref[0, g] = jnp.dot(p.astype(BF16), v_ref[0, g], preferred_element_type=F32)
        imps.append(jnp.sum(p, axis=0, keepdims=True))
    imp = jnp.concatenate(imps, axis=0)
    jj = lax.broadcasted_iota(jnp.int32, (NSA_KV_HEADS, nblk), 1)
    score = jnp.where(jj == 0, FORCE, jnp.where(jj == nblk - 1, FORCE, imp))
    rank = jnp.zeros((NSA_KV_HEADS, nblk), jnp.int32)
    for i in range(nblk):
        ci = score[:, i:i + 1]
        rank = rank + jnp.where(ci > score, 1, jnp.where(ci == score, jnp.where(jj > i, 1, 0), 0))
    n_pick = min(TOP_K_BLOCKS, nblk + 1) - 1
    cols = [jnp.sum(jnp.where(rank == r, jj, 0), axis=1, keepdims=True) for r in range(n_pick)]
    cols.append(jnp.full((NSA_KV_HEADS, 1), nblk, jnp.int32))
    idx_ref[0] = jnp.concatenate(cols, axis=1)


def _nsa_cmp_sample(q, summ):
    b = q.shape[0]
    nblk = summ.shape[3]
    n_sel = min(TOP_K_BLOCKS, nblk + 1)
    sum_spec = pl.BlockSpec((1, NSA_KV_HEADS, nblk, NSA_HEAD_DIM), lambda i: (i, 0, 0, 0))
    return pl.pallas_call(
        _nsa_cmp_sample_body,
        grid=(b,),
        in_specs=[pl.BlockSpec((1, NSA_KV_HEADS, NSA_GROUP, NSA_HEAD_DIM), lambda i: (i, 0, 0, 0)),
                  sum_spec, sum_spec],
        out_specs=[pl.BlockSpec((1, NSA_KV_HEADS, NSA_GROUP, NSA_HEAD_DIM), lambda i: (i, 0, 0, 0)),
                   pl.BlockSpec((1, NSA_KV_HEADS, n_sel), lambda i: (i, 0, 0))],
        out_shape=[jax.ShapeDtypeStruct((b, NSA_KV_HEADS, NSA_GROUP, NSA_HEAD_DIM), F32),
                   jax.ShapeDtypeStruct((b, NSA_KV_HEADS, n_sel), jnp.int32)],
        compiler_params=pltpu.CompilerParams(dimension_semantics=("arbitrary",)),
        name="nsa_cmp_sample",
    )(q, summ[0], summ[1])


def _nsa_sel_sample_body(n_past, idx_ref, pt_ref, q_ref, new_ref, *refs):
    del pt_ref
    n_sel = (len(refs) - 1) // NSA_KV_HEADS
    pages, o_ref = refs[:-1], refs[-1]
    bi = pl.program_id(0)
    pos = n_past * SEL_BLOCK
    per_page = PAGE_SIZE // SEL_BLOCK
    tok = lax.broadcasted_iota(jnp.int32, (1, PAGE_SIZE), 1)
    for g in range(NSA_KV_HEADS):
        qf = q_ref[0, g]
        q = qf.astype(BF16)
        slope = _alibi_col(g * NSA_GROUP, NSA_GROUP)
        vts, scores = [], []
        for k in range(n_sel):
            j = idx_ref[bi, g * n_sel + k]
            kv = pages[g * n_sel + k][0].astype(BF16)
            dist = (pos - (_div_pow2(j, per_page) * PAGE_SIZE + tok)).astype(F32)
            s = jnp.dot(q, kv[:NSA_HEAD_DIM], preferred_element_type=F32) - slope * dist
            in_block = _div_pow2(tok, SEL_BLOCK) == (j & (per_page - 1))
            keep = jnp.where(j < n_past, jnp.where(in_block, 1, 0), 0) > 0
            scores.append(jnp.where(keep, s, NEG))
            vts.append(kv[NSA_HEAD_DIM:])
        new = new_ref[0, g]
        s_new = jnp.sum(qf * new[0:1], axis=1, keepdims=True)
        m = s_new
        for s in scores:
            m = jnp.maximum(m, jnp.max(s, axis=1, keepdims=True))
        e_new = jnp.exp(s_new - m)
        l = e_new
        acc = e_new * new[1:2]
        for s, vt in zip(scores, vts):
            e = jnp.exp(s - m)
            l = l + jnp.sum(e, axis=1, keepdims=True)
            acc = acc + _dot_nt(e.astype(BF16), vt)
        o_ref[0, g] = acc / l


def _nsa_channel_major(cache):
    lead = cache.ndim - 4
    perm = tuple(range(lead)) + (lead + 1, lead + 2, lead + 3, lead)
    t = cache.transpose(perm)
    return t.reshape(t.shape[:lead] + (NSA_KV_COLS, t.shape[-1]))


def _nsa_sel_sample(q, new_kv, cache_sel, layer, page_table, idx):
    b, g, n_sel = idx.shape
    n_pages = page_table.shape[1]
    per_page = PAGE_SIZE // SEL_BLOCK
    n_past = n_pages * per_page
    cache = _nsa_channel_major(cache_sel)

    def page_spec(gi, k):
        def index(bi, idx_ref, pt_ref):
            j = jnp.minimum(idx_ref[bi, gi * n_sel + k], n_past - 1)
            return (layer, pt_ref[bi, _div_pow2(j, per_page)], gi, 0)
        return pl.BlockSpec((None, 1, NSA_KV_PAIR, PAGE_SIZE), index)

    seq_spec = lambda rows: pl.BlockSpec((1, g, rows, NSA_HEAD_DIM), lambda bi, i_, p_: (bi, 0, 0, 0))
    grid_spec = pltpu.PrefetchScalarGridSpec(
        num_scalar_prefetch=2,
        grid=(b,),
        in_specs=[seq_spec(NSA_GROUP), seq_spec(2)] + [page_spec(gi, k) for gi in range(g) for k in range(n_sel)],
        out_specs=seq_spec(NSA_GROUP),
    )
    return pl.pallas_call(
        functools.partial(_nsa_sel_sample_body, n_past),
        grid_spec=grid_spec,
        out_shape=jax.ShapeDtypeStruct((b, g, NSA_GROUP, NSA_HEAD_DIM), F32),
        compiler_params=pltpu.CompilerParams(dimension_semantics=("arbitrary",)),
        name="nsa_sel_sample",
    )(idx.reshape(b, g * n_sel), page_table, q, new_kv, *([cache] * (g * n_sel)))


def _nsa_win_sample_body(q_ref, new_ref, win_ref, o_ref, nwin_ref):
    wbuf = win_ref.shape[2]
    win = win_ref[0]
    tok = lax.broadcasted_iota(jnp.int32, (1, wbuf), 1)
    dist = wbuf - tok
    valid = dist < WINDOW
    for g in range(NSA_KV_HEADS):
        r0 = g * NSA_KV_PAIR
        qf = q_ref[0, g]
        new = new_ref[0, g]
        kt = win[r0:r0 + NSA_HEAD_DIM].astype(BF16)
        vt = win[r0 + NSA_HEAD_DIM:r0 + NSA_KV_PAIR].astype(BF16)
        slope = _alibi_col(g * NSA_GROUP, NSA_GROUP)
        s = jnp.dot(qf.astype(BF16), kt, preferred_element_type=F32) - slope * dist.astype(F32)
        s = jnp.where(valid, s, NEG)
        s_new = jnp.sum(qf * new[0:1], axis=1, keepdims=True)
        m = jnp.maximum(s_new, jnp.max(s, axis=1, keepdims=True))
        e = jnp.exp(s - m)
        e_new = jnp.exp(s_new - m)
        acc = e_new * new[1:2] + _dot_nt(e.astype(BF16), vt)
        o_ref[0, g] = acc / (e_new + jnp.sum(e, axis=1, keepdims=True))
    cols = win.shape[0]
    eye = (lax.broadcasted_iota(jnp.int32, (cols, cols), 0) == lax.broadcasted_iota(jnp.int32, (cols, cols), 1))
    new_row = jnp.concatenate([new_ref[0, g][c:c + 1] for g in range(NSA_KV_HEADS) for c in range(2)], axis=1)
    new_col = jnp.sum(jnp.where(eye, new_row, 0.0), axis=1, keepdims=True)
    nwin_ref[0] = jnp.where(tok == wbuf - 1, new_col, pltpu.roll(win, wbuf - 1, 1))


def _nsa_win_sample(q, new_kv, win_state):
    b, wbuf = win_state.shape[:2]
    assert wbuf == WINDOW
    win = _nsa_channel_major(win_state)
    q_spec = pl.BlockSpec((1, NSA_KV_HEADS, NSA_GROUP, NSA_HEAD_DIM), lambda i: (i, 0, 0, 0))
    win_spec = pl.BlockSpec((1, NSA_KV_COLS, wbuf), lambda i: (i, 0, 0))
    return pl.pallas_call(
        _nsa_win_sample_body,
        grid=(b,),
        in_specs=[q_spec, pl.BlockSpec((1, NSA_KV_HEADS, 2, NSA_HEAD_DIM), lambda i: (i, 0, 0, 0)), win_spec],
        out_specs=[q_spec, win_spec],
        out_shape=[jax.ShapeDtypeStruct((b, NSA_KV_HEADS, NSA_GROUP, NSA_HEAD_DIM), F32),
                   jax.ShapeDtypeStruct((b, NSA_KV_COLS, wbuf), F32)],
        compiler_params=pltpu.CompilerParams(dimension_semantics=("arbitrary",)),
        name="nsa_win_sample",
    )(q, new_kv, win)


MLSTM_KERNEL_CHUNK = 256
MLSTM_QK_COLS = MLSTM_HEADS * MLSTM_DQK
MLSTM_V_COLS = MLSTM_HEADS * MLSTM_DV


def _log_sigmoid(x):
    return jnp.minimum(x, 0.0) - jnp.log(1.0 + jnp.exp(-jnp.abs(x)))


def _mlstm_chunk_body(q_ref, k_ref, v_ref, og_ref, vt_ref, ig_ref, fg_ref,
                      h_ref, c_ref, n_ref, m_ref, c_sc, n_sc, m_sc):
    chunk = pl.program_id(2)

    @pl.when(chunk == 0)
    def _():
        c_sc[...] = jnp.zeros_like(c_sc)
        n_sc[...] = jnp.zeros_like(n_sc)
        m_sc[...] = jnp.zeros_like(m_sc)

    L = q_ref.shape[1]
    q = q_ref[0]
    qb = q.astype(BF16)
    kb = (k_ref[0] * (MLSTM_DQK ** -0.5)).astype(BF16)
    i_row = ig_ref[0, 0, 0]
    f_row = _log_sigmoid(fg_ref[0, 0, 0])
    tt = lax.broadcasted_iota(jnp.int32, (L, L), 0)
    ss = lax.broadcasted_iota(jnp.int32, (L, L), 1)
    tri = ss <= tt
    b_col = jnp.sum(jnp.where(tri, f_row, 0.0), axis=1, keepdims=True)
    b_row = jnp.sum(jnp.where(tt == ss, b_col, 0.0), axis=0, keepdims=True)
    m = m_sc[...]
    c = c_sc[...]
    n = n_sc[...]
    d = jnp.where(tri, b_col - b_row + i_row, NEG)
    inter = b_col + m
    mt = jnp.maximum(inter, jnp.max(d, axis=1, keepdims=True))
    w = jnp.exp(d - mt)
    gq = jnp.exp(inter - mt)
    a = w * _dot_nt(qb, kb)
    num = (jnp.dot(a.astype(BF16), v_ref[0].astype(BF16), preferred_element_type=F32)
           + gq * _dot_nt(qb, c.astype(BF16)))
    den = jnp.sum(a, axis=1, keepdims=True) + gq * jnp.sum(q * n, axis=1, keepdims=True)
    hc = num / jnp.maximum(jnp.abs(den), jnp.exp(-mt))
    h_ref[0] = hc * jax.nn.sigmoid(og_ref[0])

    b_last = b_col[L - 1:L, :]
    m_new = mt[L - 1:L, :]
    wl = jnp.exp(b_last - b_row + i_row - m_new)
    gl = jnp.exp(b_last + m - m_new)
    c_new = gl * c + jnp.dot((vt_ref[0, 0] * wl).astype(BF16), kb, preferred_element_type=F32)
    wl8 = jnp.broadcast_to(wl, (8, L)).astype(BF16)
    n_new = gl * n + jnp.dot(wl8, kb, preferred_element_type=F32)[0:1]
    c_sc[...] = c_new
    n_sc[...] = n_new
    m_sc[...] = m_new

    @pl.when(chunk == pl.num_programs(2) - 1)
    def _():
        c_ref[0, 0] = c_new
        n_ref[0, 0] = n_new
        m_ref[0, 0] = m_new


def _mlstm_prompt(z, b_gate):
    B, S, _ = z.shape
    H, L = MLSTM_HEADS, MLSTM_KERNEL_CHUNK
    assert S % L == 0
    nc = S // L
    kblk, vblk = MLSTM_QK_COLS // MLSTM_DQK, (2 * MLSTM_QK_COLS) // MLSTM_DV
    gates = z[..., 2 * MLSTM_QK_COLS + 2 * MLSTM_V_COLS:] + b_gate
    gates = gates.reshape(B, nc, L, 2, H).transpose(3, 0, 4, 1, 2)[:, :, :, :, None, :]
    vt = z[..., 2 * MLSTM_QK_COLS:2 * MLSTM_QK_COLS + MLSTM_V_COLS].reshape(B, S, H, MLSTM_DV)
    vt = vt.transpose(0, 2, 3, 1)
    gate_spec = pl.BlockSpec((1, 1, 1, 1, L), lambda b, h, c: (b, h, c, 0, 0))
    state = lambda r, w: pl.BlockSpec((1, 1, r, w), lambda b, h, c: (b, h, 0, 0))
    hs, c, n, m = pl.pallas_call(
        _mlstm_chunk_body,
        grid=(B, H, nc),
        in_specs=[pl.BlockSpec((1, L, MLSTM_DQK), lambda b, h, c: (b, c, h)),
                  pl.BlockSpec((1, L, MLSTM_DQK), lambda b, h, c: (b, c, kblk + h)),
                  pl.BlockSpec((1, L, MLSTM_DV), lambda b, h, c: (b, c, vblk + h)),
                  pl.BlockSpec((1, L, MLSTM_DV), lambda b, h, c: (b, c, vblk + H + h)),
                  pl.BlockSpec((1, 1, MLSTM_DV, L), lambda b, h, c: (b, h, 0, c)),
                  gate_spec, gate_spec],
        out_specs=[pl.BlockSpec((1, L, MLSTM_DV), lambda b, h, c: (b, c, h)),
                   state(MLSTM_DV, MLSTM_DQK), state(1, MLSTM_DQK), state(1, 1)],
        out_shape=[jax.ShapeDtypeStruct((B, S, MLSTM_V_COLS), F32),
                   jax.ShapeDtypeStruct((B, H, MLSTM_DV, MLSTM_DQK), F32),
                   jax.ShapeDtypeStruct((B, H, 1, MLSTM_DQK), F32),
                   jax.ShapeDtypeStruct((B, H, 1, 1), F32)],
        scratch_shapes=[pltpu.VMEM((MLSTM_DV, MLSTM_DQK), F32), pltpu.VMEM((1, MLSTM_DQK), F32),
                        pltpu.VMEM((1, 1), F32)],
        compiler_params=pltpu.CompilerParams(dimension_semantics=("arbitrary",) * 3),
        name="mlstm_chunks",
    )(z, z, z, z, vt, gates[0], gates[1])
    return hs, c, n.reshape(B, H, MLSTM_DQK), m.reshape(B, H)


def _mlstm_step_body(q_ref, k_ref, v_ref, og_ref, ig_ref, fg_ref, c_ref, n_ref, m_ref,
                     h_ref, c_out, n_out, m_out):
    eye = (lax.broadcasted_iota(jnp.int32, (MLSTM_DV, MLSTM_DV), 0)
           == lax.broadcasted_iota(jnp.int32, (MLSTM_DV, MLSTM_DV), 1))
    for h in range(MLSTM_HEADS):
        c, n, m = c_ref[0, h], n_ref[0, h], m_ref[0, h]
        q = q_ref[0, h]
        k = k_ref[0, h] * (MLSTM_DQK ** -0.5)
        v = v_ref[0, h]
        i_g = ig_ref[0, h]
        inter = _log_sigmoid(fg_ref[0, h]) + m
        mt = jnp.maximum(inter, i_g)
        w = jnp.exp(i_g - mt)
        gq = jnp.exp(inter - mt)
        a = w * jnp.sum(q * k, axis=1, keepdims=True)
        cq_col = jnp.sum(c * q, axis=1, keepdims=True)
        cq_row = jnp.sum(jnp.where(eye, cq_col, 0.0), axis=0, keepdims=True)
        den = a + gq * jnp.sum(n * q, axis=1, keepdims=True)
        hc = (a * v + gq * cq_row) / jnp.maximum(jnp.abs(den), jnp.exp(-mt))
        h_ref[0, h] = hc * jax.nn.sigmoid(og_ref[0, h])
        v_col = jnp.sum(jnp.where(eye, v, 0.0), axis=1, keepdims=True)
        c_out[0, h] = gq * c + (w * v_col) * k
        n_out[0, h] = gq * n + w * k
        m_out[0, h] = mt


def _mlstm_sample(z, b_gate, c0, n0, m0):
    B, T, _ = z.shape
    assert T == 1
    H = MLSTM_HEADS
    z = z.reshape(B, -1)
    qk, hv = MLSTM_QK_COLS, MLSTM_V_COLS
    q = z[:, :qk].reshape(B, H, 1, MLSTM_DQK)
    k = z[:, qk:2 * qk].reshape(B, H, 1, MLSTM_DQK)
    v = z[:, 2 * qk:2 * qk + hv].reshape(B, H, 1, MLSTM_DV)
    og = z[:, 2 * qk + hv:2 * qk + 2 * hv].reshape(B, H, 1, MLSTM_DV)
    gates = z[:, 2 * qk + 2 * hv:] + b_gate
    ig = gates[:, :H].reshape(B, H, 1, 1)
    fg = gates[:, H:].reshape(B, H, 1, 1)
    spec = lambda r, w: pl.BlockSpec((1, H, r, w), lambda b: (b, 0, 0, 0))
    shapes = [(1, MLSTM_DV), (MLSTM_DV, MLSTM_DQK), (1, MLSTM_DQK), (1, 1)]
    hs, c, n, m = pl.pallas_call(
        _mlstm_step_body,
        grid=(B,),
        in_specs=[spec(1, MLSTM_DQK), spec(1, MLSTM_DQK), spec(1, MLSTM_DV), spec(1, MLSTM_DV),
                  spec(1, 1), spec(1, 1), spec(MLSTM_DV, MLSTM_DQK), spec(1, MLSTM_DQK), spec(1, 1)],
        out_specs=[spec(*s) for s in shapes],
        out_shape=[jax.ShapeDtypeStruct((B, H) + s, F32) for s in shapes],
        compiler_params=pltpu.CompilerParams(dimension_semantics=("arbitrary",)),
        name="mlstm_step",
    )(q, k, v, og, ig, fg, c0, n0.reshape(B, H, 1, MLSTM_DQK), m0.reshape(B, H, 1, 1))
    return hs.reshape(B, T, hv), c, n.reshape(B, H, MLSTM_DQK), m.reshape(B, H)


def _rope(x, pos):
    half = x.shape[-1] // 2
    freqs = ROPE_THETA ** (-jnp.arange(half, dtype=F32) / half)
    ang = pos.astype(F32)[:, None] * freqs[None, :]
    cos = jnp.cos(ang)[None, :, None, :]
    sin = jnp.sin(ang)[None, :, None, :]
    x1, x2 = x[..., :half], x[..., half:]
    return jnp.concatenate([x1 * cos - x2 * sin, x1 * sin + x2 * cos], axis=-1)


def _mla_prompt(x, g0, B, S, w_a, g_q, g_kv, w_uq, w_uk, w_uv):
    w_a_ext, w_q, w_kv = _mla_prompt_weights(w_a, w_uq, w_uk, w_uv)
    a_ext = _norm_proj(x, g0, w_a_ext)
    q, ckv, k, v, kpe, kpe_b = _mla_prompt_proj(a_ext, g_q, g_kv, w_q, w_kv, S)
    seq = lambda t: t.reshape(B, S, -1)
    o = _mla_flash(seq(q), seq(k), seq(kpe_b), seq(v))
    new_rows = jnp.concatenate([ckv, kpe[:, MLA_NOPE_DIM:MLA_NOPE_DIM + MLA_ROPE_DIM]], axis=-1)
    return o, new_rows.reshape(B, S, -1)


def _mla_sample(a, cache, j, page_table, g_q, g_kv, w_uq, w_uk, w_uv):
    B, T, _ = a.shape
    assert T == 1
    H = MLA_HEADS
    past_len = page_table.shape[1] * cache.shape[2]
    pos = past_len + jnp.arange(T, dtype=jnp.int32)
    a2 = a.reshape(B * T, -1)
    q = _norm_proj(a2[:, :MLA_Q_LORA], g_q, w_uq).reshape(B, T, H, MLA_NOPE_DIM + MLA_ROPE_DIM)
    w_kv = jnp.concatenate([w_uk.reshape(MLA_KV_LORA, -1), w_uv.reshape(MLA_KV_LORA, -1)], axis=1)
    _, ckv = _norm_proj(a2[:, MLA_Q_LORA:MLA_Q_LORA + MLA_KV_LORA], g_kv, w_kv, with_normed=True)
    kpe = _rope(a[..., MLA_Q_LORA + MLA_KV_LORA:][:, :, None, :], pos)
    q_pe = _rope(q[..., MLA_NOPE_DIM:], pos)
    new_rows = jnp.concatenate([ckv.reshape(B, T, -1), kpe[:, :, 0]], axis=-1)
    q_nope = q[:, 0, :, :MLA_NOPE_DIM].transpose(1, 0, 2)
    q_lat = _heads_matmul(q_nope, w_uk.transpose(1, 2, 0)).transpose(1, 0, 2)
    q_abs = jnp.concatenate([q_lat, q_pe[:, 0]], axis=-1) * MLA_SCALE
    o_lat = _mla_decode(q_abs, new_rows, cache, j, page_table)
    o = _heads_matmul(o_lat.transpose(1, 0, 2), w_uv.transpose(1, 0, 2))
    return o.transpose(1, 0, 2).reshape(B, T, -1), new_rows


def _nsa_split(z):
    B, T, _ = z.shape
    q = z[..., :NSA_Q_COLS].reshape(B, T, NSA_HEADS, NSA_HEAD_DIM)
    kv = z[..., NSA_Q_COLS:NSA_Q_COLS + 3 * NSA_KV_COLS].reshape(B, T, 3, NSA_KV_HEADS, 2, NSA_HEAD_DIM)
    g = jax.nn.sigmoid(z[..., NSA_Q_COLS + 3 * NSA_KV_COLS:]).reshape(B, T, NSA_HEADS, 3)
    return q, kv[:, :, 0], kv[:, :, 1], kv[:, :, 2], g


def _nsa_merge(g, o_cmp, o_sel, o_win):
    o = g[..., 0:1] * o_cmp + g[..., 1:2] * o_sel + g[..., 2:3] * o_win
    B, T = o.shape[:2]
    return o.reshape(B, T, -1)


def _nsa_seq_layout(kv):
    return kv.transpose(3, 0, 2, 1, 4).astype(BF16)


def _nsa_prompt(z, w_bd):
    B, S, _ = z.shape
    nb = S // CMP_BLOCK
    q = (z[..., :NSA_Q_COLS] * (NSA_SCALE * LOG2E)).astype(BF16)
    q = q.reshape(B, S, NSA_HEADS, NSA_HEAD_DIM).transpose(0, 2, 1, 3)
    kv = z[..., NSA_Q_COLS:NSA_Q_COLS + 3 * NSA_KV_COLS].reshape(B, S, 3, NSA_KV_HEADS, 2, NSA_HEAD_DIM)
    kv_c, kv_s, kv_w = kv[:, :, 0], kv[:, :, 1], kv[:, :, 2]
    summ = _summarize_blocks(kv_c.reshape(B * nb, CMP_BLOCK * NSA_KV_COLS), w_bd)
    summ = summ.reshape(B, nb, NSA_KV_HEADS, 2, NSA_HEAD_DIM).transpose(3, 0, 2, 1, 4).astype(BF16)
    gate = z[..., NSA_Q_COLS + 3 * NSA_KV_COLS:].reshape(B, S, NSA_KV_HEADS, 3 * NSA_GROUP).transpose(0, 2, 1, 3)
    y = _nsa_prompt_attend(q, summ, _nsa_seq_layout(kv_s), _nsa_seq_layout(kv_w), gate)
    return y, kv_c, kv_s, kv_w[:, -min(WINDOW, S):]


def _nsa_sample(z, cache_cmp, cache_sel, win_state, j, page_table, w_paged):
    B, T, _ = z.shape
    assert T == 1 and T < CMP_BLOCK
    G, Dh = NSA_KV_HEADS, NSA_HEAD_DIM
    assert cache_cmp.shape[2] == PAGE_SIZE
    n_pages = page_table.shape[1]
    q, kv_c, kv_s, kv_w, gate = _nsa_split(z)
    summ = _summarize_pool(cache_cmp[j], w_paged)[page_table]
    summ = summ.transpose(3, 0, 2, 1, 4, 5).reshape(2, B, G, n_pages * SUMM_PAGE_BLOCKS, Dh).astype(BF16)
    qg = q.reshape(B, G, NSA_GROUP, Dh) * NSA_SCALE
    o_cmp, idx = _nsa_cmp_sample(qg, summ)
    o_sel = _nsa_sel_sample(qg, kv_s.reshape(B, G, 2, Dh), cache_sel, j, page_table, idx)
    o_win, new_win = _nsa_win_sample(qg, kv_w.reshape(B, G, 2, Dh), win_state)
    heads = lambda o: o.reshape(B, T, NSA_HEADS, Dh)
    y = _nsa_merge(gate, heads(o_cmp), heads(o_sel), heads(o_win))
    new_win = new_win.reshape(B, G, 2, Dh, new_win.shape[-1]).transpose(0, 4, 1, 2, 3)
    return y, kv_c, kv_s, new_win


def kernel(x_prompt, x_sample, cache_mla_kv, state_mlstm_c, state_mlstm_n, state_mlstm_m, cache_nsa_cmp,
           cache_nsa_sel, state_nsa_win, page_table, norm_g, final_norm_g, mla_w_a, mla_g_q, mla_g_kv,
           mla_w_uq, mla_w_uk, mla_w_uv, mla_w_o, mlstm_w_in, mlstm_b_gate, mlstm_w_out, nsa_w_in,
           nsa_w_cmp, nsa_w_out, mlp_w1, mlp_w2):
    B, S, D = x_prompt.shape
    Bs, Ts, _ = x_sample.shape
    xp = x_prompt.reshape(B * S, D)
    xs = x_sample.reshape(Bs * Ts, D)
    mla_p, mla_s = [], []
    mc_p, mn_p, mm_p, mc_s, mn_s, mm_s = [], [], [], [], [], []
    cmp_p, cmp_s, sel_p, sel_s, win_p, win_s = [], [], [], [], [], []
    for i in range(DEPTH):
        j = i // N_MIXERS
        g0 = norm_g[i, 0]
        if i % N_MIXERS == 0:
            as_ = _norm_proj(xs, g0, mla_w_a[j]).reshape(Bs, Ts, -1)
            w = (mla_g_q[j], mla_g_kv[j], mla_w_uq[j], mla_w_uk[j], mla_w_uv[j])
            op, rp = _mla_prompt(xp, g0, B, S, mla_w_a[j], *w)
            os_, rs = _mla_sample(as_, cache_mla_kv, j, page_table, *w)
            mla_p.append(rp)
            mla_s.append(rs)
            w_out = mla_w_o[j]
        elif i % N_MIXERS == 1:
            zp = _norm_proj(xp, g0, mlstm_w_in[j]).reshape(B, S, -1)
            zs = _norm_proj(xs, g0, mlstm_w_in[j]).reshape(Bs, Ts, -1)
            op, cp, nst_p, mp = _mlstm_prompt(zp, mlstm_b_gate[j])
            os_, cs, nst_s, ms = _mlstm_sample(zs, mlstm_b_gate[j], state_mlstm_c[j], state_mlstm_n[j],
                                               state_mlstm_m[j])
            mc_p.append(cp)
            mn_p.append(nst_p)
            mm_p.append(mp)
            mc_s.append(cs)
            mn_s.append(nst_s)
            mm_s.append(ms)
            w_out = mlstm_w_out[j]
        else:
            zp = _norm_proj(xp, g0, nsa_w_in[j]).reshape(B, S, -1)
            zs = _norm_proj(xs, g0, nsa_w_in[j]).reshape(Bs, Ts, -1)
            op, kcp, ksp, kwp = _nsa_prompt(zp, _summ_weights(nsa_w_cmp[j]))
            os_, kcs, kss, kws = _nsa_sample(zs, cache_nsa_cmp, cache_nsa_sel, state_nsa_win[j], j,
                                             page_table, _summ_weights_paged(nsa_w_cmp[j]))
            cmp_p.append(kcp)
            cmp_s.append(kcs)
            sel_p.append(ksp)
            sel_s.append(kss)
            win_p.append(kwp)
            win_s.append(kws)
            w_out = nsa_w_out[j]
        xp = _proj_res(op.reshape(B * S, -1), w_out, xp)
        xs = _proj_res(os_.reshape(Bs * Ts, -1), w_out, xs)
        xp = _mlp_res(xp, norm_g[i, 1], mlp_w1[i], mlp_w2[i])
        xs = _mlp_res(xs, norm_g[i, 1], mlp_w1[i], mlp_w2[i])
    y_prompt = _final_norm(xp, final_norm_g).reshape(B, S, D)
    y_sample = _final_norm(xs, final_norm_g).reshape(Bs, Ts, D)
    return (y_prompt, y_sample,
            jnp.stack(mla_p), jnp.stack(mla_s),
            jnp.stack(mc_p), jnp.stack(mn_p), jnp.stack(mm_p),
            jnp.stack(mc_s), jnp.stack(mn_s), jnp.stack(mm_s),
            jnp.stack(cmp_p), jnp.stack(cmp_s),
            jnp.stack(sel_p), jnp.stack(sel_s),
            jnp.stack(win_p), jnp.stack(win_s))
```
